```python
import math
import jax
import jax.numpy as jnp
from jax import lax
import numpy as np

D_MODEL = 4096
BATCH = 4
SEQ = 4096
DEPTH = 1

SSM_WIDTH = D_MODEL // 2
SSM_GROUP = 16
SSM_GROUPS = SSM_WIDTH // SSM_GROUP
SSM_STATE = 64
HEAD_DIM = 64
N_Q_HEADS = D_MODEL // 128
N_KV_HEADS = N_Q_HEADS // 8
Q_PER_KV = N_Q_HEADS // N_KV_HEADS
ATTN_WIDTH = N_Q_HEADS * HEAD_DIM
KV_WIDTH = N_KV_HEADS * HEAD_DIM
WINDOW = 128
ATTN_BLOCK = 128
ALIBI_MAX_BIAS = 8.0
N_EXPERTS = 64
N_EXPERT_GROUPS = 8
TOPK_GROUPS = 4
TOP_K = 8
EXPERT_FF = 512
SHARED_FF = 512
ROUTED_SCALE = 2.5
MOE_BLOCK = 128
PLE_DIM = 256
EPS = 1e-6
IN_COLS = SSM_WIDTH + ATTN_WIDTH + 2 * KV_WIDTH + 2 * D_MODEL

kernel_name = "hybrid_s5_swa_sink_moe_block"


def _rms_f32(x, g):
    xf = x.astype(jnp.float32)
    xf = xf * lax.rsqrt(jnp.mean(xf * xf, axis=-1, keepdims=True) + EPS)
    return xf * g.astype(jnp.float32)


def rms_norm(x, g):
    return _rms_f32(x, g).astype(x.dtype)


def swiglu(x, w_gate, w_up, w_down):
    return (jax.nn.silu(x @ w_gate) * (x @ w_up)) @ w_down


def s5_mixer(u, a_re, a_im, log_step, b_re, b_im, c_re, c_im, d_skip, w_glu_a, w_glu_b):
    bsz, seq, _ = u.shape
    uf = u.astype(jnp.float32).reshape(bsz, seq, SSM_GROUPS, SSM_GROUP)
    lam = lax.complex(a_re.astype(jnp.float32), a_im.astype(jnp.float32))
    step = jnp.exp(log_step.astype(jnp.float32))[:, None]
    lam_bar = jnp.exp(lam * step)
    b_mat = lax.complex(b_re.astype(jnp.float32), b_im.astype(jnp.float32))
    b_bar = ((lam_bar - 1.0) / lam)[..., None] * b_mat
    c_mat = lax.complex(c_re.astype(jnp.float32), c_im.astype(jnp.float32))
    bu = jnp.einsum('blgc,gnc->blgn', uf.astype(jnp.complex64), b_bar)
    a_seq = jnp.broadcast_to(lam_bar, (1, seq) + lam_bar.shape)

    def combine(left, right):
        a_l, b_l = left
        a_r, b_r = right
        return a_r * a_l, a_r * b_l + b_r

    _, states = lax.associative_scan(combine, (a_seq, bu), axis=1)
    y = jnp.einsum('blgn,gcn->blgc', states, c_mat).real + d_skip.astype(jnp.float32).reshape(SSM_GROUPS, SSM_GROUP) * uf
    y = jax.nn.gelu(y.reshape(bsz, seq, SSM_WIDTH)).astype(u.dtype)
    return (y @ w_glu_a) * jax.nn.sigmoid(y @ w_glu_b)


def swa_sink_attention(q, k, v, q_gain, k_gain, sinks):
    bsz, seq, _ = q.shape
    nb = seq // ATTN_BLOCK
    qb = _rms_f32(q.reshape(bsz, seq, N_KV_HEADS, Q_PER_KV, HEAD_DIM), q_gain).reshape(bsz, nb, ATTN_BLOCK, N_KV_HEADS, Q_PER_KV, HEAD_DIM)
    kb = _rms_f32(k.reshape(bsz, seq, N_KV_HEADS, HEAD_DIM), k_gain).reshape(bsz, nb, ATTN_BLOCK, N_KV_HEADS, HEAD_DIM)
    vb = v.reshape(bsz, nb, ATTN_BLOCK, N_KV_HEADS, HEAD_DIM)

    def with_prev(t):
        prev = jnp.concatenate([jnp.zeros_like(t[:, :1]), t[:, :-1]], axis=1)
        return jnp.concatenate([prev, t], axis=2)

    kk = with_prev(kb)
    vv = with_prev(vb)
    scores = jnp.einsum('bnqhgd,bnkhd->bnhgqk', qb, kk) * (HEAD_DIM ** -0.5)
    qi = jnp.arange(ATTN_BLOCK)[:, None]
    kj = jnp.arange(2 * ATTN_BLOCK)[None, :]
    dist = qi + ATTN_BLOCK - kj
    slopes = jnp.exp2(-ALIBI_MAX_BIAS * jnp.arange(1, N_Q_HEADS + 1, dtype=jnp.float32) / N_Q_HEADS).reshape(N_KV_HEADS, Q_PER_KV)
    scores = scores - slopes[:, :, None, None] * dist.astype(jnp.float32)
    key_pos = jnp.arange(nb)[:, None, None] * ATTN_BLOCK + kj[None] - ATTN_BLOCK
    valid = ((dist >= 0) & (dist < WINDOW))[None] & (key_pos >= 0)
    scores = jnp.where(valid[None, :, None, None], scores, -jnp.inf)
    sink = jnp.broadcast_to(sinks.astype(jnp.float32).reshape(N_KV_HEADS, Q_PER_KV)[None, None, :, :, None, None], scores.shape[:-1] + (1,))
    probs = jax.nn.softmax(jnp.concatenate([scores, sink], axis=-1), axis=-1)[..., :-1]
    out = jnp.einsum('bnhgqk,bnkhd->bnqhgd', probs.astype(v.dtype), vv)
    return out.reshape(bsz, seq, ATTN_WIDTH)


def route(hn, w_router, router_bias):
    n_tok = hn.shape[0]
    s = jax.nn.sigmoid(hn.astype(jnp.float32) @ w_router.astype(jnp.float32))
    sel = s + router_bias.astype(jnp.float32)
    grp = sel.reshape(n_tok, N_EXPERT_GROUPS, N_EXPERTS // N_EXPERT_GROUPS)
    grp_score = lax.top_k(grp, 2)[0].sum(axis=-1)
    _, gidx = lax.top_k(grp_score, TOPK_GROUPS)
    gmask = jax.nn.one_hot(gidx, N_EXPERT_GROUPS, dtype=jnp.float32).sum(axis=1) > 0
    emask = jnp.repeat(gmask, N_EXPERTS // N_EXPERT_GROUPS, axis=1)
    _, eidx = lax.top_k(jnp.where(emask, sel, -jnp.inf), TOP_K)
    w = jnp.take_along_axis(s, eidx, axis=1)
    w = w / jnp.sum(w, axis=-1, keepdims=True) * ROUTED_SCALE
    return eidx, w


def routed_experts(hn, eidx, ew, we_gate, we_up, we_down):
    n_tok, d = hn.shape
    n_rows = n_tok * TOP_K
    n_blocks = -(-n_rows // MOE_BLOCK) + N_EXPERTS
    cap = n_blocks * MOE_BLOCK
    flat_e = eidx.reshape(-1)
    order = jnp.argsort(flat_e)
    sorted_e = flat_e[order]
    counts = jnp.bincount(flat_e, length=N_EXPERTS)
    padded = (counts + MOE_BLOCK - 1) // MOE_BLOCK * MOE_BLOCK
    pad_end = jnp.cumsum(padded)
    pad_start = pad_end - padded
    start = jnp.cumsum(counts) - counts
    dest = pad_start[sorted_e] + jnp.arange(n_rows, dtype=jnp.int32) - start[sorted_e]
    row_tok = jnp.full((cap,), n_tok, jnp.int32).at[dest].set((order // TOP_K).astype(jnp.int32))
    row_w = jnp.zeros((cap,), jnp.float32).at[dest].set(ew.reshape(-1)[order])
    block_expert = jnp.minimum(jnp.searchsorted(pad_end // MOE_BLOCK, jnp.arange(n_blocks), side='right'), N_EXPERTS - 1)
    h_pad = jnp.concatenate([hn, jnp.zeros((1, d), hn.dtype)], axis=0)

    def block_step(acc, blk):
        rows, wts, e = blk
        yb = swiglu(h_pad[rows], we_gate[e], we_up[e], we_down[e])
        return acc.at[rows].add(yb.astype(jnp.float32) * wts[:, None]), None

    acc, _ = lax.scan(block_step, jnp.zeros((n_tok + 1, d), jnp.float32),
                      (row_tok.reshape(n_blocks, MOE_BLOCK), row_w.reshape(n_blocks, MOE_BLOCK), block_expert))
    return acc[:n_tok].astype(hn.dtype)


def setup_inputs(seed: int = 0) -> dict:
    key = jax.random.key(seed)
    ks = jax.random.split(key, 40)
    f32 = jnp.float32

    def nrm(k, shape, scale):
        return jax.random.normal(k, shape, f32) * scale

    def gain(k, n):
        return 1.0 + nrm(k, (DEPTH, n), 0.01)

    return {
        "x": nrm(ks[0], (BATCH, SEQ, D_MODEL), 1.0),
        "p": nrm(ks[1], (DEPTH, BATCH, SEQ, PLE_DIM), 1.0),
        "mix_norm": gain(ks[2], D_MODEL),
        "w_in": nrm(ks[3], (DEPTH, D_MODEL, IN_COLS), D_MODEL ** -0.5),
        "ssm_a_re": -0.5 + nrm(ks[4], (DEPTH, SSM_GROUPS, SSM_STATE), 0.01),
        "ssm_a_im": math.pi * jnp.arange(SSM_STATE, dtype=f32) + nrm(ks[5], (DEPTH, SSM_GROUPS, SSM_STATE), 0.01),
        "ssm_log_step": jax.random.uniform(ks[6], (DEPTH, SSM_GROUPS), f32, math.log(1e-3), math.log(1e-1)),
        "ssm_b_re": nrm(ks[7], (DEPTH, SSM_GROUPS, SSM_STATE, SSM_GROUP), (2 * SSM_GROUP) ** -0.5),
        "ssm_b_im": nrm(ks[8], (DEPTH, SSM_GROUPS, SSM_STATE, SSM_GROUP), (2 * SSM_GROUP) ** -0.5),
        "ssm_c_re": nrm(ks[9], (DEPTH, SSM_GROUPS, SSM_GROUP, SSM_STATE), SSM_STATE ** -0.5),
        "ssm_c_im": nrm(ks[10], (DEPTH, SSM_GROUPS, SSM_GROUP, SSM_STATE), SSM_STATE ** -0.5),
        "ssm_d": nrm(ks[11], (DEPTH, SSM_WIDTH), 1.0),
        "w_glu_a": nrm(ks[12], (DEPTH, SSM_WIDTH, SSM_WIDTH), SSM_WIDTH ** -0.5),
        "w_glu_b": nrm(ks[13], (DEPTH, SSM_WIDTH, SSM_WIDTH), SSM_WIDTH ** -0.5),
        "q_norm": gain(ks[14], HEAD_DIM),
        "k_norm": gain(ks[15], HEAD_DIM),
        "attn_sinks": nrm(ks[16], (DEPTH, N_Q_HEADS), 0.5),
        "w_branch_ssm": nrm(ks[17], (DEPTH, SSM_WIDTH, D_MODEL), SSM_WIDTH ** -0.5),
        "w_branch_attn": nrm(ks[18], (DEPTH, ATTN_WIDTH, D_MODEL), ATTN_WIDTH ** -0.5),
        "w_out": nrm(ks[19], (DEPTH, D_MODEL, D_MODEL), D_MODEL ** -0.5),
        "moe_norm": gain(ks[20], D_MODEL),
        "w_router": nrm(ks[21], (DEPTH, D_MODEL, N_EXPERTS), D_MODEL ** -0.5),
        "router_bias": nrm(ks[22], (DEPTH, N_EXPERTS), 0.01),
        "we_gate": nrm(ks[23], (DEPTH, N_EXPERTS, D_MODEL, EXPERT_FF), D_MODEL ** -0.5),
        "we_up": nrm(ks[24], (DEPTH, N_EXPERTS, D_MODEL, EXPERT_FF), D_MODEL ** -0.5),
        "we_down": nrm(ks[25], (DEPTH, N_EXPERTS, EXPERT_FF, D_MODEL), EXPERT_FF ** -0.5),
        "ws_gate": nrm(ks[26], (DEPTH, D_MODEL, SHARED_FF), D_MODEL ** -0.5),
        "ws_up": nrm(ks[27], (DEPTH, D_MODEL, SHARED_FF), D_MODEL ** -0.5),
        "ws_down": nrm(ks[28], (DEPTH, SHARED_FF, D_MODEL), SHARED_FF ** -0.5),
        "ple_norm": gain(ks[29], D_MODEL),
        "w_ple": nrm(ks[30], (DEPTH, PLE_DIM, D_MODEL), PLE_DIM ** -0.5),
        "w_ple_gate": nrm(ks[31], (DEPTH, D_MODEL, D_MODEL), D_MODEL ** -0.5),
    }


def reference(x, p, mix_norm, w_in, ssm_a_re, ssm_a_im, ssm_log_step, ssm_b_re, ssm_b_im, ssm_c_re, ssm_c_im,
              ssm_d, w_glu_a, w_glu_b, q_norm, k_norm, attn_sinks, w_branch_ssm, w_branch_attn, w_out,
              moe_norm, w_router, router_bias, we_gate, we_up, we_down, ws_gate, ws_up, ws_down,
              ple_norm, w_ple, w_ple_gate):
    bsz, seq, d = x.shape
    n_tok = bsz * seq
    splits = [SSM_WIDTH, SSM_WIDTH + ATTN_WIDTH, SSM_WIDTH + ATTN_WIDTH + KV_WIDTH,
              SSM_WIDTH + ATTN_WIDTH + 2 * KV_WIDTH, SSM_WIDTH + ATTN_WIDTH + 2 * KV_WIDTH + D_MODEL]
    h = x
    for i in range(DEPTH):
        hn = rms_norm(h, mix_norm[i])
        u, q, k, v, g_ssm, g_attn = jnp.split(hn @ w_in[i], splits, axis=-1)
        y_ssm = s5_mixer(u, ssm_a_re[i], ssm_a_im[i], ssm_log_step[i], ssm_b_re[i], ssm_b_im[i],
                         ssm_c_re[i], ssm_c_im[i], ssm_d[i], w_glu_a[i], w_glu_b[i])
        y_attn = swa_sink_attention(q, k, v, q_norm[i], k_norm[i], attn_sinks[i])
        merged = jax.nn.sigmoid(g_ssm) * (y_ssm @ w_branch_ssm[i]) + jax.nn.sigmoid(g_attn) * (y_attn @ w_branch_attn[i])
        h = h + merged @ w_out[i]
        hm = rms_norm(h, moe_norm[i]).reshape(n_tok, d)
        eidx, ew = route(hm, w_router[i], router_bias[i])
        moe_out = swiglu(hm, ws_gate[i], ws_up[i], ws_down[i]) + routed_experts(hm, eidx, ew, we_gate[i], we_up[i], we_down[i])
        h = h + moe_out.reshape(bsz, seq, d)
        gate = jax.nn.sigmoid(rms_norm(h, ple_norm[i]) @ w_ple_gate[i])
        h = h + gate * (p[i] @ w_ple[i])
    return h
```

```python
import functools
import math

import jax
import jax.numpy as jnp
from jax import lax
from jax.experimental import pallas as pl
from jax.experimental.pallas import tpu as pltpu

SSM_GROUP = 16
SSM_STATE = 64
SSM_GROUPS_PER_STEP = 16
HEAD_DIM = 64
Q_PER_KV = 8
ATTN_BLOCK = 128
ALIBI_MAX_BIAS = 8.0
N_EXPERT_GROUPS = 8
TOPK_GROUPS = 4
TOP_K = 8
ROUTED_SCALE = 2.5
EPS = 1e-6
MOE_ROWS = 256
COMBINE_TOKENS = 64
V7X_VMEM_LIMIT = 56 * 1024 * 1024

BF16 = jnp.bfloat16
F32 = jnp.float32


def _dot(a, b):
    return jnp.dot(a, b, preferred_element_type=F32)


def _params(sem, vmem=V7X_VMEM_LIMIT):
    return pltpu.CompilerParams(dimension_semantics=sem, vmem_limit_bytes=vmem)


def _pick(n, pref):
    b = min(n, pref)
    while n % b:
        b //= 2
    return b


def _rmsnorm_kernel(x_ref, g_ref, o_ref):
    x = x_ref[...]
    ms = jnp.mean(x * x, axis=-1, keepdims=True)
    o_ref[...] = (x * lax.rsqrt(ms + EPS) * g_ref[...]).astype(o_ref.dtype)


def rmsnorm(x, g, out_dtype, bm=256):
    m, d = x.shape
    bm = _pick(m, bm)
    return pl.pallas_call(
        _rmsnorm_kernel,
        out_shape=jax.ShapeDtypeStruct((m, d), out_dtype),
        grid=(m // bm,),
        in_specs=[pl.BlockSpec((bm, d), lambda i: (i, 0)),
                  pl.BlockSpec((1, d), lambda i: (0, 0))],
        out_specs=pl.BlockSpec((bm, d), lambda i: (i, 0)),
        compiler_params=_params(("parallel",)),
        name="rmsnorm",
    )(x, g.reshape(1, d).astype(F32))


def _mm_kernel(x_ref, w_ref, o_ref):
    o_ref[...] = _dot(x_ref[...], w_ref[...]).astype(o_ref.dtype)


def matmul(x, w, out_dtype, bm=1024, bn=1024, out_shape=None, out_map=None, name="matmul"):
    m, k = x.shape
    n = w.shape[1]
    bm, bn = _pick(m, bm), _pick(n, bn)
    if out_shape is None:
        out_shape, out_map = (m, n), (lambda i, j: (i, j))
    return pl.pallas_call(
        _mm_kernel,
        out_shape=jax.ShapeDtypeStruct(out_shape, out_dtype),
        grid=(m // bm, n // bn),
        in_specs=[pl.BlockSpec((bm, k), lambda i, j: (i, 0)),
                  pl.BlockSpec((k, bn), lambda i, j: (0, j))],
        out_specs=pl.BlockSpec((bm, bn), out_map),
        compiler_params=_params(("parallel", "parallel")),
        name=name,
    )(x, w)


def _s5_kernel(u_ref, bmat_ref, cmat_ref, lre_ref, lim_ref, d_ref, o_ref, st_ref, cre_ref, cim_ref, *, batch):
    half = st_ref.shape[1] // 2

    @pl.when(pl.program_id(1) == 0)
    def _():
        cre_ref[...] = jnp.zeros_like(cre_ref)
        cim_ref[...] = jnp.zeros_like(cim_ref)

    u = u_ref[...]
    st_ref[...] = _dot(u, bmat_ref[...])
    lre = jnp.broadcast_to(lre_ref[...], (8, half))
    lim = jnp.broadcast_to(lim_ref[...], (8, half))
    steps = 8 // batch
    row = lax.broadcasted_iota(jnp.int32, (8, half), 0)

    def step(c_re, c_im, v_re, v_im):
        return lre * c_re - lim * c_im + v_re, lre * c_im + lim * c_re + v_im

    def body(i, carry):
        c_re, c_im = carry
        r0 = pl.multiple_of(i * 8, 8)
        v_re = st_ref[pl.ds(r0, 8), pl.ds(0, half)]
        v_im = st_ref[pl.ds(r0, 8), pl.ds(half, half)]
        out_re, out_im = None, None
        for k in range(steps):
            a_re, a_im = step(c_re, c_im, v_re, v_im)
            if steps == 1:
                out_re, out_im, c_re, c_im = a_re, a_im, a_re, a_im
                break
            grp = (row >= k * batch) & (row < (k + 1) * batch)
            out_re = a_re if out_re is None else jnp.where(grp, a_re, out_re)
            out_im = a_im if out_im is None else jnp.where(grp, a_im, out_im)
            c_re = jnp.where(grp, a_re, pltpu.roll(a_re, batch, 0))
            c_im = jnp.where(grp, a_im, pltpu.roll(a_im, batch, 0))
        st_ref[pl.ds(r0, 8), pl.ds(0, half)] = out_re
        st_ref[pl.ds(r0, 8), pl.ds(half, half)] = out_im
        return c_re, c_im

    c_re, c_im = lax.fori_loop(0, st_ref.shape[0] // 8, body, (cre_ref[...], cim_ref[...]))
    cre_ref[...] = c_re
    cim_ref[...] = c_im
    y = _dot(st_ref[...].astype(BF16), cmat_ref[...]) + d_ref[...] * u.astype(F32)
    o_ref[...] = jax.nn.gelu(y).astype(o_ref.dtype)


def s5_scan(u_tm, batch, a_re, a_im, log_step, b_re, b_im, c_re, c_im, d_skip, time_block=128):
    rows, width = u_tm.shape
    assert batch in (4, 8), "row groups of one time step must tile the 8 sublanes"
    gps = SSM_GROUPS_PER_STEP
    n_sets = width // (gps * SSM_GROUP)
    lanes = gps * SSM_GROUP
    half = gps * SSM_STATE
    lam = lax.complex(a_re.astype(F32), a_im.astype(F32))
    lam_bar = jnp.exp(lam * jnp.exp(log_step.astype(F32))[:, None])
    b_bar = ((lam_bar - 1.0) / lam)[..., None] * lax.complex(b_re.astype(F32), b_im.astype(F32))
    eye = jnp.eye(gps, dtype=F32)

    def block_diag_in(m):
        m = m.reshape(n_sets, gps, SSM_STATE, SSM_GROUP)
        return jnp.einsum('sgnc,gh->sgchn', m, eye).reshape(n_sets, lanes, half)

    def block_diag_out(m):
        m = m.reshape(n_sets, gps, SSM_GROUP, SSM_STATE)
        return jnp.einsum('sgcn,gh->sgnhc', m, eye).reshape(n_sets, half, lanes)

    bmat = jnp.concatenate([block_diag_in(jnp.real(b_bar)), block_diag_in(jnp.imag(b_bar))], axis=2).astype(BF16)
    cmat = jnp.concatenate([block_diag_out(c_re.astype(F32)), block_diag_out(-c_im.astype(F32))], axis=1).astype(BF16)
    lre = jnp.real(lam_bar).reshape(n_sets, 1, half)
    lim = jnp.imag(lam_bar).reshape(n_sets, 1, half)
    dsk = d_skip.astype(F32).reshape(n_sets, 1, lanes)
    seq = rows // batch
    tb = _pick(seq, time_block)
    rb = tb * batch
    return pl.pallas_call(
        functools.partial(_s5_kernel, batch=batch),
        out_shape=jax.ShapeDtypeStruct((rows, width), BF16),
        grid=(n_sets, seq // tb),
        in_specs=[pl.BlockSpec((rb, lanes), lambda s, t: (t, s)),
                  pl.BlockSpec((None, lanes, 2 * half), lambda s, t: (s, 0, 0)),
                  pl.BlockSpec((None, 2 * half, lanes), lambda s, t: (s, 0, 0)),
                  pl.BlockSpec((None, 1, half), lambda s, t: (s, 0, 0)),
                  pl.BlockSpec((None, 1, half), lambda s, t: (s, 0, 0)),
                  pl.BlockSpec((None, 1, lanes), lambda s, t: (s, 0, 0))],
        out_specs=pl.BlockSpec((rb, lanes), lambda s, t: (t, s)),
        scratch_shapes=[pltpu.VMEM((rb, 2 * half), F32),
                        pltpu.VMEM((8, half), F32),
                        pltpu.VMEM((8, half), F32)],
        compiler_params=_params(("parallel", "arbitrary")),
        name="s5_scan",
    )(u_tm, bmat, cmat, lre, lim, dsk)


def _glu_kernel(x_ref, wa_ref, wb_ref, o_ref):
    x = x_ref[...]
    o_ref[...] = (_dot(x, wa_ref[...]) * jax.nn.sigmoid(_dot(x, wb_ref[...]))).astype(o_ref.dtype)


def glu_from_time_major(y_tm, batch, wa, wb, bm=1024, bn=1024):
    rows, width = y_tm.shape
    seq = rows // batch
    n = wa.shape[1]
    bm, bn = _pick(seq, bm), _pick(n, bn)
    tpb = seq // bm
    y2 = y_tm.reshape(seq, batch * width)
    return pl.pallas_call(
        _glu_kernel,
        out_shape=jax.ShapeDtypeStruct((rows, n), BF16),
        grid=(rows // bm, n // bn),
        in_specs=[pl.BlockSpec((bm, width), lambda i, j: (i % tpb, i // tpb)),
                  pl.BlockSpec((width, bn), lambda i, j: (0, j)),
                  pl.BlockSpec((width, bn), lambda i, j: (0, j))],
        out_specs=pl.BlockSpec((bm, bn), lambda i, j: (i, j)),
        compiler_params=_params(("parallel", "parallel")),
        name="ssm_glu",
    )(y2, wa, wb)


def _attn_kernel(sink_ref, q_ref, kc_ref, kp_ref, vc_ref, vp_ref, qg_ref, kg_ref, o_ref, *, n_kv):
    blk = ATTN_BLOCK
    first = pl.program_id(1) == 0
    n_q = n_kv * Q_PER_KV
    qi = lax.broadcasted_iota(jnp.int32, (blk, 2 * blk), 0)
    kj = lax.broadcasted_iota(jnp.int32, (blk, 2 * blk), 1)
    dist = qi + blk - kj
    kmin = jnp.where(first, blk, 0)
    valid = (dist >= 0) & (dist < blk) & (kj >= kmin)
    dist_f = dist.astype(F32)
    qg = qg_ref[...]
    kg = kg_ref[...]

    def head_norm(t, gain):
        t = t.astype(F32)
        return t * lax.rsqrt(jnp.mean(t * t, axis=-1, keepdims=True) + EPS) * gain

    for h in range(n_kv):
        sl = slice(h * HEAD_DIM, (h + 1) * HEAD_DIM)
        kk = jnp.concatenate([kp_ref[:, sl], kc_ref[:, sl]], axis=0)
        vv = jnp.concatenate([vp_ref[:, sl], vc_ref[:, sl]], axis=0)
        kn = head_norm(kk, kg).astype(BF16)
        for g in range(Q_PER_KV):
            head = h * Q_PER_KV + g
            hs = slice(head * HEAD_DIM, (head + 1) * HEAD_DIM)
            qn = head_norm(q_ref[:, hs], qg).astype(BF16)
            s = lax.dot_general(qn, kn, (((1,), (1,)), ((), ())), preferred_element_type=F32)
            slope = 2.0 ** (-ALIBI_MAX_BIAS * (head + 1) / n_q)
            s = s * (HEAD_DIM ** -0.5) - slope * dist_f
            s = jnp.where(valid, s, -jnp.inf)
            sink = sink_ref[head]
            m = jnp.maximum(jnp.max(s, axis=-1, keepdims=True), sink)
            p = jnp.exp(s - m)
            denom = jnp.sum(p, axis=-1, keepdims=True) + jnp.exp(sink - m)
            probs = (p / denom).astype(BF16)
            o_ref[:, hs] = _dot(probs, vv).astype(o_ref.dtype)


def swa_attention(qkv, batch, n_q, q_gain, k_gain, sinks):
    rows = qkv.shape[0]
    n_kv = n_q // Q_PER_KV
    qw, kw = n_q * HEAD_DIM, n_kv * HEAD_DIM
    blk = ATTN_BLOCK
    nb = rows // batch // blk
    kcol = qw // kw
    cur = lambda b, j: b * nb + j
    prev = lambda b, j: b * nb + jnp.maximum(j - 1, 0)
    return pl.pallas_call(
        functools.partial(_attn_kernel, n_kv=n_kv),
        out_shape=jax.ShapeDtypeStruct((rows, qw), BF16),
        grid=(batch, nb),
        in_specs=[pl.BlockSpec(memory_space=pltpu.SMEM),
                  pl.BlockSpec((blk, qw), lambda b, j: (cur(b, j), 0)),
                  pl.BlockSpec((blk, kw), lambda b, j: (cur(b, j), kcol)),
                  pl.BlockSpec((blk, kw), lambda b, j: (prev(b, j), kcol)),
                  pl.BlockSpec((blk, kw), lambda b, j: (cur(b, j), kcol + 1)),
                  pl.BlockSpec((blk, kw), lambda b, j: (prev(b, j), kcol + 1)),
                  pl.BlockSpec((1, HEAD_DIM), lambda b, j: (0, 0)),
                  pl.BlockSpec((1, HEAD_DIM), lambda b, j: (0, 0))],
        out_specs=pl.BlockSpec((blk, qw), lambda b, j: (cur(b, j), 0)),
        compiler_params=_params(("parallel", "arbitrary")),
        name="swa_attention",
    )(sinks.astype(F32), qkv, qkv, qkv, qkv, qkv,
      q_gain.reshape(1, HEAD_DIM).astype(F32), k_gain.reshape(1, HEAD_DIM).astype(F32))


def _merge_kernel(ys_ref, ws_ref, ya_ref, wa_ref, gs_ref, ga_ref, o_ref):
    s = jax.nn.sigmoid(gs_ref[...].astype(F32)) * _dot(ys_ref[...], ws_ref[...])
    a = jax.nn.sigmoid(ga_ref[...].astype(F32)) * _dot(ya_ref[...], wa_ref[...])
    o_ref[...] = (s + a).astype(o_ref.dtype)


def branch_merge(y_ssm, w_ssm, y_attn, w_attn, gates, bm=1024, bn=1024):
    m, ks = y_ssm.shape
    ka = y_attn.shape[1]
    n = w_ssm.shape[1]
    bm, bn = _pick(m, bm), _pick(n, bn)
    nj = n // bn
    return pl.pallas_call(
        _merge_kernel,
        out_shape=jax.ShapeDtypeStruct((m, n), BF16),
        grid=(m // bm, nj),
        in_specs=[pl.BlockSpec((bm, ks), lambda i, j: (i, 0)),
                  pl.BlockSpec((ks, bn), lambda i, j: (0, j)),
                  pl.BlockSpec((bm, ka), lambda i, j: (i, 0)),
                  pl.BlockSpec((ka, bn), lambda i, j: (0, j)),
                  pl.BlockSpec((bm, bn), lambda i, j: (i, j)),
                  pl.BlockSpec((bm, bn), lambda i, j: (i, nj + j))],
        out_specs=pl.BlockSpec((bm, bn), lambda i, j: (i, j)),
        compiler_params=_params(("parallel", "parallel")),
        name="branch_merge",
    )(y_ssm, w_ssm, y_attn, w_attn, gates, gates)


def _mm_res_kernel(x_ref, w_ref, r_ref, o_ref):
    o_ref[...] = r_ref[...] + _dot(x_ref[...], w_ref[...])


def matmul_residual(x, w, res, bm=1024, bn=1024):
    m, k = x.shape
    n = w.shape[1]
    bm, bn = _pick(m, bm), _pick(n, bn)
    return pl.pallas_call(
        _mm_res_kernel,
        out_shape=jax.ShapeDtypeStruct((m, n), F32),
        grid=(m // bm, n // bn),
        in_specs=[pl.BlockSpec((bm, k), lambda i, j: (i, 0)),
                  pl.BlockSpec((k, bn), lambda i, j: (0, j)),
                  pl.BlockSpec((bm, bn), lambda i, j: (i, j))],
        out_specs=pl.BlockSpec((bm, bn), lambda i, j: (i, j)),
        compiler_params=_params(("parallel", "parallel")),
        name="out_proj_residual",
    )(x, w, res)


def _router_kernel(x_ref, w_ref, b_ref, idx_ref, wt_ref, *, n_exp):
    s = jax.nn.sigmoid(_dot(x_ref[...], w_ref[...]))
    sel = s + b_ref[...]
    rows = s.shape[0]
    lane = lax.broadcasted_iota(jnp.int32, (rows, n_exp), 1)
    per_group = n_exp // N_EXPERT_GROUPS
    grp = lane // per_group
    neg = -jnp.inf
    gscore = jnp.zeros_like(sel)
    for g in range(N_EXPERT_GROUPS):
        in_g = grp == g
        v = jnp.where(in_g, sel, neg)
        m1 = jnp.max(v, axis=-1, keepdims=True)
        i1 = jnp.min(jnp.where(v == m1, lane, n_exp), axis=-1, keepdims=True)
        m2 = jnp.max(jnp.where(lane == i1, neg, v), axis=-1, keepdims=True)
        gscore = jnp.where(in_g, m1 + m2, gscore)
    cand = jnp.full_like(sel, neg)
    remaining = gscore
    for _ in range(TOPK_GROUPS):
        gm = jnp.max(remaining, axis=-1, keepdims=True)
        gi = jnp.min(jnp.where(remaining == gm, grp, N_EXPERT_GROUPS), axis=-1, keepdims=True)
        hit = grp == gi
        cand = jnp.where(hit, sel, cand)
        remaining = jnp.where(hit, neg, remaining)
    slot = lax.broadcasted_iota(jnp.int32, (rows, TOP_K), 1)
    idx = jnp.zeros((rows, TOP_K), jnp.int32)
    wts = jnp.zeros((rows, TOP_K), F32)
    total = jnp.zeros((rows, 1), F32)
    for k in range(TOP_K):
        mx = jnp.max(cand, axis=-1, keepdims=True)
        ei = jnp.min(jnp.where(cand == mx, lane, n_exp), axis=-1, keepdims=True)
        hit = lane == ei
        wk = jnp.sum(jnp.where(hit, s, 0.0), axis=-1, keepdims=True)
        idx = jnp.where(slot == k, ei, idx)
        wts = jnp.where(slot == k, wk, wts)
        total = total + wk
        cand = jnp.where(hit, neg, cand)
    idx_ref[...] = idx
    wt_ref[...] = wts / total * ROUTED_SCALE


def router(hm, w_router, bias, bm=512):
    m, d = hm.shape
    n_exp = w_router.shape[1]
    bm = _pick(m, bm)
    return pl.pallas_call(
        functools.partial(_router_kernel, n_exp=n_exp),
        out_shape=(jax.ShapeDtypeStruct((m, TOP_K), jnp.int32), jax.ShapeDtypeStruct((m, TOP_K), F32)),
        grid=(m // bm,),
        in_specs=[pl.BlockSpec((bm, d), lambda i: (i, 0)),
                  pl.BlockSpec((d, n_exp), lambda i: (0, 0)),
                  pl.BlockSpec((1, n_exp), lambda i: (0, 0))],
        out_specs=(pl.BlockSpec((bm, TOP_K), lambda i: (i, 0)),
                   pl.BlockSpec((bm, TOP_K), lambda i: (i, 0))),
        compiler_params=_params(("parallel",)),
        name="router",
    )(hm, w_router, bias.reshape(1, n_exp).astype(F32))


def _swiglu(x, wg, wu, wd):
    act = (jax.nn.silu(_dot(x, wg)) * _dot(x, wu)).astype(BF16)
    return _dot(act, wd)


def _shared_kernel(x_ref, wg_ref, wu_ref, wd_ref, r_ref, o_ref):
    o_ref[...] = r_ref[...] + _swiglu(x_ref[...], wg_ref[...], wu_ref[...], wd_ref[...])


def shared_expert_residual(hm, wg, wu, wd, res, bm=256):
    m, d = hm.shape
    ff = wg.shape[1]
    bm = _pick(m, bm)
    return pl.pallas_call(
        _shared_kernel,
        out_shape=jax.ShapeDtypeStruct((m, d), F32),
        grid=(m // bm,),
        in_specs=[pl.BlockSpec((bm, d), lambda i: (i, 0)),
                  pl.BlockSpec((d, ff), lambda i: (0, 0)),
                  pl.BlockSpec((d, ff), lambda i: (0, 0)),
                  pl.BlockSpec((ff, d), lambda i: (0, 0)),
                  pl.BlockSpec((bm, d), lambda i: (i, 0))],
        out_specs=pl.BlockSpec((bm, d), lambda i: (i, 0)),
        compiler_params=_params(("parallel",)),
        name="shared_expert",
    )(hm, wg, wu, wd, res)


def _row_copy(src_hbm, row, dst, r, sem):
    return pltpu.make_async_copy(src_hbm.at[pl.ds(row, 1)], dst.at[pl.ds(r, 1)], sem)


def _expert_kernel(bexp_ref, nused_ref, tok_hbm, x_hbm, wg_ref, wu_ref, wd_ref, o_ref,
                   idx_smem, xbuf, idx_sem, row_sem):
    i = pl.program_id(0)
    n_used = nused_ref[0]
    rows = xbuf.shape[1]

    def fetch(blk, slot):
        cp = pltpu.make_async_copy(tok_hbm.at[blk], idx_smem.at[slot], idx_sem)
        cp.start()
        cp.wait()

        def body(r, _):
            _row_copy(x_hbm, idx_smem[slot, r], xbuf.at[slot], r, row_sem.at[slot]).start()
            return 0

        lax.fori_loop(0, rows, body, 0)

    @pl.when(jnp.logical_and(i == 0, n_used > 0))
    def _():
        fetch(0, 0)

    @pl.when(i + 1 < n_used)
    def _():
        fetch(i + 1, (i + 1) % 2)

    @pl.when(i < n_used)
    def _():
        slot = i % 2
        pltpu.make_async_copy(x_hbm.at[pl.ds(0, rows)], xbuf.at[slot], row_sem.at[slot]).wait()
        x = xbuf[slot].astype(BF16)
        o_ref[...] = _swiglu(x, wg_ref[...], wu_ref[...], wd_ref[...])

    @pl.when(i >= n_used)
    def _():
        o_ref[...] = jnp.zeros_like(o_ref)


def routed_experts_sorted(hm_f32, row_tok, block_expert, n_used, wg, wu, wd):
    nb, rows = row_tok.shape
    d = hm_f32.shape[1]
    ff = wg.shape[2]
    grid_spec = pltpu.PrefetchScalarGridSpec(
        num_scalar_prefetch=2,
        grid=(nb,),
        in_specs=[pl.BlockSpec(memory_space=pl.ANY),
                  pl.BlockSpec(memory_space=pl.ANY),
                  pl.BlockSpec((None, d, ff), lambda i, be, nu: (be[i], 0, 0)),
                  pl.BlockSpec((None, d, ff), lambda i, be, nu: (be[i], 0, 0)),
                  pl.BlockSpec((None, ff, d), lambda i, be, nu: (be[i], 0, 0))],
        out_specs=pl.BlockSpec((rows, d), lambda i, be, nu: (i, 0)),
        scratch_shapes=[pltpu.SMEM((2, rows), jnp.int32),
                        pltpu.VMEM((2, rows, d), F32),
                        pltpu.SemaphoreType.DMA(()),
                        pltpu.SemaphoreType.DMA((2,))],
    )
    return pl.pallas_call(
        _expert_kernel,
        out_shape=jax.ShapeDtypeStruct((nb * rows, d), F32),
        grid_spec=grid_spec,
        compiler_params=_params(("arbitrary",)),
        name="routed_experts",
    )(block_expert, n_used, row_tok, hm_f32, wg, wu, wd)


def _combine_kernel(pos_hbm, y_hbm, base_ref, wt_ref, g_ref, h_ref, hn_ref, idx_smem, gbuf, idx_sem, row_sem):
    i = pl.program_id(0)
    n_steps = pl.num_programs(0)
    toks = gbuf.shape[2]

    def fetch(blk, slot):
        cp = pltpu.make_async_copy(pos_hbm.at[blk], idx_smem.at[slot], idx_sem)
        cp.start()
        cp.wait()

        def body(r, _):
            for k in range(TOP_K):
                _row_copy(y_hbm, idx_smem[slot, r * TOP_K + k], gbuf.at[slot, k], r, row_sem.at[slot]).start()
            return 0

        lax.fori_loop(0, toks, body, 0)

    @pl.when(i == 0)
    def _():
        fetch(0, 0)

    @pl.when(i + 1 < n_steps)
    def _():
        fetch(i + 1, (i + 1) % 2)

    slot = i % 2
    for k in range(TOP_K):
        pltpu.make_async_copy(y_hbm.at[pl.ds(0, toks)], gbuf.at[slot, k], row_sem.at[slot]).wait()
    acc = jnp.zeros(base_ref.shape, F32)
    wt = wt_ref[...]
    for k in range(TOP_K):
        acc = acc + gbuf[slot, k] * wt[:, k:k + 1]
    h = base_ref[...] + acc
    h_ref[...] = h
    ms = jnp.mean(h * h, axis=-1, keepdims=True)
    hn_ref[...] = (h * lax.rsqrt(ms + EPS) * g_ref[...]).astype(hn_ref.dtype)


def combine(pos, y_sorted, base, wts, gain):
    steps, per = pos.shape
    toks = per // TOP_K
    m, d = base.shape
    return pl.pallas_call(
        _combine_kernel,
        out_shape=(jax.ShapeDtypeStruct((m, d), F32), jax.ShapeDtypeStruct((m, d), BF16)),
        grid=(steps,),
        in_specs=[pl.BlockSpec(memory_space=pl.ANY),
                  pl.BlockSpec(memory_space=pl.ANY),
                  pl.BlockSpec((toks, d), lambda i: (i, 0)),
                  pl.BlockSpec((toks, TOP_K), lambda i: (i, 0)),
                  pl.BlockSpec((1, d), lambda i: (0, 0))],
        out_specs=(pl.BlockSpec((toks, d), lambda i: (i, 0)),
                   pl.BlockSpec((toks, d), lambda i: (i, 0))),
        scratch_shapes=[pltpu.SMEM((2, per), jnp.int32),
                        pltpu.VMEM((2, TOP_K, toks, d), F32),
                        pltpu.SemaphoreType.DMA(()),
                        pltpu.SemaphoreType.DMA((2,))],
        compiler_params=_params(("arbitrary",)),
        name="moe_combine",
    )(pos, y_sorted, base, wts, gain.reshape(1, d).astype(F32))


def dispatch_tables(eidx, n_exp, rows):
    n_tok = eidx.shape[0]
    n_rows = n_tok * TOP_K
    nb = n_rows // rows + n_exp
    flat_e = eidx.reshape(-1)
    order = jnp.argsort(flat_e).astype(jnp.int32)
    sorted_e = flat_e[order]
    counts = jnp.bincount(flat_e, length=n_exp).astype(jnp.int32)
    padded = (counts + rows - 1) // rows * rows
    pad_end = jnp.cumsum(padded)
    pad_start = pad_end - padded
    start = jnp.cumsum(counts) - counts
    dest = (pad_start[sorted_e] + jnp.arange(n_rows, dtype=jnp.int32) - start[sorted_e]).astype(jnp.int32)
    row_tok = jnp.zeros((nb * rows,), jnp.int32).at[dest].set(order // TOP_K)
    pos = jnp.zeros((n_rows,), jnp.int32).at[order].set(dest)
    block_expert = jnp.minimum(jnp.searchsorted(pad_end // rows, jnp.arange(nb, dtype=jnp.int32), side='right'),
                               n_exp - 1).astype(jnp.int32)
    n_used = (pad_end[-1:] // rows).astype(jnp.int32)
    return row_tok.reshape(nb, rows), pos, block_expert, n_used


def _ple_kernel(hn_ref, wg_ref, p_ref, wp_ref, h_ref, o_ref):
    gate = jax.nn.sigmoid(_dot(hn_ref[...], wg_ref[...]))
    o_ref[...] = h_ref[...] + gate * _dot(p_ref[...].astype(BF16), wp_ref[...])


def ple_gate(hn, w_gate, p, w_ple, h, bm=512, bn=1024):
    m, d = hn.shape
    n = w_gate.shape[1]
    pd = p.shape[1]
    bm, bn = _pick(m, bm), _pick(n, bn)
    return pl.pallas_call(
        _ple_kernel,
        out_shape=jax.ShapeDtypeStruct((m, n), F32),
        grid=(m // bm, n // bn),
        in_specs=[pl.BlockSpec((bm, d), lambda i, j: (i, 0)),
                  pl.BlockSpec((d, bn), lambda i, j: (0, j)),
                  pl.BlockSpec((bm, pd), lambda i, j: (i, 0)),
                  pl.BlockSpec((pd, bn), lambda i, j: (0, j)),
                  pl.BlockSpec((bm, bn), lambda i, j: (i, j))],
        out_specs=pl.BlockSpec((bm, bn), lambda i, j: (i, j)),
        compiler_params=_params(("parallel", "parallel")),
        name="ple_gate",
    )(hn, w_gate, p, w_ple, h)


def _layer(h, p_i, prm, batch):
    (mix_norm, w_in, a_re, a_im, log_step, b_re, b_im, c_re, c_im, ssm_d, w_glu_a, w_glu_b, q_norm, k_norm,
     sinks, w_bs, w_ba, w_out, moe_norm, w_router, router_bias, we_gate, we_up, we_down, ws_gate, ws_up,
     ws_down, ple_norm, w_ple, w_ple_gate) = prm
    n_tok, d = h.shape
    seq = n_tok // batch
    ssm_w = w_glu_a.shape[0]
    n_q = sinks.shape[0]
    attn_w = n_q * HEAD_DIM
    kv_w = attn_w // Q_PER_KV
    n_exp = w_router.shape[1]
    c0, c1 = ssm_w, ssm_w + attn_w + 2 * kv_w
    bf = lambda w: w.astype(BF16)

    hn = rmsnorm(h, mix_norm, BF16)
    bm_u = _pick(seq, 1024)
    bn_u = _pick(ssm_w, 1024)
    tpb, npb = seq // bm_u, ssm_w // bn_u
    u_tm = matmul(hn, bf(w_in[:, :c0]), BF16, bm=bm_u, bn=bn_u, out_shape=(seq, batch * ssm_w),
                  out_map=lambda i, j: (i % tpb, (i // tpb) * npb + j), name="proj_u_time_major")
    qkv = matmul(hn, bf(w_in[:, c0:c1]), BF16, bn=(c1 - c0) // 2, name="proj_qkv")
    gates = matmul(hn, bf(w_in[:, c1:]), BF16, name="proj_gates")
    y_pre = s5_scan(u_tm.reshape(n_tok, ssm_w), batch, a_re, a_im, log_step, b_re, b_im, c_re, c_im, ssm_d)
    y_ssm = glu_from_time_major(y_pre, batch, bf(w_glu_a), bf(w_glu_b))
    y_attn = swa_attention(qkv, batch, n_q, q_norm, k_norm, sinks)
    merged = branch_merge(y_ssm, bf(w_bs), y_attn, bf(w_ba), gates)
    h = matmul_residual(merged, bf(w_out), h)

    hm32 = rmsnorm(h, moe_norm, F32)
    hm = hm32.astype(BF16)
    eidx, ew = router(hm, bf(w_router), router_bias)
    row_tok, pos, block_expert, n_used = dispatch_tables(eidx, n_exp, MOE_ROWS)
    base = shared_expert_residual(hm, bf(ws_gate), bf(ws_up), bf(ws_down), h)
    y_sorted = routed_experts_sorted(hm32, row_tok, block_expert, n_used, bf(we_gate), bf(we_up), bf(we_down))
    toks = _pick(n_tok, COMBINE_TOKENS)
    h, hn3 = combine(pos.reshape(n_tok // toks, toks * TOP_K), y_sorted, base, ew, ple_norm)

    return ple_gate(hn3, bf(w_ple_gate), p_i, bf(w_ple), h)


def kernel(x, p, mix_norm, w_in, ssm_a_re, ssm_a_im, ssm_log_step, ssm_b_re, ssm_b_im, ssm_c_re, ssm_c_im, ssm_d, w_glu_a, w_glu_b, q_norm, k_norm, attn_sinks, w_branch_ssm, w_branch_attn, w_out, moe_norm, w_router, router_bias, we_gate, we_up, we_down, ws_gate, ws_up, ws_down, ple_norm, w_ple, w_ple_gate):
    bsz, seq, d = x.shape
    layer_params = (mix_norm, w_in, ssm_a_re, ssm_a_im, ssm_log_step, ssm_b_re, ssm_b_im, ssm_c_re, ssm_c_im,
                    ssm_d, w_glu_a, w_glu_b, q_norm, k_norm, attn_sinks, w_branch_ssm, w_branch_attn, w_out,
                    moe_norm, w_router, router_bias, we_gate, we_up, we_down, ws_gate, ws_up, ws_down,
                    ple_norm, w_ple, w_ple_gate)
    h = x.reshape(bsz * seq, d)
    for i in range(mix_norm.shape[0]):
        h = _layer(h, p[i].reshape(bsz * seq, -1), tuple(w[i] for w in layer_params), bsz)
    return h.reshape(bsz, seq, d)
```

```python
import functools
import math

import jax
import jax.numpy as jnp
from jax import lax
from jax.experimental import pallas as pl
from jax.experimental.pallas import tpu as pltpu

SSM_GROUP = 16
SSM_STATE = 64
SSM_GROUPS_PER_STEP = 16
HEAD_DIM = 64
Q_PER_KV = 8
ATTN_BLOCK = 128
ALIBI_MAX_BIAS = 8.0
N_EXPERT_GROUPS = 8
TOPK_GROUPS = 4
TOP_K = 8
ROUTED_SCALE = 2.5
EPS = 1e-6
MOE_ROWS = 256
COMBINE_TOKENS = 64
V7X_VMEM_LIMIT = 56 * 1024 * 1024

BF16 = jnp.bfloat16
F32 = jnp.float32


def _dot(a, b):
    return jnp.dot(a, b, preferred_element_type=F32)


def _params(sem, vmem=V7X_VMEM_LIMIT):
    return pltpu.CompilerParams(dimension_semantics=sem, vmem_limit_bytes=vmem)


def _pick(n, pref):
    b = min(n, pref)
    while n % b:
        b //= 2
    return b


def _rmsnorm_kernel(x_ref, g_ref, o_ref):
    x = x_ref[...]
    ms = jnp.mean(x * x, axis=-1, keepdims=True)
    o_ref[...] = (x * lax.rsqrt(ms + EPS) * g_ref[...]).astype(o_ref.dtype)


def rmsnorm(x, g, out_dtype, bm=256):
    m, d = x.shape
    bm = _pick(m, bm)
    return pl.pallas_call(
        _rmsnorm_kernel,
        out_shape=jax.ShapeDtypeStruct((m, d), out_dtype),
        grid=(m // bm,),
        in_specs=[pl.BlockSpec((bm, d), lambda i: (i, 0)),
                  pl.BlockSpec((1, d), lambda i: (0, 0))],
        out_specs=pl.BlockSpec((bm, d), lambda i: (i, 0)),
        compiler_params=_params(("parallel",)),
        name="rmsnorm",
    )(x, g.reshape(1, d).astype(F32))


def _mm_kernel(x_ref, w_ref, o_ref):
    o_ref[...] = _dot(x_ref[...], w_ref[...]).astype(o_ref.dtype)


def matmul(x, w, out_dtype, bm=1024, bn=1024, out_shape=None, out_map=None, name="matmul"):
    m, k = x.shape
    n = w.shape[1]
    bm, bn = _pick(m, bm), _pick(n, bn)
    if out_shape is None:
        out_shape, out_map = (m, n), (lambda i, j: (i, j))
    return pl.pallas_call(
        _mm_kernel,
        out_shape=jax.ShapeDtypeStruct(out_shape, out_dtype),
        grid=(m // bm, n // bn),
        in_specs=[pl.BlockSpec((bm, k), lambda i, j: (i, 0)),
                  pl.BlockSpec((k, bn), lambda i, j: (0, j))],
        out_specs=pl.BlockSpec((bm, bn), out_map),
        compiler_params=_params(("parallel", "parallel")),
        name=name,
    )(x, w)


def _s5_kernel(u_ref, bmat_ref, cmat_ref, lre_ref, lim_ref, d_ref, o_ref, st_ref, cre_ref, cim_ref, *, batch):
    half = st_ref.shape[1] // 2

    @pl.when(pl.program_id(1) == 0)
    def _():
        cre_ref[...] = jnp.zeros_like(cre_ref)
        cim_ref[...] = jnp.zeros_like(cim_ref)

    u = u_ref[...]
    st_ref[...] = _dot(u, bmat_ref[...])
    lre = jnp.broadcast_to(lre_ref[...], (8, half))
    lim = jnp.broadcast_to(lim_ref[...], (8, half))
    steps = 8 // batch
    row = lax.broadcasted_iota(jnp.int32, (8, half), 0)

    def step(c_re, c_im, v_re, v_im):
        return lre * c_re - lim * c_im + v_re, lre * c_im + lim * c_re + v_im

    def body(i, carry):
        c_re, c_im = carry
        r0 = pl.multiple_of(i * 8, 8)
        v_re = st_ref[pl.ds(r0, 8), pl.ds(0, half)]
        v_im = st_ref[pl.ds(r0, 8), pl.ds(half, half)]
        out_re, out_im = None, None
        for k in range(steps):
            a_re, a_im = step(c_re, c_im, v_re, v_im)
            if steps == 1:
                out_re, out_im, c_re, c_im = a_re, a_im, a_re, a_im
                break
            grp = (row >= k * batch) & (row < (k + 1) * batch)
            out_re = a_re if out_re is None else jnp.where(grp, a_re, out_re)
            out_im = a_im if out_im is None else jnp.where(grp, a_im, out_im)
            c_re = jnp.where(grp, a_re, pltpu.roll(a_re, batch, 0))
            c_im = jnp.where(grp, a_im, pltpu.roll(a_im, batch, 0))
        st_ref[pl.ds(r0, 8), pl.ds(0, half)] = out_re
        st_ref[pl.ds(r0, 8), pl.ds(half, half)] = out_im
        return c_re, c_im

    c_re, c_im = lax.fori_loop(0, st_ref.shape[0] // 8, body, (cre_ref[...], cim_ref[...]))
    cre_ref[...] = c_re
    cim_ref[...] = c_im
    y = _dot(st_ref[...].astype(BF16), cmat_ref[...]) + d_ref[...] * u.astype(F32)
    o_ref[...] = jax.nn.gelu(y).astype(o_ref.dtype)


def s5_scan(u_tm, batch, a_re, a_im, log_step, b_re, b_im, c_re, c_im, d_skip, time_block=128):
    rows, width = u_tm.shape
    assert batch in (4, 8), "row groups of one time step must tile the 8 sublanes"
    gps = SSM_GROUPS_PER_STEP
    n_sets = width // (gps * SSM_GROUP)
    lanes = gps * SSM_GROUP
    half = gps * SSM_STATE
    lam = lax.complex(a_re.astype(F32), a_im.astype(F32))
    lam_bar = jnp.exp(lam * jnp.exp(log_step.astype(F32))[:, None])
    b_bar = ((lam_bar - 1.0) / lam)[..., None] * lax.complex(b_re.astype(F32), b_im.astype(F32))
    eye = jnp.eye(gps, dtype=F32)

    def block_diag_in(m):
        m = m.reshape(n_sets, gps, SSM_STATE, SSM_GROUP)
        return jnp.einsum('sgnc,gh->sgchn', m, eye).reshape(n_sets, lanes, half)

    def block_diag_out(m):
        m = m.reshape(n_sets, gps, SSM_GROUP, SSM_STATE)
        return jnp.einsum('sgcn,gh->sgnhc', m, eye).reshape(n_sets, half, lanes)

    bmat = jnp.concatenate([block_diag_in(jnp.real(b_bar)), block_diag_in(jnp.imag(b_bar))], axis=2).astype(BF16)
    cmat = jnp.concatenate([block_diag_out(c_re.astype(F32)), block_diag_out(-c_im.astype(F32))], axis=1).astype(BF16)
    lre = jnp.real(lam_bar).reshape(n_sets, 1, half)
    lim = jnp.imag(lam_bar).reshape(n_sets, 1, half)
    dsk = d_skip.astype(F32).reshape(n_sets, 1, lanes)
    seq = rows // batch
    tb = _pick(seq, time_block)
    rb = tb * batch
    return pl.pallas_call(
        functools.partial(_s5_kernel, batch=batch),
        out_shape=jax.ShapeDtypeStruct((rows, width), BF16),
        grid=(n_sets, seq // tb),
        in_specs=[pl.BlockSpec((rb, lanes), lambda s, t: (t, s)),
                  pl.BlockSpec((None, lanes, 2 * half), lambda s, t: (s, 0, 0)),
                  pl.BlockSpec((None, 2 * half, lanes), lambda s, t: (s, 0, 0)),
                  pl.BlockSpec((None, 1, half), lambda s, t: (s, 0, 0)),
                  pl.BlockSpec((None, 1, half), lambda s, t: (s, 0, 0)),
                  pl.BlockSpec((None, 1, lanes), lambda s, t: (s, 0, 0))],
        out_specs=pl.BlockSpec((rb, lanes), lambda s, t: (t, s)),
        scratch_shapes=[pltpu.VMEM((rb, 2 * half), F32),
                        pltpu.VMEM((8, half), F32),
                        pltpu.VMEM((8, half), F32)],
        compiler_params=_params(("parallel", "arbitrary")),
        name="s5_scan",
    )(u_tm, bmat, cmat, lre, lim, dsk)


def _glu_kernel(x_ref, wa_ref, wb_ref, o_ref):
    x = x_ref[...]
    o_ref[...] = (_dot(x, wa_ref[...]) * jax.nn.sigmoid(_dot(x, wb_ref[...]))).astype(o_ref.dtype)


def glu_from_time_major(y_tm, batch, wa, wb, bm=1024, bn=1024):
    rows, width = y_tm.shape
    seq = rows // batch
    n = wa.shape[1]
    bm, bn = _pick(seq, bm), _pick(n, bn)
    tpb = seq // bm
    y2 = y_tm.reshape(seq, batch * width)
    return pl.pallas_call(
        _glu_kernel,
        out_shape=jax.ShapeDtypeStruct((rows, n), BF16),
        grid=(rows // bm, n // bn),
        in_specs=[pl.BlockSpec((bm, width), lambda i, j: (i % tpb, i // tpb)),
                  pl.BlockSpec((width, bn), lambda i, j: (0, j)),
                  pl.BlockSpec((width, bn), lambda i, j: (0, j))],
        out_specs=pl.BlockSpec((bm, bn), lambda i, j: (i, j)),
        compiler_params=_params(("parallel", "parallel")),
        name="ssm_glu",
    )(y2, wa, wb)


def _attn_kernel(sink_ref, q_ref, kc_ref, kp_ref, vc_ref, vp_ref, qg_ref, kg_ref, o_ref, *, n_kv):
    blk = ATTN_BLOCK
    first = pl.program_id(1) == 0
    n_q = n_kv * Q_PER_KV
    qi = lax.broadcasted_iota(jnp.int32, (blk, 2 * blk), 0)
    kj = lax.broadcasted_iota(jnp.int32, (blk, 2 * blk), 1)
    dist = qi + blk - kj
    kmin = jnp.where(first, blk, 0)
    valid = (dist >= 0) & (dist < blk) & (kj >= kmin)
    dist_f = dist.astype(F32)
    qg = qg_ref[...]
    kg = kg_ref[...]

    def head_norm(t, gain):
        t = t.astype(F32)
        return t * lax.rsqrt(jnp.mean(t * t, axis=-1, keepdims=True) + EPS) * gain

    for h in range(n_kv):
        sl = slice(h * HEAD_DIM, (h + 1) * HEAD_DIM)
        kk = jnp.concatenate([kp_ref[:, sl], kc_ref[:, sl]], axis=0)
        vv = jnp.concatenate([vp_ref[:, sl], vc_ref[:, sl]], axis=0)
        kn = head_norm(kk, kg).astype(BF16)
        for g in range(Q_PER_KV):
            head = h * Q_PER_KV + g
            hs = slice(head * HEAD_DIM, (head + 1) * HEAD_DIM)
            qn = head_norm(q_ref[:, hs], qg).astype(BF16)
            s = lax.dot_general(qn, kn, (((1,), (1,)), ((), ())), preferred_element_type=F32)
            slope = 2.0 ** (-ALIBI_MAX_BIAS * (head + 1) / n_q)
            s = s * (HEAD_DIM ** -0.5) - slope * dist_f
            s = jnp.where(valid, s, -jnp.inf)
            sink = sink_ref[head]
            m = jnp.maximum(jnp.max(s, axis=-1, keepdims=True), sink)
            p = jnp.exp(s - m)
            denom = jnp.sum(p, axis=-1, keepdims=True) + jnp.exp(sink - m)
            probs = (p / denom).astype(BF16)
            o_ref[:, hs] = _dot(probs, vv).astype(o_ref.dtype)


def swa_attention(qkv, batch, n_q, q_gain, k_gain, sinks):
    rows = qkv.shape[0]
    n_kv = n_q // Q_PER_KV
    qw, kw = n_q * HEAD_DIM, n_kv * HEAD_DIM
    blk = ATTN_BLOCK
    nb = rows // batch // blk
    kcol = qw // kw
    cur = lambda b, j: b * nb + j
    prev = lambda b, j: b * nb + jnp.maximum(j - 1, 0)
    return pl.pallas_call(
        functools.partial(_attn_kernel, n_kv=n_kv),
        out_shape=jax.ShapeDtypeStruct((rows, qw), BF16),
        grid=(batch, nb),
        in_specs=[pl.BlockSpec(memory_space=pltpu.SMEM),
                  pl.BlockSpec((blk, qw), lambda b, j: (cur(b, j), 0)),
                  pl.BlockSpec((blk, kw), lambda b, j: (cur(b, j), kcol)),
                  pl.BlockSpec((blk, kw), lambda b, j: (prev(b, j), kcol)),
                  pl.BlockSpec((blk, kw), lambda b, j: (cur(b, j), kcol + 1)),
                  pl.BlockSpec((blk, kw), lambda b, j: (prev(b, j), kcol + 1)),
                  pl.BlockSpec((1, HEAD_DIM), lambda b, j: (0, 0)),
                  pl.BlockSpec((1, HEAD_DIM), lambda b, j: (0, 0))],
        out_specs=pl.BlockSpec((blk, qw), lambda b, j: (cur(b, j), 0)),
        compiler_params=_params(("parallel", "arbitrary")),
        name="swa_attention",
    )(sinks.astype(F32), qkv, qkv, qkv, qkv, qkv,
      q_gain.reshape(1, HEAD_DIM).astype(F32), k_gain.reshape(1, HEAD_DIM).astype(F32))


def _merge_kernel(ys_ref, ws_ref, ya_ref, wa_ref, gs_ref, ga_ref, o_ref):
    s = jax.nn.sigmoid(gs_ref[...].astype(F32)) * _dot(ys_ref[...], ws_ref[...])
    a = jax.nn.sigmoid(ga_ref[...].astype(F32)) * _dot(ya_ref[...], wa_ref[...])
    o_ref[...] = (s + a).astype(o_ref.dtype)


def branch_merge(y_ssm, w_ssm, y_attn, w_attn, gates, bm=1024, bn=1024):
    m, ks = y_ssm.shape
    ka = y_attn.shape[1]
    n = w_ssm.shape[1]
    bm, bn = _pick(m, bm), _pick(n, bn)
    nj = n // bn
    return pl.pallas_call(
        _merge_kernel,
        out_shape=jax.ShapeDtypeStruct((m, n), BF16),
        grid=(m // bm, nj),
        in_specs=[pl.BlockSpec((bm, ks), lambda i, j: (i, 0)),
                  pl.BlockSpec((ks, bn), lambda i, j: (0, j)),
                  pl.BlockSpec((bm, ka), lambda i, j: (i, 0)),
                  pl.BlockSpec((ka, bn), lambda i, j: (0, j)),
                  pl.BlockSpec((bm, bn), lambda i, j: (i, j)),
                  pl.BlockSpec((bm, bn), lambda i, j: (i, nj + j))],
        out_specs=pl.BlockSpec((bm, bn), lambda i, j: (i, j)),
        compiler_params=_params(("parallel", "parallel")),
        name="branch_merge",
    )(y_ssm, w_ssm, y_attn, w_attn, gates, gates)


def _mm_res_kernel(x_ref, w_ref, r_ref, o_ref):
    o_ref[...] = r_ref[...] + _dot(x_ref[...], w_ref[...])


def matmul_residual(x, w, res, bm=1024, bn=1024):
    m, k = x.shape
    n = w.shape[1]
    bm, bn = _pick(m, bm), _pick(n, bn)
    return pl.pallas_call(
        _mm_res_kernel,
        out_shape=jax.ShapeDtypeStruct((m, n), F32),
        grid=(m // bm, n // bn),
        in_specs=[pl.BlockSpec((bm, k), lambda i, j: (i, 0)),
                  pl.BlockSpec((k, bn), lambda i, j: (0, j)),
                  pl.BlockSpec((bm, bn), lambda i, j: (i, j))],
        out_specs=pl.BlockSpec((bm, bn), lambda i, j: (i, j)),
        compiler_params=_params(("parallel", "parallel")),
        name="out_proj_residual",
    )(x, w, res)


def _router_kernel(x_ref, w_ref, b_ref, idx_ref, wt_ref, rank_ref, cnt_ref, seen_ref, *, n_exp):
    @pl.when(pl.program_id(0) == 0)
    def _():
        seen_ref[...] = jnp.zeros_like(seen_ref)

    s = jax.nn.sigmoid(_dot(x_ref[...], w_ref[...]))
    sel = s + b_ref[...]
    rows = s.shape[0]
    lane = lax.broadcasted_iota(jnp.int32, (rows, n_exp), 1)
    per_group = n_exp // N_EXPERT_GROUPS
    grp = lane // per_group
    neg = -jnp.inf
    gscore = jnp.zeros_like(sel)
    for g in range(N_EXPERT_GROUPS):
        in_g = grp == g
        v = jnp.where(in_g, sel, neg)
        m1 = jnp.max(v, axis=-1, keepdims=True)
        i1 = jnp.min(jnp.where(v == m1, lane, n_exp), axis=-1, keepdims=True)
        m2 = jnp.max(jnp.where(lane == i1, neg, v), axis=-1, keepdims=True)
        gscore = jnp.where(in_g, m1 + m2, gscore)
    cand = jnp.full_like(sel, neg)
    remaining = gscore
    for _ in range(TOPK_GROUPS):
        gm = jnp.max(remaining, axis=-1, keepdims=True)
        gi = jnp.min(jnp.where(remaining == gm, grp, N_EXPERT_GROUPS), axis=-1, keepdims=True)
        hit = grp == gi
        cand = jnp.where(hit, sel, cand)
        remaining = jnp.where(hit, neg, remaining)
    slot = lax.broadcasted_iota(jnp.int32, (rows, TOP_K), 1)
    idx = jnp.zeros((rows, TOP_K), jnp.int32)
    wts = jnp.zeros((rows, TOP_K), F32)
    total = jnp.zeros((rows, 1), F32)
    picked = jnp.zeros_like(sel)
    hits = []
    for k in range(TOP_K):
        mx = jnp.max(cand, axis=-1, keepdims=True)
        ei = jnp.min(jnp.where(cand == mx, lane, n_exp), axis=-1, keepdims=True)
        hit = lane == ei
        wk = jnp.sum(jnp.where(hit, s, 0.0), axis=-1, keepdims=True)
        idx = jnp.where(slot == k, ei, idx)
        wts = jnp.where(slot == k, wk, wts)
        total = total + wk
        cand = jnp.where(hit, neg, cand)
        picked = jnp.where(hit, 1.0, picked)
        hits.append(hit)
    idx_ref[...] = idx
    wt_ref[...] = wts / total * ROUTED_SCALE
    r_i = lax.broadcasted_iota(jnp.int32, (rows, rows), 0)
    c_i = lax.broadcasted_iota(jnp.int32, (rows, rows), 1)
    lower = jnp.where(c_i < r_i, 1.0, 0.0).astype(BF16)
    before = _dot(lower, picked.astype(BF16)) + seen_ref[...]
    rank = jnp.zeros((rows, TOP_K), F32)
    for k in range(TOP_K):
        rank = jnp.where(slot == k, jnp.sum(jnp.where(hits[k], before, 0.0), axis=-1, keepdims=True), rank)
    rank_ref[...] = rank.astype(jnp.int32)
    seen = seen_ref[...] + jnp.sum(picked, axis=0, keepdims=True)
    seen_ref[...] = seen
    cnt_ref[...] = seen.astype(jnp.int32)


def router(hm, w_router, bias, bm=512):
    m, d = hm.shape
    n_exp = w_router.shape[1]
    bm = _pick(m, bm)
    tk = lambda dt: jax.ShapeDtypeStruct((m, TOP_K), dt)
    tk_spec = pl.BlockSpec((bm, TOP_K), lambda i: (i, 0))
    return pl.pallas_call(
        functools.partial(_router_kernel, n_exp=n_exp),
        out_shape=(tk(jnp.int32), tk(F32), tk(jnp.int32), jax.ShapeDtypeStruct((1, n_exp), jnp.int32)),
        grid=(m // bm,),
        in_specs=[pl.BlockSpec((bm, d), lambda i: (i, 0)),
                  pl.BlockSpec((d, n_exp), lambda i: (0, 0)),
                  pl.BlockSpec((1, n_exp), lambda i: (0, 0))],
        out_specs=(tk_spec, tk_spec, tk_spec, pl.BlockSpec((1, n_exp), lambda i: (0, 0))),
        scratch_shapes=[pltpu.VMEM((1, n_exp), F32)],
        compiler_params=_params(("arbitrary",)),
        name="router",
    )(hm, w_router, bias.reshape(1, n_exp).astype(F32))


def _swiglu(x, wg, wu, wd):
    act = (jax.nn.silu(_dot(x, wg)) * _dot(x, wu)).astype(BF16)
    return _dot(act, wd)


def _shared_kernel(x_ref, wg_ref, wu_ref, wd_ref, r_ref, o_ref):
    o_ref[...] = r_ref[...] + _swiglu(x_ref[...], wg_ref[...], wu_ref[...], wd_ref[...])


def shared_expert_residual(hm, wg, wu, wd, res, bm=256):
    m, d = hm.shape
    ff = wg.shape[1]
    bm = _pick(m, bm)
    return pl.pallas_call(
        _shared_kernel,
        out_shape=jax.ShapeDtypeStruct((m, d), F32),
        grid=(m // bm,),
        in_specs=[pl.BlockSpec((bm, d), lambda i: (i, 0)),
                  pl.BlockSpec((d, ff), lambda i: (0, 0)),
                  pl.BlockSpec((d, ff), lambda i: (0, 0)),
                  pl.BlockSpec((ff, d), lambda i: (0, 0)),
                  pl.BlockSpec((bm, d), lambda i: (i, 0))],
        out_specs=pl.BlockSpec((bm, d), lambda i: (i, 0)),
        compiler_params=_params(("parallel",)),
        name="shared_expert",
    )(hm, wg, wu, wd, res)


def _pack_rows(v):
    c = v.shape[1] // 2
    lo = lax.bitcast_convert_type(v[:, :c].astype(BF16).astype(F32), jnp.uint32)
    hi = lax.bitcast_convert_type(v[:, c:].astype(BF16).astype(F32), jnp.uint32)
    return hi | (lo >> 16)


def _unpack_rows(w):
    lo = lax.bitcast_convert_type(w << 16, F32)
    hi = lax.bitcast_convert_type(w & jnp.uint32(0xFFFF0000), F32)
    return lo, hi


def _rmsnorm_pack_kernel(x_ref, g_ref, o_ref, p_ref):
    x = x_ref[...]
    ms = jnp.mean(x * x, axis=-1, keepdims=True)
    y = x * lax.rsqrt(ms + EPS) * g_ref[...]
    o_ref[...] = y.astype(o_ref.dtype)
    p_ref[...] = _pack_rows(y)


def rmsnorm_packed(x, g, bm=256):
    m, d = x.shape
    bm = _pick(m, bm)
    packed = jax.eval_shape(_pack_rows, jax.ShapeDtypeStruct((bm, d), F32))
    pw = packed.shape[1]
    return pl.pallas_call(
        _rmsnorm_pack_kernel,
        out_shape=(jax.ShapeDtypeStruct((m, d), BF16), jax.ShapeDtypeStruct((m, pw), packed.dtype)),
        grid=(m // bm,),
        in_specs=[pl.BlockSpec((bm, d), lambda i: (i, 0)),
                  pl.BlockSpec((1, d), lambda i: (0, 0))],
        out_specs=(pl.BlockSpec((bm, d), lambda i: (i, 0)),
                   pl.BlockSpec((bm, pw), lambda i: (i, 0))),
        compiler_params=_params(("parallel",)),
        name="rmsnorm_packed",
    )(x, g.reshape(1, d).astype(F32))


def _row_copy(src, src_row, dst, dst_row, sem):
    return pltpu.make_async_copy(src.at[pl.ds(src_row, 1)], dst.at[pl.ds(dst_row, 1)], sem)


def _dispatch_kernel(cnt_ref, pstart_ref, pend_ref, pos_hbm, x_hbm, zero_hbm, o_hbm, idx_smem, idx_sem, row_sem):
    i = pl.program_id(0)
    n_steps = pl.num_programs(0)
    per = idx_smem.shape[1]
    toks = per // TOP_K

    def idx_copy(step, slot):
        return pltpu.make_async_copy(pos_hbm.at[step], idx_smem.at[slot], idx_sem.at[slot])

    def wait_rows(slot):
        pltpu.make_async_copy(o_hbm.at[pl.ds(0, per)], o_hbm.at[pl.ds(0, per)], row_sem.at[slot]).wait()

    @pl.when(i == 0)
    def _():
        idx_copy(0, 0).start()

    @pl.when(i + 1 < n_steps)
    def _():
        idx_copy(i + 1, (i + 1) % 2).start()

    slot = i % 2
    idx_copy(i, slot).wait()
    base = i * toks

    def body(t, _):
        for k in range(TOP_K):
            _row_copy(x_hbm, base + t, o_hbm, idx_smem[slot, t * TOP_K + k], row_sem.at[slot]).start()
        return 0

    lax.fori_loop(0, toks, body, 0)

    @pl.when(i > 0)
    def _():
        wait_rows(1 - slot)

    @pl.when(i == n_steps - 1)
    def _():
        wait_rows(slot)
        def expert_padding(e, _):
            first = pstart_ref[e] + cnt_ref[e]
            n_pad = pend_ref[e] - first

            def zbody(r, _):
                _row_copy(zero_hbm, 0, o_hbm, first + r, row_sem.at[0]).start()
                return 0

            def zwait(r, _):
                _row_copy(zero_hbm, 0, o_hbm, first, row_sem.at[0]).wait()
                return 0

            lax.fori_loop(0, n_pad, zbody, 0)
            lax.fori_loop(0, n_pad, zwait, 0)
            return 0

        lax.fori_loop(0, cnt_ref.shape[0], expert_padding, 0)

        blk_rows = zero_hbm.shape[0]
        n_exp = cnt_ref.shape[0]
        first_blk = pend_ref[n_exp - 1] // blk_rows
        n_tail = o_hbm.shape[0] // blk_rows - first_blk

        def tail_copy(b):
            dst = o_hbm.at[pl.ds(pl.multiple_of((first_blk + b) * blk_rows, blk_rows), blk_rows)]
            return pltpu.make_async_copy(zero_hbm, dst, row_sem.at[0])

        def tbody(b, _):
            tail_copy(b).start()
            return 0

        def twait(b, _):
            tail_copy(b).wait()
            return 0

        lax.fori_loop(0, n_tail, tbody, 0)
        lax.fori_loop(0, n_tail, twait, 0)


def dispatch_rows(x_rows, pos, counts, pad_start, pad_end, cap, blk_rows, chunk_tokens=512):
    n_tok, width = x_rows.shape
    toks = _pick(n_tok, chunk_tokens)
    steps = n_tok // toks
    grid_spec = pltpu.PrefetchScalarGridSpec(
        num_scalar_prefetch=3,
        grid=(steps,),
        in_specs=[pl.BlockSpec(memory_space=pl.ANY)] * 3,
        out_specs=pl.BlockSpec(memory_space=pl.ANY),
        scratch_shapes=[pltpu.SMEM((2, toks * TOP_K), jnp.int32),
                        pltpu.SemaphoreType.DMA((2,)),
                        pltpu.SemaphoreType.DMA((2,))],
    )
    return pl.pallas_call(
        _dispatch_kernel,
        out_shape=jax.ShapeDtypeStruct((cap, width), x_rows.dtype),
        grid_spec=grid_spec,
        compiler_params=_params(("arbitrary",)),
        name="moe_dispatch",
    )(counts, pad_start, pad_end, pos.reshape(steps, toks * TOP_K), x_rows,
      jnp.zeros((blk_rows, width), x_rows.dtype))


def _expert_kernel(bexp_ref, nused_ref, x_ref, wg_ref, wu_ref, wd_ref, o_ref):
    i = pl.program_id(0)

    @pl.when(i < nused_ref[0])
    def _():
        lo, hi = _unpack_rows(x_ref[...])
        x = jnp.concatenate([lo, hi], axis=1).astype(BF16)
        o_ref[...] = _pack_rows(_swiglu(x, wg_ref[...], wu_ref[...], wd_ref[...]))

    @pl.when(i >= nused_ref[0])
    def _():
        o_ref[...] = jnp.zeros_like(o_ref)


def routed_experts_sorted(x_sorted, block_expert, n_used, wg, wu, wd, rows):
    cap, width = x_sorted.shape
    nb = cap // rows
    d, ff = wg.shape[1], wg.shape[2]
    used = lambda i, be, nu: (jnp.maximum(jnp.minimum(i, nu[0] - 1), 0), 0)
    grid_spec = pltpu.PrefetchScalarGridSpec(
        num_scalar_prefetch=2,
        grid=(nb,),
        in_specs=[pl.BlockSpec((rows, width), used),
                  pl.BlockSpec((None, d, ff), lambda i, be, nu: (be[i], 0, 0)),
                  pl.BlockSpec((None, d, ff), lambda i, be, nu: (be[i], 0, 0)),
                  pl.BlockSpec((None, ff, d), lambda i, be, nu: (be[i], 0, 0))],
        out_specs=pl.BlockSpec((rows, width), lambda i, be, nu: (i, 0)),
    )
    return pl.pallas_call(
        _expert_kernel,
        out_shape=jax.ShapeDtypeStruct((cap, width), x_sorted.dtype),
        grid_spec=grid_spec,
        compiler_params=_params(("arbitrary",)),
        name="routed_experts",
    )(block_expert, n_used, x_sorted, wg, wu, wd)


def _combine_kernel(pos_hbm, y_hbm, base_ref, wt_ref, g_ref, h_ref, hn_ref, idx_smem, gbuf, idx_sem, row_sem):
    i = pl.program_id(0)
    n_steps = pl.num_programs(0)
    toks = gbuf.shape[2]

    def fetch(blk, slot):
        cp = pltpu.make_async_copy(pos_hbm.at[blk], idx_smem.at[slot], idx_sem)
        cp.start()
        cp.wait()

        def body(r, _):
            for k in range(TOP_K):
                _row_copy(y_hbm, idx_smem[slot, r * TOP_K + k], gbuf.at[slot, k], r, row_sem.at[slot]).start()
            return 0

        lax.fori_loop(0, toks, body, 0)

    @pl.when(i == 0)
    def _():
        fetch(0, 0)

    @pl.when(i + 1 < n_steps)
    def _():
        fetch(i + 1, (i + 1) % 2)

    slot = i % 2
    for k in range(TOP_K):
        pltpu.make_async_copy(y_hbm.at[pl.ds(0, toks)], gbuf.at[slot, k], row_sem.at[slot]).wait()
    wt = wt_ref[...]
    acc_lo, acc_hi = None, None
    for k in range(TOP_K):
        lo, hi = _unpack_rows(gbuf[slot, k])
        wk = wt[:, k:k + 1]
        acc_lo = lo * wk if acc_lo is None else acc_lo + lo * wk
        acc_hi = hi * wk if acc_hi is None else acc_hi + hi * wk
    h = base_ref[...] + jnp.concatenate([acc_lo, acc_hi], axis=1)
    h_ref[...] = h
    ms = jnp.mean(h * h, axis=-1, keepdims=True)
    hn_ref[...] = (h * lax.rsqrt(ms + EPS) * g_ref[...]).astype(hn_ref.dtype)


def combine(pos, y_sorted, base, wts, gain, tokens=COMBINE_TOKENS):
    m, d = base.shape
    toks = _pick(m, tokens)
    steps, per = m // toks, toks * TOP_K
    pos = pos.reshape(steps, per)
    return pl.pallas_call(
        _combine_kernel,
        out_shape=(jax.ShapeDtypeStruct((m, d), F32), jax.ShapeDtypeStruct((m, d), BF16)),
        grid=(steps,),
        in_specs=[pl.BlockSpec(memory_space=pl.ANY),
                  pl.BlockSpec(memory_space=pl.ANY),
                  pl.BlockSpec((toks, d), lambda i: (i, 0)),
                  pl.BlockSpec((toks, TOP_K), lambda i: (i, 0)),
                  pl.BlockSpec((1, d), lambda i: (0, 0))],
        out_specs=(pl.BlockSpec((toks, d), lambda i: (i, 0)),
                   pl.BlockSpec((toks, d), lambda i: (i, 0))),
        scratch_shapes=[pltpu.SMEM((2, per), jnp.int32),
                        pltpu.VMEM((2, TOP_K, toks, y_sorted.shape[1]), y_sorted.dtype),
                        pltpu.SemaphoreType.DMA(()),
                        pltpu.SemaphoreType.DMA((2,))],
        compiler_params=_params(("arbitrary",)),
        name="moe_combine",
    )(pos, y_sorted, base, wts, gain.reshape(1, d).astype(F32))


def dispatch_tables(eidx, rank, counts, rows):
    n_tok = eidx.shape[0]
    n_exp = counts.shape[0]
    nb = n_tok * TOP_K // rows + n_exp
    pad_end = jnp.cumsum((counts + rows - 1) // rows * rows)
    pad_start = pad_end - (counts + rows - 1) // rows * rows
    experts = jnp.arange(n_exp, dtype=jnp.int32)
    pos = rank + jnp.sum(jnp.where(eidx[..., None] == experts, pad_start, 0), axis=-1)
    blocks = jnp.arange(nb, dtype=jnp.int32)
    block_expert = jnp.minimum(jnp.sum(pad_end[None, :] // rows <= blocks[:, None], axis=-1), n_exp - 1)
    n_used = pad_end[-1:] // rows
    i32 = lambda a: a.astype(jnp.int32)
    return i32(pos).reshape(-1), i32(block_expert), i32(n_used), i32(pad_start), i32(pad_end), nb * rows


def _ple_kernel(hn_ref, wg_ref, p_ref, wp_ref, h_ref, o_ref):
    gate = jax.nn.sigmoid(_dot(hn_ref[...], wg_ref[...]))
    o_ref[...] = h_ref[...] + gate * _dot(p_ref[...].astype(BF16), wp_ref[...])


def ple_gate(hn, w_gate, p, w_ple, h, bm=512, bn=1024):
    m, d = hn.shape
    n = w_gate.shape[1]
    pd = p.shape[1]
    bm, bn = _pick(m, bm), _pick(n, bn)
    return pl.pallas_call(
        _ple_kernel,
        out_shape=jax.ShapeDtypeStruct((m, n), F32),
        grid=(m // bm, n // bn),
        in_specs=[pl.BlockSpec((bm, d), lambda i, j: (i, 0)),
                  pl.BlockSpec((d, bn), lambda i, j: (0, j)),
                  pl.BlockSpec((bm, pd), lambda i, j: (i, 0)),
                  pl.BlockSpec((pd, bn), lambda i, j: (0, j)),
                  pl.BlockSpec((bm, bn), lambda i, j: (i, j))],
        out_specs=pl.BlockSpec((bm, bn), lambda i, j: (i, j)),
        compiler_params=_params(("parallel", "parallel")),
        name="ple_gate",
    )(hn, w_gate, p, w_ple, h)


def _layer(h, p_i, prm, batch):
    (mix_norm, w_in, a_re, a_im, log_step, b_re, b_im, c_re, c_im, ssm_d, w_glu_a, w_glu_b, q_norm, k_norm,
     sinks, w_bs, w_ba, w_out, moe_norm, w_router, router_bias, we_gate, we_up, we_down, ws_gate, ws_up,
     ws_down, ple_norm, w_ple, w_ple_gate) = prm
    n_tok, d = h.shape
    seq = n_tok // batch
    ssm_w = w_glu_a.shape[0]
    n_q = sinks.shape[0]
    attn_w = n_q * HEAD_DIM
    kv_w = attn_w // Q_PER_KV
    n_exp = w_router.shape[1]
    c0, c1 = ssm_w, ssm_w + attn_w + 2 * kv_w
    bf = lambda w: w.astype(BF16)

    hn = rmsnorm(h, mix_norm, BF16)
    bm_u = _pick(seq, 1024)
    bn_u = _pick(ssm_w, 1024)
    tpb, npb = seq // bm_u, ssm_w // bn_u
    u_tm = matmul(hn, bf(w_in[:, :c0]), BF16, bm=bm_u, bn=bn_u, out_shape=(seq, batch * ssm_w),
                  out_map=lambda i, j: (i % tpb, (i // tpb) * npb + j), name="proj_u_time_major")
    qkv = matmul(hn, bf(w_in[:, c0:c1]), BF16, bn=(c1 - c0) // 2, name="proj_qkv")
    gates = matmul(hn, bf(w_in[:, c1:]), BF16, name="proj_gates")
    y_pre = s5_scan(u_tm.reshape(n_tok, ssm_w), batch, a_re, a_im, log_step, b_re, b_im, c_re, c_im, ssm_d)
    y_ssm = glu_from_time_major(y_pre, batch, bf(w_glu_a), bf(w_glu_b))
    y_attn = swa_attention(qkv, batch, n_q, q_norm, k_norm, sinks)
    merged = branch_merge(y_ssm, bf(w_bs), y_attn, bf(w_ba), gates)
    h = matmul_residual(merged, bf(w_out), h)

    hm, hm_rows = rmsnorm_packed(h, moe_norm)
    eidx, ew, rank, counts = router(hm, bf(w_router), router_bias)
    counts = counts.reshape(n_exp)
    pos, block_expert, n_used, pad_start, pad_end, cap = dispatch_tables(eidx, rank, counts, MOE_ROWS)
    x_sorted = dispatch_rows(hm_rows, pos, counts, pad_start, pad_end, cap, MOE_ROWS)
    base = shared_expert_residual(hm, bf(ws_gate), bf(ws_up), bf(ws_down), h)
    y_sorted = routed_experts_sorted(x_sorted, block_expert, n_used, bf(we_gate), bf(we_up), bf(we_down), MOE_ROWS)
    h, hn3 = combine(pos, y_sorted, base, ew, ple_norm)

    return ple_gate(hn3, bf(w_ple_gate), p_i, bf(w_ple), h)


def kernel(x, p, mix_norm, w_in, ssm_a_re, ssm_a_im, ssm_log_step, ssm_b_re, ssm_b_im, ssm_c_re, ssm_c_im, ssm_d, w_glu_a, w_glu_b, q_norm, k_norm, attn_sinks, w_branch_ssm, w_branch_attn, w_out, moe_norm, w_router, router_bias, we_gate, we_up, we_down, ws_gate, ws_up, ws_down, ple_norm, w_ple, w_ple_gate):
    bsz, seq, d = x.shape
    layer_params = (mix_norm, w_in, ssm_a_re, ssm_a_im, ssm_log_step, ssm_b_re, ssm_b_im, ssm_c_re, ssm_c_im,
                    ssm_d, w_glu_a, w_glu_b, q_norm, k_norm, attn_sinks, w_branch_ssm, w_branch_attn, w_out,
                    moe_norm, w_router, router_bias, we_gate, we_up, we_down, ws_gate, ws_up, ws_down,
                    ple_norm, w_ple, w_ple_gate)
    h = x.reshape(bsz * seq, d)
    for i in range(mix_norm.shape[0]):
        h = _layer(h, p[i].reshape(bsz * seq, -1), tuple(w[i] for w in layer_params), bsz)
    return h.reshape(bsz, seq, d)
```

```python
import functools
import math

import jax
import jax.numpy as jnp
from jax import lax
from jax.experimental import pallas as pl
from jax.experimental.pallas import tpu as pltpu

SSM_GROUP = 16
SSM_STATE = 64
SSM_GROUPS_PER_STEP = 16
HEAD_DIM = 64
Q_PER_KV = 8
ATTN_BLOCK = 128
ALIBI_MAX_BIAS = 8.0
N_EXPERT_GROUPS = 8
TOPK_GROUPS = 4
TOP_K = 8
ROUTED_SCALE = 2.5
EPS = 1e-6
MOE_ROWS = 256
COMBINE_TOKENS = 64
V7X_VMEM_LIMIT = 56 * 1024 * 1024

BF16 = jnp.bfloat16
F32 = jnp.float32


def _dot(a, b):
    return jnp.dot(a, b, preferred_element_type=F32)


def _params(sem, vmem=V7X_VMEM_LIMIT):
    return pltpu.CompilerParams(dimension_semantics=sem, vmem_limit_bytes=vmem)


def _pick(n, pref):
    b = min(n, pref)
    while n % b:
        b //= 2
    return b


def _rmsnorm_kernel(x_ref, g_ref, o_ref):
    x = x_ref[...]
    ms = jnp.mean(x * x, axis=-1, keepdims=True)
    o_ref[...] = (x * lax.rsqrt(ms + EPS) * g_ref[...]).astype(o_ref.dtype)


def rmsnorm(x, g, out_dtype, bm=256):
    m, d = x.shape
    bm = _pick(m, bm)
    return pl.pallas_call(
        _rmsnorm_kernel,
        out_shape=jax.ShapeDtypeStruct((m, d), out_dtype),
        grid=(m // bm,),
        in_specs=[pl.BlockSpec((bm, d), lambda i: (i, 0)),
                  pl.BlockSpec((1, d), lambda i: (0, 0))],
        out_specs=pl.BlockSpec((bm, d), lambda i: (i, 0)),
        compiler_params=_params(("parallel",)),
        name="rmsnorm",
    )(x, g.reshape(1, d).astype(F32))


def _mm_kernel(x_ref, w_ref, o_ref):
    o_ref[...] = _dot(x_ref[...], w_ref[...]).astype(o_ref.dtype)


def matmul(x, w, out_dtype, bm=1024, bn=1024, out_shape=None, out_map=None, name="matmul"):
    m, k = x.shape
    n = w.shape[1]
    bm, bn = _pick(m, bm), _pick(n, bn)
    if out_shape is None:
        out_shape, out_map = (m, n), (lambda i, j: (i, j))
    return pl.pallas_call(
        _mm_kernel,
        out_shape=jax.ShapeDtypeStruct(out_shape, out_dtype),
        grid=(m // bm, n // bn),
        in_specs=[pl.BlockSpec((bm, k), lambda i, j: (i, 0)),
                  pl.BlockSpec((k, bn), lambda i, j: (0, j))],
        out_specs=pl.BlockSpec((bm, bn), out_map),
        compiler_params=_params(("parallel", "parallel")),
        name=name,
    )(x, w)


def _s5_kernel(u_ref, bmat_ref, cmat_ref, lre_ref, lim_ref, d_ref, o_ref, st_ref, cre_ref, cim_ref, *, batch):
    half = st_ref.shape[1] // 2

    @pl.when(pl.program_id(1) == 0)
    def _():
        cre_ref[...] = jnp.zeros_like(cre_ref)
        cim_ref[...] = jnp.zeros_like(cim_ref)

    u = u_ref[...]
    st_ref[...] = _dot(u, bmat_ref[...])
    lre = jnp.broadcast_to(lre_ref[...], (8, half))
    lim = jnp.broadcast_to(lim_ref[...], (8, half))
    steps = 8 // batch
    row = lax.broadcasted_iota(jnp.int32, (8, half), 0)

    def step(c_re, c_im, v_re, v_im):
        return lre * c_re - lim * c_im + v_re, lre * c_im + lim * c_re + v_im

    def body(i, carry):
        c_re, c_im = carry
        r0 = pl.multiple_of(i * 8, 8)
        v_re = st_ref[pl.ds(r0, 8), pl.ds(0, half)]
        v_im = st_ref[pl.ds(r0, 8), pl.ds(half, half)]
        out_re, out_im = None, None
        for k in range(steps):
            a_re, a_im = step(c_re, c_im, v_re, v_im)
            if steps == 1:
                out_re, out_im, c_re, c_im = a_re, a_im, a_re, a_im
                break
            grp = (row >= k * batch) & (row < (k + 1) * batch)
            out_re = a_re if out_re is None else jnp.where(grp, a_re, out_re)
            out_im = a_im if out_im is None else jnp.where(grp, a_im, out_im)
            c_re = jnp.where(grp, a_re, pltpu.roll(a_re, batch, 0))
            c_im = jnp.where(grp, a_im, pltpu.roll(a_im, batch, 0))
        st_ref[pl.ds(r0, 8), pl.ds(0, half)] = out_re
        st_ref[pl.ds(r0, 8), pl.ds(half, half)] = out_im
        return c_re, c_im

    c_re, c_im = lax.fori_loop(0, st_ref.shape[0] // 8, body, (cre_ref[...], cim_ref[...]))
    cre_ref[...] = c_re
    cim_ref[...] = c_im
    y = _dot(st_ref[...].astype(BF16), cmat_ref[...]) + d_ref[...] * u.astype(F32)
    o_ref[...] = jax.nn.gelu(y).astype(o_ref.dtype)


def s5_scan(u_tm, batch, a_re, a_im, log_step, b_re, b_im, c_re, c_im, d_skip, time_block=128):
    rows, width = u_tm.shape
    assert batch in (4, 8), "row groups of one time step must tile the 8 sublanes"
    gps = SSM_GROUPS_PER_STEP
    n_sets = width // (gps * SSM_GROUP)
    lanes = gps * SSM_GROUP
    half = gps * SSM_STATE
    lam = lax.complex(a_re.astype(F32), a_im.astype(F32))
    lam_bar = jnp.exp(lam * jnp.exp(log_step.astype(F32))[:, None])
    b_bar = ((lam_bar - 1.0) / lam)[..., None] * lax.complex(b_re.astype(F32), b_im.astype(F32))
    eye = jnp.eye(gps, dtype=F32)

    def block_diag_in(m):
        m = m.reshape(n_sets, gps, SSM_STATE, SSM_GROUP)
        return jnp.einsum('sgnc,gh->sgchn', m, eye).reshape(n_sets, lanes, half)

    def block_diag_out(m):
        m = m.reshape(n_sets, gps, SSM_GROUP, SSM_STATE)
        return jnp.einsum('sgcn,gh->sgnhc', m, eye).reshape(n_sets, half, lanes)

    bmat = jnp.concatenate([block_diag_in(jnp.real(b_bar)), block_diag_in(jnp.imag(b_bar))], axis=2).astype(BF16)
    cmat = jnp.concatenate([block_diag_out(c_re.astype(F32)), block_diag_out(-c_im.astype(F32))], axis=1).astype(BF16)
    lre = jnp.real(lam_bar).reshape(n_sets, 1, half)
    lim = jnp.imag(lam_bar).reshape(n_sets, 1, half)
    dsk = d_skip.astype(F32).reshape(n_sets, 1, lanes)
    seq = rows // batch
    tb = _pick(seq, time_block)
    rb = tb * batch
    return pl.pallas_call(
        functools.partial(_s5_kernel, batch=batch),
        out_shape=jax.ShapeDtypeStruct((rows, width), BF16),
        grid=(n_sets, seq // tb),
        in_specs=[pl.BlockSpec((rb, lanes), lambda s, t: (t, s)),
                  pl.BlockSpec((None, lanes, 2 * half), lambda s, t: (s, 0, 0)),
                  pl.BlockSpec((None, 2 * half, lanes), lambda s, t: (s, 0, 0)),
                  pl.BlockSpec((None, 1, half), lambda s, t: (s, 0, 0)),
                  pl.BlockSpec((None, 1, half), lambda s, t: (s, 0, 0)),
                  pl.BlockSpec((None, 1, lanes), lambda s, t: (s, 0, 0))],
        out_specs=pl.BlockSpec((rb, lanes), lambda s, t: (t, s)),
        scratch_shapes=[pltpu.VMEM((rb, 2 * half), F32),
                        pltpu.VMEM((8, half), F32),
                        pltpu.VMEM((8, half), F32)],
        compiler_params=_params(("parallel", "arbitrary")),
        name="s5_scan",
    )(u_tm, bmat, cmat, lre, lim, dsk)


def _glu_kernel(x_ref, wa_ref, wb_ref, o_ref):
    x = x_ref[...]
    o_ref[...] = (_dot(x, wa_ref[...]) * jax.nn.sigmoid(_dot(x, wb_ref[...]))).astype(o_ref.dtype)


def glu_from_time_major(y_tm, batch, wa, wb, bm=1024, bn=1024):
    rows, width = y_tm.shape
    seq = rows // batch
    n = wa.shape[1]
    bm, bn = _pick(seq, bm), _pick(n, bn)
    tpb = seq // bm
    y2 = y_tm.reshape(seq, batch * width)
    return pl.pallas_call(
        _glu_kernel,
        out_shape=jax.ShapeDtypeStruct((rows, n), BF16),
        grid=(rows // bm, n // bn),
        in_specs=[pl.BlockSpec((bm, width), lambda i, j: (i % tpb, i // tpb)),
                  pl.BlockSpec((width, bn), lambda i, j: (0, j)),
                  pl.BlockSpec((width, bn), lambda i, j: (0, j))],
        out_specs=pl.BlockSpec((bm, bn), lambda i, j: (i, j)),
        compiler_params=_params(("parallel", "parallel")),
        name="ssm_glu",
    )(y2, wa, wb)


def _attn_kernel(sink_ref, q_ref, kc_ref, kp_ref, vc_ref, vp_ref, qg_ref, kg_ref, o_ref, *, n_kv):
    blk, hd = ATTN_BLOCK, HEAD_DIM
    first = pl.program_id(1) == 0
    n_q = n_kv * Q_PER_KV
    kj = lax.broadcasted_iota(jnp.int32, (2 * blk, blk), 0)
    qi = lax.broadcasted_iota(jnp.int32, (2 * blk, blk), 1)
    dist = qi + blk - kj
    kmin = jnp.where(first, blk, 0)
    valid = (dist >= 0) & (dist < blk) & (kj >= kmin)
    dist_f = dist.astype(F32)

    def head_norm_t(t, heads, gain):
        t3 = t.reshape(heads, hd, t.shape[1])
        ms = jnp.mean(t3 * t3, axis=1, keepdims=True)
        return t3 * lax.rsqrt(ms + EPS) * gain[None]

    qn = (head_norm_t(q_ref[...].astype(F32).T, n_q, qg_ref[...]) * (hd ** -0.5)).astype(BF16)
    kk = jnp.concatenate([kp_ref[...], kc_ref[...]], axis=0).astype(F32)
    kn = head_norm_t(kk.T, n_kv, kg_ref[...]).reshape(n_kv * hd, 2 * blk).T.astype(BF16)
    vt = jnp.concatenate([vp_ref[...], vc_ref[...]], axis=0).astype(F32).T.astype(BF16)

    for h in range(n_kv):
        qt = jnp.concatenate([qn[h * Q_PER_KV + g] for g in range(Q_PER_KV)], axis=1)
        st = _dot(kn[:, h * hd:(h + 1) * hd], qt)
        probs = []
        for g in range(Q_PER_KV):
            head = h * Q_PER_KV + g
            slope = 2.0 ** (-ALIBI_MAX_BIAS * (head + 1) / n_q)
            s = jnp.where(valid, st[:, g * blk:(g + 1) * blk] - slope * dist_f, -jnp.inf)
            sink = sink_ref[head]
            m = jnp.maximum(jnp.max(s, axis=0, keepdims=True), sink)
            p = jnp.exp(s - m)
            denom = jnp.sum(p, axis=0, keepdims=True) + jnp.exp(sink - m)
            probs.append((p * (1.0 / denom)).astype(BF16))
        ot = _dot(vt[h * hd:(h + 1) * hd, :], jnp.concatenate(probs, axis=1))
        ot = jnp.concatenate([ot[:, g * blk:(g + 1) * blk] for g in range(Q_PER_KV)], axis=0)
        o_ref[:, h * Q_PER_KV * hd:(h + 1) * Q_PER_KV * hd] = ot.T.astype(o_ref.dtype)


def swa_attention(qkv, batch, n_q, q_gain, k_gain, sinks):
    rows = qkv.shape[0]
    n_kv = n_q // Q_PER_KV
    qw, kw = n_q * HEAD_DIM, n_kv * HEAD_DIM
    blk = ATTN_BLOCK
    nb = rows // batch // blk
    kcol = qw // kw
    cur = lambda b, j: b * nb + j
    prev = lambda b, j: b * nb + jnp.maximum(j - 1, 0)
    return pl.pallas_call(
        functools.partial(_attn_kernel, n_kv=n_kv),
        out_shape=jax.ShapeDtypeStruct((rows, qw), BF16),
        grid=(batch, nb),
        in_specs=[pl.BlockSpec(memory_space=pltpu.SMEM),
                  pl.BlockSpec((blk, qw), lambda b, j: (cur(b, j), 0)),
                  pl.BlockSpec((blk, kw), lambda b, j: (cur(b, j), kcol)),
                  pl.BlockSpec((blk, kw), lambda b, j: (prev(b, j), kcol)),
                  pl.BlockSpec((blk, kw), lambda b, j: (cur(b, j), kcol + 1)),
                  pl.BlockSpec((blk, kw), lambda b, j: (prev(b, j), kcol + 1)),
                  pl.BlockSpec((HEAD_DIM, blk), lambda b, j: (0, 0)),
                  pl.BlockSpec((HEAD_DIM, 2 * blk), lambda b, j: (0, 0))],
        out_specs=pl.BlockSpec((blk, qw), lambda b, j: (cur(b, j), 0)),
        compiler_params=_params(("parallel", "arbitrary")),
        name="swa_attention",
    )(sinks.astype(F32), qkv, qkv, qkv, qkv, qkv,
      jnp.broadcast_to(q_gain.astype(F32)[:, None], (HEAD_DIM, blk)),
      jnp.broadcast_to(k_gain.astype(F32)[:, None], (HEAD_DIM, 2 * blk)))


def _merge_kernel(ys_ref, ws_ref, ya_ref, wa_ref, gs_ref, ga_ref, o_ref):
    s = jax.nn.sigmoid(gs_ref[...].astype(F32)) * _dot(ys_ref[...], ws_ref[...])
    a = jax.nn.sigmoid(ga_ref[...].astype(F32)) * _dot(ya_ref[...], wa_ref[...])
    o_ref[...] = (s + a).astype(o_ref.dtype)


def branch_merge(y_ssm, w_ssm, y_attn, w_attn, gates, bm=1024, bn=1024):
    m, ks = y_ssm.shape
    ka = y_attn.shape[1]
    n = w_ssm.shape[1]
    bm, bn = _pick(m, bm), _pick(n, bn)
    nj = n // bn
    return pl.pallas_call(
        _merge_kernel,
        out_shape=jax.ShapeDtypeStruct((m, n), BF16),
        grid=(m // bm, nj),
        in_specs=[pl.BlockSpec((bm, ks), lambda i, j: (i, 0)),
                  pl.BlockSpec((ks, bn), lambda i, j: (0, j)),
                  pl.BlockSpec((bm, ka), lambda i, j: (i, 0)),
                  pl.BlockSpec((ka, bn), lambda i, j: (0, j)),
                  pl.BlockSpec((bm, bn), lambda i, j: (i, j)),
                  pl.BlockSpec((bm, bn), lambda i, j: (i, nj + j))],
        out_specs=pl.BlockSpec((bm, bn), lambda i, j: (i, j)),
        compiler_params=_params(("parallel", "parallel")),
        name="branch_merge",
    )(y_ssm, w_ssm, y_attn, w_attn, gates, gates)


def _mm_res_kernel(x_ref, w_ref, r_ref, o_ref):
    o_ref[...] = r_ref[...] + _dot(x_ref[...], w_ref[...])


def matmul_residual(x, w, res, bm=1024, bn=1024):
    m, k = x.shape
    n = w.shape[1]
    bm, bn = _pick(m, bm), _pick(n, bn)
    return pl.pallas_call(
        _mm_res_kernel,
        out_shape=jax.ShapeDtypeStruct((m, n), F32),
        grid=(m // bm, n // bn),
        in_specs=[pl.BlockSpec((bm, k), lambda i, j: (i, 0)),
                  pl.BlockSpec((k, bn), lambda i, j: (0, j)),
                  pl.BlockSpec((bm, bn), lambda i, j: (i, j))],
        out_specs=pl.BlockSpec((bm, bn), lambda i, j: (i, j)),
        compiler_params=_params(("parallel", "parallel")),
        name="out_proj_residual",
    )(x, w, res)


def _router_kernel(x_ref, w_ref, b_ref, idx_ref, wt_ref, rank_ref, cnt_ref, seen_ref, *, n_exp):
    @pl.when(pl.program_id(0) == 0)
    def _():
        seen_ref[...] = jnp.zeros_like(seen_ref)

    s = jax.nn.sigmoid(_dot(x_ref[...], w_ref[...]))
    sel = s + b_ref[...]
    rows = s.shape[0]
    lane = lax.broadcasted_iota(jnp.int32, (rows, n_exp), 1)
    per_group = n_exp // N_EXPERT_GROUPS
    grp = lane // per_group
    neg = -jnp.inf
    gscore = jnp.zeros_like(sel)
    for g in range(N_EXPERT_GROUPS):
        in_g = grp == g
        v = jnp.where(in_g, sel, neg)
        m1 = jnp.max(v, axis=-1, keepdims=True)
        i1 = jnp.min(jnp.where(v == m1, lane, n_exp), axis=-1, keepdims=True)
        m2 = jnp.max(jnp.where(lane == i1, neg, v), axis=-1, keepdims=True)
        gscore = jnp.where(in_g, m1 + m2, gscore)
    cand = jnp.full_like(sel, neg)
    remaining = gscore
    for _ in range(TOPK_GROUPS):
        gm = jnp.max(remaining, axis=-1, keepdims=True)
        gi = jnp.min(jnp.where(remaining == gm, grp, N_EXPERT_GROUPS), axis=-1, keepdims=True)
        hit = grp == gi
        cand = jnp.where(hit, sel, cand)
        remaining = jnp.where(hit, neg, remaining)
    slot = lax.broadcasted_iota(jnp.int32, (rows, TOP_K), 1)
    idx = jnp.zeros((rows, TOP_K), jnp.int32)
    wts = jnp.zeros((rows, TOP_K), F32)
    total = jnp.zeros((rows, 1), F32)
    picked = jnp.zeros_like(sel)
    hits = []
    for k in range(TOP_K):
        mx = jnp.max(cand, axis=-1, keepdims=True)
        ei = jnp.min(jnp.where(cand == mx, lane, n_exp), axis=-1, keepdims=True)
        hit = lane == ei
        wk = jnp.sum(jnp.where(hit, s, 0.0), axis=-1, keepdims=True)
        idx = jnp.where(slot == k, ei, idx)
        wts = jnp.where(slot == k, wk, wts)
        total = total + wk
        cand = jnp.where(hit, neg, cand)
        picked = jnp.where(hit, 1.0, picked)
        hits.append(hit)
    idx_ref[...] = idx
    wt_ref[...] = wts / total * ROUTED_SCALE
    r_i = lax.broadcasted_iota(jnp.int32, (rows, rows), 0)
    c_i = lax.broadcasted_iota(jnp.int32, (rows, rows), 1)
    lower = jnp.where(c_i < r_i, 1.0, 0.0).astype(BF16)
    before = _dot(lower, picked.astype(BF16)) + seen_ref[...]
    rank = jnp.zeros((rows, TOP_K), F32)
    for k in range(TOP_K):
        rank = jnp.where(slot == k, jnp.sum(jnp.where(hits[k], before, 0.0), axis=-1, keepdims=True), rank)
    rank_ref[...] = rank.astype(jnp.int32)
    seen = seen_ref[...] + jnp.sum(picked, axis=0, keepdims=True)
    seen_ref[...] = seen
    cnt_ref[...] = seen.astype(jnp.int32)


def router(hm, w_router, bias, bm=512):
    m, d = hm.shape
    n_exp = w_router.shape[1]
    bm = _pick(m, bm)
    tk = lambda dt: jax.ShapeDtypeStruct((m, TOP_K), dt)
    tk_spec = pl.BlockSpec((bm, TOP_K), lambda i: (i, 0))
    return pl.pallas_call(
        functools.partial(_router_kernel, n_exp=n_exp),
        out_shape=(tk(jnp.int32), tk(F32), tk(jnp.int32), jax.ShapeDtypeStruct((1, n_exp), jnp.int32)),
        grid=(m // bm,),
        in_specs=[pl.BlockSpec((bm, d), lambda i: (i, 0)),
                  pl.BlockSpec((d, n_exp), lambda i: (0, 0)),
                  pl.BlockSpec((1, n_exp), lambda i: (0, 0))],
        out_specs=(tk_spec, tk_spec, tk_spec, pl.BlockSpec((1, n_exp), lambda i: (0, 0))),
        scratch_shapes=[pltpu.VMEM((1, n_exp), F32)],
        compiler_params=_params(("arbitrary",)),
        name="router",
    )(hm, w_router, bias.reshape(1, n_exp).astype(F32))


def _swiglu(x, wg, wu, wd):
    act = (jax.nn.silu(_dot(x, wg)) * _dot(x, wu)).astype(BF16)
    return _dot(act, wd)


def _shared_kernel(x_ref, wg_ref, wu_ref, wd_ref, r_ref, o_ref):
    o_ref[...] = r_ref[...] + _swiglu(x_ref[...], wg_ref[...], wu_ref[...], wd_ref[...])


def shared_expert_residual(hm, wg, wu, wd, res, bm=256):
    m, d = hm.shape
    ff = wg.shape[1]
    bm = _pick(m, bm)
    return pl.pallas_call(
        _shared_kernel,
        out_shape=jax.ShapeDtypeStruct((m, d), F32),
        grid=(m // bm,),
        in_specs=[pl.BlockSpec((bm, d), lambda i: (i, 0)),
                  pl.BlockSpec((d, ff), lambda i: (0, 0)),
                  pl.BlockSpec((d, ff), lambda i: (0, 0)),
                  pl.BlockSpec((ff, d), lambda i: (0, 0)),
                  pl.BlockSpec((bm, d), lambda i: (i, 0))],
        out_specs=pl.BlockSpec((bm, d), lambda i: (i, 0)),
        compiler_params=_params(("parallel",)),
        name="shared_expert",
    )(hm, wg, wu, wd, res)


def _pack_rows(v):
    c = v.shape[1] // 2
    lo = lax.bitcast_convert_type(v[:, :c].astype(BF16).astype(F32), jnp.uint32)
    hi = lax.bitcast_convert_type(v[:, c:].astype(BF16).astype(F32), jnp.uint32)
    return hi | (lo >> 16)


def _unpack_rows(w):
    lo = lax.bitcast_convert_type(w << 16, F32)
    hi = lax.bitcast_convert_type(w & jnp.uint32(0xFFFF0000), F32)
    return lo, hi


def _rmsnorm_pack_kernel(x_ref, g_ref, o_ref, p_ref):
    x = x_ref[...]
    ms = jnp.mean(x * x, axis=-1, keepdims=True)
    y = x * lax.rsqrt(ms + EPS) * g_ref[...]
    o_ref[...] = y.astype(o_ref.dtype)
    p_ref[...] = _pack_rows(y)


def rmsnorm_packed(x, g, bm=256):
    m, d = x.shape
    bm = _pick(m, bm)
    packed = jax.eval_shape(_pack_rows, jax.ShapeDtypeStruct((bm, d), F32))
    pw = packed.shape[1]
    return pl.pallas_call(
        _rmsnorm_pack_kernel,
        out_shape=(jax.ShapeDtypeStruct((m, d), BF16), jax.ShapeDtypeStruct((m, pw), packed.dtype)),
        grid=(m // bm,),
        in_specs=[pl.BlockSpec((bm, d), lambda i: (i, 0)),
                  pl.BlockSpec((1, d), lambda i: (0, 0))],
        out_specs=(pl.BlockSpec((bm, d), lambda i: (i, 0)),
                   pl.BlockSpec((bm, pw), lambda i: (i, 0))),
        compiler_params=_params(("parallel",)),
        name="rmsnorm_packed",
    )(x, g.reshape(1, d).astype(F32))


def _row_copy(src, src_row, dst, dst_row, sem):
    return pltpu.make_async_copy(src.at[pl.ds(src_row, 1)], dst.at[pl.ds(dst_row, 1)], sem)


def _dispatch_kernel(cnt_ref, pstart_ref, pend_ref, pos_hbm, x_hbm, o_hbm,
                     idx_smem, xbuf, zbuf, idx_sem, load_sem, row_sem):
    i = pl.program_id(0)
    n_steps = pl.num_programs(0)
    n_slots, toks = xbuf.shape[0], xbuf.shape[1]
    per = toks * TOP_K

    def idx_copy(step):
        return pltpu.make_async_copy(pos_hbm.at[step], idx_smem.at[step % 2], idx_sem.at[step % 2])

    def load(step):
        src = x_hbm.at[pl.ds(pl.multiple_of(step * toks, toks), toks)]
        return pltpu.make_async_copy(src, xbuf.at[step % n_slots], load_sem.at[step % n_slots])

    def wait_rows(step):
        pltpu.make_async_copy(o_hbm.at[pl.ds(0, per)], o_hbm.at[pl.ds(0, per)], row_sem.at[step % n_slots]).wait()

    @pl.when(i == 0)
    def _():
        zbuf[...] = jnp.zeros_like(zbuf)
        idx_copy(0).start()
        load(0).start()

    @pl.when(i >= 2)
    def _():
        wait_rows(i - 2)

    @pl.when(i + 1 < n_steps)
    def _():
        idx_copy(i + 1).start()
        load(i + 1).start()

    idx_copy(i).wait()
    load(i).wait()
    slot, islot = i % n_slots, i % 2

    def body(t, _):
        for k in range(TOP_K):
            _row_copy(xbuf.at[slot], t, o_hbm, idx_smem[islot, t * TOP_K + k], row_sem.at[slot]).start()
        return 0

    lax.fori_loop(0, toks, body, 0)

    @pl.when(i == n_steps - 1)
    def _():
        @pl.when(i >= 1)
        def _():
            wait_rows(i - 1)

        wait_rows(i)

        def expert_padding(e, _):
            first = pstart_ref[e] + cnt_ref[e]
            n_pad = pend_ref[e] - first

            def zbody(r, _):
                _row_copy(zbuf, 0, o_hbm, first + r, row_sem.at[0]).start()
                return 0

            def zwait(r, _):
                _row_copy(zbuf, 0, o_hbm, first, row_sem.at[0]).wait()
                return 0

            lax.fori_loop(0, n_pad, zbody, 0)
            lax.fori_loop(0, n_pad, zwait, 0)
            return 0

        lax.fori_loop(0, cnt_ref.shape[0], expert_padding, 0)

        blk_rows = zbuf.shape[0]
        n_exp = cnt_ref.shape[0]
        first_blk = pend_ref[n_exp - 1] // blk_rows
        n_tail = o_hbm.shape[0] // blk_rows - first_blk

        def tail_copy(b):
            dst = o_hbm.at[pl.ds(pl.multiple_of((first_blk + b) * blk_rows, blk_rows), blk_rows)]
            return pltpu.make_async_copy(zbuf, dst, row_sem.at[0])

        def tbody(b, _):
            tail_copy(b).start()
            return 0

        def twait(b, _):
            tail_copy(b).wait()
            return 0

        lax.fori_loop(0, n_tail, tbody, 0)
        lax.fori_loop(0, n_tail, twait, 0)


def dispatch_rows(x_rows, pos, counts, pad_start, pad_end, cap, blk_rows, chunk_tokens=512):
    n_tok, width = x_rows.shape
    toks = _pick(n_tok, chunk_tokens)
    steps = n_tok // toks
    n_slots = 3
    grid_spec = pltpu.PrefetchScalarGridSpec(
        num_scalar_prefetch=3,
        grid=(steps,),
        in_specs=[pl.BlockSpec(memory_space=pl.ANY)] * 2,
        out_specs=pl.BlockSpec(memory_space=pl.ANY),
        scratch_shapes=[pltpu.SMEM((2, toks * TOP_K), jnp.int32),
                        pltpu.VMEM((n_slots, toks, width), x_rows.dtype),
                        pltpu.VMEM((blk_rows, width), x_rows.dtype),
                        pltpu.SemaphoreType.DMA((2,)),
                        pltpu.SemaphoreType.DMA((n_slots,)),
                        pltpu.SemaphoreType.DMA((n_slots,))],
    )
    return pl.pallas_call(
        _dispatch_kernel,
        out_shape=jax.ShapeDtypeStruct((cap, width), x_rows.dtype),
        grid_spec=grid_spec,
        compiler_params=_params(("arbitrary",)),
        name="moe_dispatch",
    )(counts, pad_start, pad_end, pos.reshape(steps, toks * TOP_K), x_rows)


def _expert_kernel(bexp_ref, nused_ref, x_ref, wg_ref, wu_ref, wd_ref, o_ref):
    i = pl.program_id(0)

    @pl.when(i < nused_ref[0])
    def _():
        lo, hi = _unpack_rows(x_ref[...])
        x = jnp.concatenate([lo, hi], axis=1).astype(BF16)
        o_ref[...] = _pack_rows(_swiglu(x, wg_ref[...], wu_ref[...], wd_ref[...]))

    @pl.when(i >= nused_ref[0])
    def _():
        o_ref[...] = jnp.zeros_like(o_ref)


def routed_experts_sorted(x_sorted, block_expert, n_used, wg, wu, wd, rows):
    cap, width = x_sorted.shape
    nb = cap // rows
    d, ff = wg.shape[1], wg.shape[2]
    used = lambda i, be, nu: (jnp.maximum(jnp.minimum(i, nu[0] - 1), 0), 0)
    grid_spec = pltpu.PrefetchScalarGridSpec(
        num_scalar_prefetch=2,
        grid=(nb,),
        in_specs=[pl.BlockSpec((rows, width), used),
                  pl.BlockSpec((None, d, ff), lambda i, be, nu: (be[i], 0, 0)),
                  pl.BlockSpec((None, d, ff), lambda i, be, nu: (be[i], 0, 0)),
                  pl.BlockSpec((None, ff, d), lambda i, be, nu: (be[i], 0, 0))],
        out_specs=pl.BlockSpec((rows, width), lambda i, be, nu: (i, 0)),
    )
    return pl.pallas_call(
        _expert_kernel,
        out_shape=jax.ShapeDtypeStruct((cap, width), x_sorted.dtype),
        grid_spec=grid_spec,
        compiler_params=_params(("arbitrary",)),
        name="routed_experts",
    )(block_expert, n_used, x_sorted, wg, wu, wd)


def _combine_kernel(pos_hbm, y_hbm, base_ref, wt_ref, g_ref, h_ref, hn_ref, idx_smem, gbuf, idx_sem, row_sem):
    i = pl.program_id(0)
    n_steps = pl.num_programs(0)
    toks = gbuf.shape[2]

    def fetch(blk, slot):
        cp = pltpu.make_async_copy(pos_hbm.at[blk], idx_smem.at[slot], idx_sem)
        cp.start()
        cp.wait()

        def body(r, _):
            for k in range(TOP_K):
                _row_copy(y_hbm, idx_smem[slot, r * TOP_K + k], gbuf.at[slot, k], r, row_sem.at[slot]).start()
            return 0

        lax.fori_loop(0, toks, body, 0)

    @pl.when(i == 0)
    def _():
        fetch(0, 0)

    @pl.when(i + 1 < n_steps)
    def _():
        fetch(i + 1, (i + 1) % 2)

    slot = i % 2
    for k in range(TOP_K):
        pltpu.make_async_copy(y_hbm.at[pl.ds(0, toks)], gbuf.at[slot, k], row_sem.at[slot]).wait()
    wt = wt_ref[...]
    acc_lo, acc_hi = None, None
    for k in range(TOP_K):
        lo, hi = _unpack_rows(gbuf[slot, k])
        wk = wt[:, k:k + 1]
        acc_lo = lo * wk if acc_lo is None else acc_lo + lo * wk
        acc_hi = hi * wk if acc_hi is None else acc_hi + hi * wk
    h = base_ref[...] + jnp.concatenate([acc_lo, acc_hi], axis=1)
    h_ref[...] = h
    ms = jnp.mean(h * h, axis=-1, keepdims=True)
    hn_ref[...] = (h * lax.rsqrt(ms + EPS) * g_ref[...]).astype(hn_ref.dtype)


def combine(pos, y_sorted, base, wts, gain, tokens=COMBINE_TOKENS):
    m, d = base.shape
    toks = _pick(m, tokens)
    steps, per = m // toks, toks * TOP_K
    pos = pos.reshape(steps, per)
    return pl.pallas_call(
        _combine_kernel,
        out_shape=(jax.ShapeDtypeStruct((m, d), F32), jax.ShapeDtypeStruct((m, d), BF16)),
        grid=(steps,),
        in_specs=[pl.BlockSpec(memory_space=pl.ANY),
                  pl.BlockSpec(memory_space=pl.ANY),
                  pl.BlockSpec((toks, d), lambda i: (i, 0)),
                  pl.BlockSpec((toks, TOP_K), lambda i: (i, 0)),
                  pl.BlockSpec((1, d), lambda i: (0, 0))],
        out_specs=(pl.BlockSpec((toks, d), lambda i: (i, 0)),
                   pl.BlockSpec((toks, d), lambda i: (i, 0))),
        scratch_shapes=[pltpu.SMEM((2, per), jnp.int32),
                        pltpu.VMEM((2, TOP_K, toks, y_sorted.shape[1]), y_sorted.dtype),
                        pltpu.SemaphoreType.DMA(()),
                        pltpu.SemaphoreType.DMA((2,))],
        compiler_params=_params(("arbitrary",)),
        name="moe_combine",
    )(pos, y_sorted, base, wts, gain.reshape(1, d).astype(F32))


def dispatch_tables(eidx, rank, counts, rows):
    n_tok = eidx.shape[0]
    n_exp = counts.shape[0]
    nb = n_tok * TOP_K // rows + n_exp
    pad_end = jnp.cumsum((counts + rows - 1) // rows * rows)
    pad_start = pad_end - (counts + rows - 1) // rows * rows
    experts = jnp.arange(n_exp, dtype=jnp.int32)
    pos = rank + jnp.sum(jnp.where(eidx[..., None] == experts, pad_start, 0), axis=-1)
    blocks = jnp.arange(nb, dtype=jnp.int32)
    block_expert = jnp.minimum(jnp.sum(pad_end[None, :] // rows <= blocks[:, None], axis=-1), n_exp - 1)
    n_used = pad_end[-1:] // rows
    i32 = lambda a: a.astype(jnp.int32)
    return i32(pos).reshape(-1), i32(block_expert), i32(n_used), i32(pad_start), i32(pad_end), nb * rows


def _ple_kernel(hn_ref, wg_ref, p_ref, wp_ref, h_ref, o_ref):
    gate = jax.nn.sigmoid(_dot(hn_ref[...], wg_ref[...]))
    o_ref[...] = h_ref[...] + gate * _dot(p_ref[...].astype(BF16), wp_ref[...])


def ple_gate(hn, w_gate, p, w_ple, h, bm=512, bn=1024):
    m, d = hn.shape
    n = w_gate.shape[1]
    pd = p.shape[1]
    bm, bn = _pick(m, bm), _pick(n, bn)
    return pl.pallas_call(
        _ple_kernel,
        out_shape=jax.ShapeDtypeStruct((m, n), F32),
        grid=(m // bm, n // bn),
        in_specs=[pl.BlockSpec((bm, d), lambda i, j: (i, 0)),
                  pl.BlockSpec((d, bn), lambda i, j: (0, j)),
                  pl.BlockSpec((bm, pd), lambda i, j: (i, 0)),
                  pl.BlockSpec((pd, bn), lambda i, j: (0, j)),
                  pl.BlockSpec((bm, bn), lambda i, j: (i, j))],
        out_specs=pl.BlockSpec((bm, bn), lambda i, j: (i, j)),
        compiler_params=_params(("parallel", "parallel")),
        name="ple_gate",
    )(hn, w_gate, p, w_ple, h)


def _layer(h, p_i, prm, batch):
    (mix_norm, w_in, a_re, a_im, log_step, b_re, b_im, c_re, c_im, ssm_d, w_glu_a, w_glu_b, q_norm, k_norm,
     sinks, w_bs, w_ba, w_out, moe_norm, w_router, router_bias, we_gate, we_up, we_down, ws_gate, ws_up,
     ws_down, ple_norm, w_ple, w_ple_gate) = prm
    n_tok, d = h.shape
    seq = n_tok // batch
    ssm_w = w_glu_a.shape[0]
    n_q = sinks.shape[0]
    attn_w = n_q * HEAD_DIM
    kv_w = attn_w // Q_PER_KV
    n_exp = w_router.shape[1]
    c0, c1 = ssm_w, ssm_w + attn_w + 2 * kv_w
    bf = lambda w: w.astype(BF16)

    hn = rmsnorm(h, mix_norm, BF16)
    bm_u = _pick(seq, 1024)
    bn_u = _pick(ssm_w, 1024)
    tpb, npb = seq // bm_u, ssm_w // bn_u
    u_tm = matmul(hn, bf(w_in[:, :c0]), BF16, bm=bm_u, bn=bn_u, out_shape=(seq, batch * ssm_w),
                  out_map=lambda i, j: (i % tpb, (i // tpb) * npb + j), name="proj_u_time_major")
    qkv = matmul(hn, bf(w_in[:, c0:c1]), BF16, bn=(c1 - c0) // 2, name="proj_qkv")
    gates = matmul(hn, bf(w_in[:, c1:]), BF16, name="proj_gates")
    y_pre = s5_scan(u_tm.reshape(n_tok, ssm_w), batch, a_re, a_im, log_step, b_re, b_im, c_re, c_im, ssm_d)
    y_ssm = glu_from_time_major(y_pre, batch, bf(w_glu_a), bf(w_glu_b))
    y_attn = swa_attention(qkv, batch, n_q, q_norm, k_norm, sinks)
    merged = branch_merge(y_ssm, bf(w_bs), y_attn, bf(w_ba), gates)
    h = matmul_residual(merged, bf(w_out), h)

    hm, hm_rows = rmsnorm_packed(h, moe_norm)
    eidx, ew, rank, counts = router(hm, bf(w_router), router_bias)
    counts = counts.reshape(n_exp)
    pos, block_expert, n_used, pad_start, pad_end, cap = dispatch_tables(eidx, rank, counts, MOE_ROWS)
    x_sorted = dispatch_rows(hm_rows, pos, counts, pad_start, pad_end, cap, MOE_ROWS)
    base = shared_expert_residual(hm, bf(ws_gate), bf(ws_up), bf(ws_down), h)
    y_sorted = routed_experts_sorted(x_sorted, block_expert, n_used, bf(we_gate), bf(we_up), bf(we_down), MOE_ROWS)
    h, hn3 = combine(pos, y_sorted, base, ew, ple_norm)

    return ple_gate(hn3, bf(w_ple_gate), p_i, bf(w_ple), h)


def kernel(x, p, mix_norm, w_in, ssm_a_re, ssm_a_im, ssm_log_step, ssm_b_re, ssm_b_im, ssm_c_re, ssm_c_im, ssm_d, w_glu_a, w_glu_b, q_norm, k_norm, attn_sinks, w_branch_ssm, w_branch_attn, w_out, moe_norm, w_router, router_bias, we_gate, we_up, we_down, ws_gate, ws_up, ws_down, ple_norm, w_ple, w_ple_gate):
    bsz, seq, d = x.shape
    layer_params = (mix_norm, w_in, ssm_a_re, ssm_a_im, ssm_log_step, ssm_b_re, ssm_b_im, ssm_c_re, ssm_c_im,
                    ssm_d, w_glu_a, w_glu_b, q_norm, k_norm, attn_sinks, w_branch_ssm, w_branch_attn, w_out,
                    moe_norm, w_router, router_bias, we_gate, we_up, we_down, ws_gate, ws_up, ws_down,
                    ple_norm, w_ple, w_ple_gate)
    h = x.reshape(bsz * seq, d)
    for i in range(mix_norm.shape[0]):
        h = _layer(h, p[i].reshape(bsz * seq, -1), tuple(w[i] for w in layer_params), bsz)
    return h.reshape(bsz, seq, d)
```

```python
import functools
import math

import jax
import jax.numpy as jnp
from jax import lax
from jax.experimental import pallas as pl
from jax.experimental.pallas import tpu as pltpu

SSM_GROUP = 16
SSM_STATE = 64
S5_CHUNK = 8
S5_TILE_LANES = 128
HEAD_DIM = 64
Q_PER_KV = 8
ATTN_BLOCK = 128
ALIBI_MAX_BIAS = 8.0
N_EXPERT_GROUPS = 8
TOPK_GROUPS = 4
TOP_K = 8
ROUTED_SCALE = 2.5
EPS = 1e-6
MOE_ROWS = 256
COMBINE_TOKENS = 64
V7X_VMEM_LIMIT = 56 * 1024 * 1024

BF16 = jnp.bfloat16
F32 = jnp.float32


def _dot(a, b):
    return jnp.dot(a, b, preferred_element_type=F32)


def _params(sem, vmem=V7X_VMEM_LIMIT):
    return pltpu.CompilerParams(dimension_semantics=sem, vmem_limit_bytes=vmem)


def _pick(n, pref):
    b = min(n, pref)
    while n % b:
        b //= 2
    return b


def _rmsnorm_kernel(x_ref, g_ref, o_ref):
    x = x_ref[...]
    ms = jnp.mean(x * x, axis=-1, keepdims=True)
    o_ref[...] = (x * lax.rsqrt(ms + EPS) * g_ref[...]).astype(o_ref.dtype)


def rmsnorm(x, g, out_dtype, bm=256):
    m, d = x.shape
    bm = _pick(m, bm)
    return pl.pallas_call(
        _rmsnorm_kernel,
        out_shape=jax.ShapeDtypeStruct((m, d), out_dtype),
        grid=(m // bm,),
        in_specs=[pl.BlockSpec((bm, d), lambda i: (i, 0)),
                  pl.BlockSpec((1, d), lambda i: (0, 0))],
        out_specs=pl.BlockSpec((bm, d), lambda i: (i, 0)),
        compiler_params=_params(("parallel",)),
        name="rmsnorm",
    )(x, g.reshape(1, d).astype(F32))


def _mm_kernel(x_ref, w_ref, o_ref):
    o_ref[...] = _dot(x_ref[...], w_ref[...]).astype(o_ref.dtype)


def matmul(x, w, out_dtype, bm=1024, bn=1024, name="matmul"):
    m, k = x.shape
    n = w.shape[1]
    bm, bn = _pick(m, bm), _pick(n, bn)
    return pl.pallas_call(
        _mm_kernel,
        out_shape=jax.ShapeDtypeStruct((m, n), out_dtype),
        grid=(m // bm, n // bn),
        in_specs=[pl.BlockSpec((bm, k), lambda i, j: (i, 0)),
                  pl.BlockSpec((k, bn), lambda i, j: (0, j))],
        out_specs=pl.BlockSpec((bm, bn), lambda i, j: (i, j)),
        compiler_params=_params(("parallel", "parallel")),
        name=name,
    )(x, w)


def _s5_kernel(u_ref, w_ref, e_ref, pw_ref, d_ref, o_ref, fu_ref):
    tc = S5_CHUNK
    n_chunks = u_ref.shape[0] // tc
    lanes = u_ref.shape[1]
    half = fu_ref.shape[1] // 2
    x = jnp.concatenate([u_ref[pl.ds(i, n_chunks, stride=tc), :] for i in range(tc)], axis=1)
    z = _dot(x.astype(BF16), w_ref[...])
    fu_ref[...] = z[:, tc * lanes:]
    row = lax.broadcasted_iota(jnp.int32, (8, half), 0)
    pw = [pw_ref[k] for k in range(10)]

    def cmul(ar, ai, xr, xi):
        return ar * xr - ai * xi, ar * xi + ai * xr

    def body(t, carry):
        c_re, c_im = carry
        r0 = pl.multiple_of(t * 8, 8)
        x_re = fu_ref[pl.ds(r0, 8), pl.ds(0, half)]
        x_im = fu_ref[pl.ds(r0, 8), pl.ds(half, half)]
        for k, sh in enumerate((1, 2, 4)):
            s_re = jnp.where(row >= sh, pltpu.roll(x_re, sh, 0), 0.0)
            s_im = jnp.where(row >= sh, pltpu.roll(x_im, sh, 0), 0.0)
            m_re, m_im = cmul(pw[2 * k], pw[2 * k + 1], s_re, s_im)
            x_re, x_im = x_re + m_re, x_im + m_im
        p_re, p_im = cmul(pw[6], pw[7], c_re, c_im)
        fu_ref[pl.ds(r0, 8), pl.ds(0, half)] = jnp.where(row >= 1, pltpu.roll(x_re, 1, 0), 0.0) + p_re
        fu_ref[pl.ds(r0, 8), pl.ds(half, half)] = jnp.where(row >= 1, pltpu.roll(x_im, 1, 0), 0.0) + p_im
        f_re, f_im = cmul(pw[8], pw[9], c_re, c_im)
        f_re, f_im = f_re + x_re, f_im + x_im
        return jnp.broadcast_to(f_re[7:8], (8, half)), jnp.broadcast_to(f_im[7:8], (8, half))

    zero = jnp.zeros((8, half), F32)
    lax.fori_loop(0, n_chunks // 8, body, (zero, zero))
    y = z[:, :tc * lanes] + _dot(fu_ref[...].astype(BF16), e_ref[...]) + d_ref[...] * x
    y = jax.nn.gelu(y)
    for j in range(tc):
        o_ref[pl.ds(j, n_chunks, stride=tc), :] = y[:, j * lanes:(j + 1) * lanes]


def s5_scan(u, batch, a_re, a_im, log_step, b_re, b_im, c_re, c_im, d_skip):
    rows, width = u.shape
    seq = rows // batch
    tc, lanes = S5_CHUNK, S5_TILE_LANES
    gpt = lanes // SSM_GROUP
    n_tiles = width // lanes
    half = gpt * SSM_STATE
    assert seq % (8 * tc) == 0 and width % lanes == 0
    lam = lax.complex(a_re.astype(F32), a_im.astype(F32))
    lam_dt = lam * jnp.exp(log_step.astype(F32))[:, None]
    b_bar = ((jnp.exp(lam_dt) - 1.0) / lam)[..., None] * lax.complex(b_re.astype(F32), b_im.astype(F32))
    c_mat = lax.complex(c_re.astype(F32), c_im.astype(F32))
    lp = jnp.exp(lam_dt[None] * jnp.arange(tc + 1, dtype=F32)[:, None, None])
    kern = jnp.real(jnp.einsum('gcn,tgn,gnd->tgcd', c_mat, lp[:tc], b_bar))
    ii = jnp.arange(tc)
    lag = ii[None, :] - ii[:, None]
    toep = jnp.where((lag >= 0)[:, :, None, None, None], kern[jnp.clip(lag, 0, tc - 1)], 0.0)
    eye = jnp.eye(gpt, dtype=F32)
    tile = lambda a, g_axis: a.reshape(a.shape[:g_axis] + (n_tiles, gpt) + a.shape[g_axis + 1:])
    t_m = jnp.einsum('ijsgcd,gh->sigdjhc', tile(toep, 2), eye).reshape(n_tiles, tc * lanes, tc * lanes)
    f_c = lp[tc - 1 - ii][..., None] * b_bar[None]
    f_m = lambda a: jnp.einsum('isgnd,gh->sigdhn', tile(a, 1), eye).reshape(n_tiles, tc * lanes, half)
    e_c = c_mat[None] * lp[1:, :, None, :]
    e_m = lambda a: jnp.einsum('jsgcn,gh->sgnjhc', tile(a, 1), eye).reshape(n_tiles, half, tc * lanes)
    w_all = jnp.concatenate([t_m, f_m(jnp.real(f_c)), f_m(jnp.imag(f_c))], axis=2).astype(BF16)
    e_all = jnp.concatenate([e_m(jnp.real(e_c)), e_m(-jnp.imag(e_c))], axis=1).astype(BF16)
    r8 = jnp.arange(8, dtype=F32)[:, None, None]
    a_pow = lambda e: jnp.exp(lam_dt[None] * (tc * e))
    plist = [a_pow(jnp.full_like(r8, e)) for e in (1.0, 2.0, 4.0)] + [a_pow(r8), a_pow(r8 + 1.0)]
    pws = jnp.stack([f(p) for p in plist for f in (jnp.real, jnp.imag)])
    pws = pws.reshape(10, 8, n_tiles, half).transpose(2, 0, 1, 3)
    dsk = jnp.tile(d_skip.astype(F32).reshape(n_tiles, 1, lanes), (1, 1, tc))
    return pl.pallas_call(
        _s5_kernel,
        out_shape=jax.ShapeDtypeStruct((rows, width), F32),
        grid=(n_tiles, batch),
        in_specs=[pl.BlockSpec((seq, lanes), lambda s, b: (b, s)),
                  pl.BlockSpec((None, tc * lanes, tc * lanes + 2 * half), lambda s, b: (s, 0, 0)),
                  pl.BlockSpec((None, 2 * half, tc * lanes), lambda s, b: (s, 0, 0)),
                  pl.BlockSpec((None, 10, 8, half), lambda s, b: (s, 0, 0, 0)),
                  pl.BlockSpec((None, 1, tc * lanes), lambda s, b: (s, 0, 0))],
        out_specs=pl.BlockSpec((seq, lanes), lambda s, b: (b, s)),
        scratch_shapes=[pltpu.VMEM((seq // tc, 2 * half), F32)],
        compiler_params=_params(("parallel", "parallel")),
        name="s5_scan",
    )(u, w_all, e_all, pws, dsk)


def _glu_kernel(x_ref, wa_ref, wb_ref, o_ref):
    x = x_ref[...].astype(BF16)
    o_ref[...] = (_dot(x, wa_ref[...]) * jax.nn.sigmoid(_dot(x, wb_ref[...]))).astype(o_ref.dtype)


def glu(x, wa, wb, bm=1024, bn=1024):
    m, k = x.shape
    n = wa.shape[1]
    bm, bn = _pick(m, bm), _pick(n, bn)
    return pl.pallas_call(
        _glu_kernel,
        out_shape=jax.ShapeDtypeStruct((m, n), BF16),
        grid=(m // bm, n // bn),
        in_specs=[pl.BlockSpec((bm, k), lambda i, j: (i, 0)),
                  pl.BlockSpec((k, bn), lambda i, j: (0, j)),
                  pl.BlockSpec((k, bn), lambda i, j: (0, j))],
        out_specs=pl.BlockSpec((bm, bn), lambda i, j: (i, j)),
        compiler_params=_params(("parallel", "parallel")),
        name="ssm_glu",
    )(x, wa, wb)


def _attn_kernel(sink_ref, q_ref, kc_ref, kp_ref, vc_ref, vp_ref, qg_ref, kg_ref, o_ref, *, n_kv):
    blk, hd = ATTN_BLOCK, HEAD_DIM
    first = pl.program_id(1) == 0
    n_q = n_kv * Q_PER_KV
    kj = lax.broadcasted_iota(jnp.int32, (2 * blk, blk), 0)
    qi = lax.broadcasted_iota(jnp.int32, (2 * blk, blk), 1)
    dist = qi + blk - kj
    kmin = jnp.where(first, blk, 0)
    valid = (dist >= 0) & (dist < blk) & (kj >= kmin)
    dist_f = dist.astype(F32)

    def head_norm_t(t, heads, gain):
        t3 = t.reshape(heads, hd, t.shape[1])
        ms = jnp.mean(t3 * t3, axis=1, keepdims=True)
        return t3 * lax.rsqrt(ms + EPS) * gain[None]

    qn = (head_norm_t(q_ref[...].astype(F32).T, n_q, qg_ref[...]) * (hd ** -0.5)).astype(BF16)
    kk = jnp.concatenate([kp_ref[...], kc_ref[...]], axis=0).astype(F32)
    kn = head_norm_t(kk.T, n_kv, kg_ref[...]).reshape(n_kv * hd, 2 * blk).T.astype(BF16)
    vt = jnp.concatenate([vp_ref[...], vc_ref[...]], axis=0).astype(F32).T.astype(BF16)

    for h in range(n_kv):
        qt = jnp.concatenate([qn[h * Q_PER_KV + g] for g in range(Q_PER_KV)], axis=1)
        st = _dot(kn[:, h * hd:(h + 1) * hd], qt)
        probs = []
        for g in range(Q_PER_KV):
            head = h * Q_PER_KV + g
            slope = 2.0 ** (-ALIBI_MAX_BIAS * (head + 1) / n_q)
            s = jnp.where(valid, st[:, g * blk:(g + 1) * blk] - slope * dist_f, -jnp.inf)
            sink = sink_ref[head]
            m = jnp.maximum(jnp.max(s, axis=0, keepdims=True), sink)
            p = jnp.exp(s - m)
            denom = jnp.sum(p, axis=0, keepdims=True) + jnp.exp(sink - m)
            probs.append((p * (1.0 / denom)).astype(BF16))
        ot = _dot(vt[h * hd:(h + 1) * hd, :], jnp.concatenate(probs, axis=1))
        ot = jnp.concatenate([ot[:, g * blk:(g + 1) * blk] for g in range(Q_PER_KV)], axis=0)
        o_ref[:, h * Q_PER_KV * hd:(h + 1) * Q_PER_KV * hd] = ot.T.astype(o_ref.dtype)


def swa_attention(qkv, batch, n_q, q_gain, k_gain, sinks):
    rows = qkv.shape[0]
    n_kv = n_q // Q_PER_KV
    qw, kw = n_q * HEAD_DIM, n_kv * HEAD_DIM
    blk = ATTN_BLOCK
    nb = rows // batch // blk
    kcol = qw // kw
    cur = lambda b, j: b * nb + j
    prev = lambda b, j: b * nb + jnp.maximum(j - 1, 0)
    return pl.pallas_call(
        functools.partial(_attn_kernel, n_kv=n_kv),
        out_shape=jax.ShapeDtypeStruct((rows, qw), BF16),
        grid=(batch, nb),
        in_specs=[pl.BlockSpec(memory_space=pltpu.SMEM),
                  pl.BlockSpec((blk, qw), lambda b, j: (cur(b, j), 0)),
                  pl.BlockSpec((blk, kw), lambda b, j: (cur(b, j), kcol)),
                  pl.BlockSpec((blk, kw), lambda b, j: (prev(b, j), kcol)),
                  pl.BlockSpec((blk, kw), lambda b, j: (cur(b, j), kcol + 1)),
                  pl.BlockSpec((blk, kw), lambda b, j: (prev(b, j), kcol + 1)),
                  pl.BlockSpec((HEAD_DIM, blk), lambda b, j: (0, 0)),
                  pl.BlockSpec((HEAD_DIM, 2 * blk), lambda b, j: (0, 0))],
        out_specs=pl.BlockSpec((blk, qw), lambda b, j: (cur(b, j), 0)),
        compiler_params=_params(("parallel", "arbitrary")),
        name="swa_attention",
    )(sinks.astype(F32), qkv, qkv, qkv, qkv, qkv,
      jnp.broadcast_to(q_gain.astype(F32)[:, None], (HEAD_DIM, blk)),
      jnp.broadcast_to(k_gain.astype(F32)[:, None], (HEAD_DIM, 2 * blk)))


def _merge_kernel(ys_ref, ws_ref, ya_ref, wa_ref, gs_ref, ga_ref, o_ref):
    s = jax.nn.sigmoid(gs_ref[...].astype(F32)) * _dot(ys_ref[...], ws_ref[...])
    a = jax.nn.sigmoid(ga_ref[...].astype(F32)) * _dot(ya_ref[...], wa_ref[...])
    o_ref[...] = (s + a).astype(o_ref.dtype)


def branch_merge(y_ssm, w_ssm, y_attn, w_attn, gates, bm=1024, bn=1024):
    m, ks = y_ssm.shape
    ka = y_attn.shape[1]
    n = w_ssm.shape[1]
    bm, bn = _pick(m, bm), _pick(n, bn)
    nj = n // bn
    return pl.pallas_call(
        _merge_kernel,
        out_shape=jax.ShapeDtypeStruct((m, n), BF16),
        grid=(m // bm, nj),
        in_specs=[pl.BlockSpec((bm, ks), lambda i, j: (i, 0)),
                  pl.BlockSpec((ks, bn), lambda i, j: (0, j)),
                  pl.BlockSpec((bm, ka), lambda i, j: (i, 0)),
                  pl.BlockSpec((ka, bn), lambda i, j: (0, j)),
                  pl.BlockSpec((bm, bn), lambda i, j: (i, j)),
                  pl.BlockSpec((bm, bn), lambda i, j: (i, nj + j))],
        out_specs=pl.BlockSpec((bm, bn), lambda i, j: (i, j)),
        compiler_params=_params(("parallel", "parallel")),
        name="branch_merge",
    )(y_ssm, w_ssm, y_attn, w_attn, gates, gates)


def _mm_res_kernel(x_ref, w_ref, r_ref, o_ref):
    o_ref[...] = r_ref[...] + _dot(x_ref[...], w_ref[...])


def matmul_residual(x, w, res, bm=1024, bn=1024):
    m, k = x.shape
    n = w.shape[1]
    bm, bn = _pick(m, bm), _pick(n, bn)
    return pl.pallas_call(
        _mm_res_kernel,
        out_shape=jax.ShapeDtypeStruct((m, n), F32),
        grid=(m // bm, n // bn),
        in_specs=[pl.BlockSpec((bm, k), lambda i, j: (i, 0)),
                  pl.BlockSpec((k, bn), lambda i, j: (0, j)),
                  pl.BlockSpec((bm, bn), lambda i, j: (i, j))],
        out_specs=pl.BlockSpec((bm, bn), lambda i, j: (i, j)),
        compiler_params=_params(("parallel", "parallel")),
        name="out_proj_residual",
    )(x, w, res)


def _router_kernel(x_ref, w_ref, b_ref, idx_ref, wt_ref, rank_ref, cnt_ref, seen_ref, *, n_exp):
    @pl.when(pl.program_id(0) == 0)
    def _():
        seen_ref[...] = jnp.zeros_like(seen_ref)

    s = jax.nn.sigmoid(_dot(x_ref[...], w_ref[...]))
    sel = s + b_ref[...]
    rows = s.shape[0]
    lane = lax.broadcasted_iota(jnp.int32, (rows, n_exp), 1)
    per_group = n_exp // N_EXPERT_GROUPS
    grp = lane // per_group
    neg = -jnp.inf
    gscore = jnp.zeros_like(sel)
    for g in range(N_EXPERT_GROUPS):
        in_g = grp == g
        v = jnp.where(in_g, sel, neg)
        m1 = jnp.max(v, axis=-1, keepdims=True)
        i1 = jnp.min(jnp.where(v == m1, lane, n_exp), axis=-1, keepdims=True)
        m2 = jnp.max(jnp.where(lane == i1, neg, v), axis=-1, keepdims=True)
        gscore = jnp.where(in_g, m1 + m2, gscore)
    cand = jnp.full_like(sel, neg)
    remaining = gscore
    for _ in range(TOPK_GROUPS):
        gm = jnp.max(remaining, axis=-1, keepdims=True)
        gi = jnp.min(jnp.where(remaining == gm, grp, N_EXPERT_GROUPS), axis=-1, keepdims=True)
        hit = grp == gi
        cand = jnp.where(hit, sel, cand)
        remaining = jnp.where(hit, neg, remaining)
    slot = lax.broadcasted_iota(jnp.int32, (rows, TOP_K), 1)
    idx = jnp.zeros((rows, TOP_K), jnp.int32)
    wts = jnp.zeros((rows, TOP_K), F32)
    total = jnp.zeros((rows, 1), F32)
    picked = jnp.zeros_like(sel)
    hits = []
    for k in range(TOP_K):
        mx = jnp.max(cand, axis=-1, keepdims=True)
        ei = jnp.min(jnp.where(cand == mx, lane, n_exp), axis=-1, keepdims=True)
        hit = lane == ei
        wk = jnp.sum(jnp.where(hit, s, 0.0), axis=-1, keepdims=True)
        idx = jnp.where(slot == k, ei, idx)
        wts = jnp.where(slot == k, wk, wts)
        total = total + wk
        cand = jnp.where(hit, neg, cand)
        picked = jnp.where(hit, 1.0, picked)
        hits.append(hit)
    idx_ref[...] = idx
    wt_ref[...] = wts / total * ROUTED_SCALE
    r_i = lax.broadcasted_iota(jnp.int32, (rows, rows), 0)
    c_i = lax.broadcasted_iota(jnp.int32, (rows, rows), 1)
    lower = jnp.where(c_i < r_i, 1.0, 0.0).astype(BF16)
    before = _dot(lower, picked.astype(BF16)) + seen_ref[...]
    rank = jnp.zeros((rows, TOP_K), F32)
    for k in range(TOP_K):
        rank = jnp.where(slot == k, jnp.sum(jnp.where(hits[k], before, 0.0), axis=-1, keepdims=True), rank)
    rank_ref[...] = rank.astype(jnp.int32)
    seen = seen_ref[...] + jnp.sum(picked, axis=0, keepdims=True)
    seen_ref[...] = seen
    cnt_ref[...] = seen.astype(jnp.int32)


def router(hm, w_router, bias, bm=512):
    m, d = hm.shape
    n_exp = w_router.shape[1]
    bm = _pick(m, bm)
    tk = lambda dt: jax.ShapeDtypeStruct((m, TOP_K), dt)
    tk_spec = pl.BlockSpec((bm, TOP_K), lambda i: (i, 0))
    return pl.pallas_call(
        functools.partial(_router_kernel, n_exp=n_exp),
        out_shape=(tk(jnp.int32), tk(F32), tk(jnp.int32), jax.ShapeDtypeStruct((1, n_exp), jnp.int32)),
        grid=(m // bm,),
        in_specs=[pl.BlockSpec((bm, d), lambda i: (i, 0)),
                  pl.BlockSpec((d, n_exp), lambda i: (0, 0)),
                  pl.BlockSpec((1, n_exp), lambda i: (0, 0))],
        out_specs=(tk_spec, tk_spec, tk_spec, pl.BlockSpec((1, n_exp), lambda i: (0, 0))),
        scratch_shapes=[pltpu.VMEM((1, n_exp), F32)],
        compiler_params=_params(("arbitrary",)),
        name="router",
    )(hm, w_router, bias.reshape(1, n_exp).astype(F32))


def _swiglu(x, wg, wu, wd):
    act = (jax.nn.silu(_dot(x, wg)) * _dot(x, wu)).astype(BF16)
    return _dot(act, wd)


def _shared_kernel(x_ref, wg_ref, wu_ref, wd_ref, r_ref, o_ref):
    o_ref[...] = r_ref[...] + _swiglu(x_ref[...], wg_ref[...], wu_ref[...], wd_ref[...])


def shared_expert_residual(hm, wg, wu, wd, res, bm=256):
    m, d = hm.shape
    ff = wg.shape[1]
    bm = _pick(m, bm)
    return pl.pallas_call(
        _shared_kernel,
        out_shape=jax.ShapeDtypeStruct((m, d), F32),
        grid=(m // bm,),
        in_specs=[pl.BlockSpec((bm, d), lambda i: (i, 0)),
                  pl.BlockSpec((d, ff), lambda i: (0, 0)),
                  pl.BlockSpec((d, ff), lambda i: (0, 0)),
                  pl.BlockSpec((ff, d), lambda i: (0, 0)),
                  pl.BlockSpec((bm, d), lambda i: (i, 0))],
        out_specs=pl.BlockSpec((bm, d), lambda i: (i, 0)),
        compiler_params=_params(("parallel",)),
        name="shared_expert",
    )(hm, wg, wu, wd, res)


def _pack_rows(v):
    c = v.shape[1] // 2
    lo = lax.bitcast_convert_type(v[:, :c].astype(BF16).astype(F32), jnp.uint32)
    hi = lax.bitcast_convert_type(v[:, c:].astype(BF16).astype(F32), jnp.uint32)
    return hi | (lo >> 16)


def _unpack_rows(w):
    lo = lax.bitcast_convert_type(w << 16, F32)
    hi = lax.bitcast_convert_type(w & jnp.uint32(0xFFFF0000), F32)
    return lo, hi


def _rmsnorm_pack_kernel(x_ref, g_ref, o_ref, p_ref):
    x = x_ref[...]
    ms = jnp.mean(x * x, axis=-1, keepdims=True)
    y = x * lax.rsqrt(ms + EPS) * g_ref[...]
    o_ref[...] = y.astype(o_ref.dtype)
    p_ref[...] = _pack_rows(y)


def rmsnorm_packed(x, g, bm=256):
    m, d = x.shape
    bm = _pick(m, bm)
    packed = jax.eval_shape(_pack_rows, jax.ShapeDtypeStruct((bm, d), F32))
    pw = packed.shape[1]
    return pl.pallas_call(
        _rmsnorm_pack_kernel,
        out_shape=(jax.ShapeDtypeStruct((m, d), BF16), jax.ShapeDtypeStruct((m, pw), packed.dtype)),
        grid=(m // bm,),
        in_specs=[pl.BlockSpec((bm, d), lambda i: (i, 0)),
                  pl.BlockSpec((1, d), lambda i: (0, 0))],
        out_specs=(pl.BlockSpec((bm, d), lambda i: (i, 0)),
                   pl.BlockSpec((bm, pw), lambda i: (i, 0))),
        compiler_params=_params(("parallel",)),
        name="rmsnorm_packed",
    )(x, g.reshape(1, d).astype(F32))


def _row_copy(src, src_row, dst, dst_row, sem):
    return pltpu.make_async_copy(src.at[pl.ds(src_row, 1)], dst.at[pl.ds(dst_row, 1)], sem)


def _dispatch_kernel(cnt_ref, pstart_ref, pend_ref, pos_hbm, x_hbm, o_hbm,
                     idx_smem, xbuf, zbuf, idx_sem, load_sem, row_sem):
    i = pl.program_id(0)
    n_steps = pl.num_programs(0)
    n_slots, toks = xbuf.shape[0], xbuf.shape[1]
    per = toks * TOP_K

    def idx_copy(step):
        return pltpu.make_async_copy(pos_hbm.at[step], idx_smem.at[step % 2], idx_sem.at[step % 2])

    def load(step):
        src = x_hbm.at[pl.ds(pl.multiple_of(step * toks, toks), toks)]
        return pltpu.make_async_copy(src, xbuf.at[step % n_slots], load_sem.at[step % n_slots])

    def wait_rows(step):
        pltpu.make_async_copy(o_hbm.at[pl.ds(0, per)], o_hbm.at[pl.ds(0, per)], row_sem.at[step % n_slots]).wait()

    @pl.when(i == 0)
    def _():
        zbuf[...] = jnp.zeros_like(zbuf)
        idx_copy(0).start()
        load(0).start()

    @pl.when(i >= 2)
    def _():
        wait_rows(i - 2)

    @pl.when(i + 1 < n_steps)
    def _():
        idx_copy(i + 1).start()
        load(i + 1).start()

    idx_copy(i).wait()
    load(i).wait()
    slot, islot = i % n_slots, i % 2

    def body(t, _):
        for k in range(TOP_K):
            _row_copy(xbuf.at[slot], t, o_hbm, idx_smem[islot, t * TOP_K + k], row_sem.at[slot]).start(priority=k % 2)
        return 0

    lax.fori_loop(0, toks, body, 0)

    @pl.when(i == n_steps - 1)
    def _():
        @pl.when(i >= 1)
        def _():
            wait_rows(i - 1)

        wait_rows(i)

        def expert_padding(e, _):
            first = pstart_ref[e] + cnt_ref[e]
            n_pad = pend_ref[e] - first

            def zbody(r, _):
                _row_copy(zbuf, 0, o_hbm, first + r, row_sem.at[0]).start()
                return 0

            def zwait(r, _):
                _row_copy(zbuf, 0, o_hbm, first, row_sem.at[0]).wait()
                return 0

            lax.fori_loop(0, n_pad, zbody, 0)
            lax.fori_loop(0, n_pad, zwait, 0)
            return 0

        lax.fori_loop(0, cnt_ref.shape[0], expert_padding, 0)

        blk_rows = zbuf.shape[0]
        n_exp = cnt_ref.shape[0]
        first_blk = pend_ref[n_exp - 1] // blk_rows
        n_tail = o_hbm.shape[0] // blk_rows - first_blk

        def tail_copy(b):
            dst = o_hbm.at[pl.ds(pl.multiple_of((first_blk + b) * blk_rows, blk_rows), blk_rows)]
            return pltpu.make_async_copy(zbuf, dst, row_sem.at[0])

        def tbody(b, _):
            tail_copy(b).start()
            return 0

        def twait(b, _):
            tail_copy(b).wait()
            return 0

        lax.fori_loop(0, n_tail, tbody, 0)
        lax.fori_loop(0, n_tail, twait, 0)


def dispatch_rows(x_rows, pos, counts, pad_start, pad_end, cap, blk_rows, chunk_tokens=512):
    n_tok, width = x_rows.shape
    toks = _pick(n_tok, chunk_tokens)
    steps = n_tok // toks
    n_slots = 3
    grid_spec = pltpu.PrefetchScalarGridSpec(
        num_scalar_prefetch=3,
        grid=(steps,),
        in_specs=[pl.BlockSpec(memory_space=pl.ANY)] * 2,
        out_specs=pl.BlockSpec(memory_space=pl.ANY),
        scratch_shapes=[pltpu.SMEM((2, toks * TOP_K), jnp.int32),
                        pltpu.VMEM((n_slots, toks, width), x_rows.dtype),
                        pltpu.VMEM((blk_rows, width), x_rows.dtype),
                        pltpu.SemaphoreType.DMA((2,)),
                        pltpu.SemaphoreType.DMA((n_slots,)),
                        pltpu.SemaphoreType.DMA((n_slots,))],
    )
    return pl.pallas_call(
        _dispatch_kernel,
        out_shape=jax.ShapeDtypeStruct((cap, width), x_rows.dtype),
        grid_spec=grid_spec,
        compiler_params=_params(("arbitrary",)),
        name="moe_dispatch",
    )(counts, pad_start, pad_end, pos.reshape(steps, toks * TOP_K), x_rows)


def _expert_kernel(bexp_ref, nused_ref, x_ref, wg_ref, wu_ref, wd_ref, o_ref):
    i = pl.program_id(0)

    @pl.when(i < nused_ref[0])
    def _():
        lo, hi = _unpack_rows(x_ref[...])
        x = jnp.concatenate([lo, hi], axis=1).astype(BF16)
        o_ref[...] = _pack_rows(_swiglu(x, wg_ref[...], wu_ref[...], wd_ref[...]))

    @pl.when(i >= nused_ref[0])
    def _():
        o_ref[...] = jnp.zeros_like(o_ref)


def routed_experts_sorted(x_sorted, block_expert, n_used, wg, wu, wd, rows):
    cap, width = x_sorted.shape
    nb = cap // rows
    d, ff = wg.shape[1], wg.shape[2]
    used = lambda i, be, nu: (jnp.maximum(jnp.minimum(i, nu[0] - 1), 0), 0)
    grid_spec = pltpu.PrefetchScalarGridSpec(
        num_scalar_prefetch=2,
        grid=(nb,),
        in_specs=[pl.BlockSpec((rows, width), used),
                  pl.BlockSpec((None, d, ff), lambda i, be, nu: (be[i], 0, 0)),
                  pl.BlockSpec((None, d, ff), lambda i, be, nu: (be[i], 0, 0)),
                  pl.BlockSpec((None, ff, d), lambda i, be, nu: (be[i], 0, 0))],
        out_specs=pl.BlockSpec((rows, width), lambda i, be, nu: (i, 0)),
    )
    return pl.pallas_call(
        _expert_kernel,
        out_shape=jax.ShapeDtypeStruct((cap, width), x_sorted.dtype),
        grid_spec=grid_spec,
        compiler_params=_params(("arbitrary",)),
        name="routed_experts",
    )(block_expert, n_used, x_sorted, wg, wu, wd)


def _combine_kernel(pos_hbm, y_hbm, base_ref, wt_ref, g_ref, h_ref, hn_ref, idx_smem, gbuf, idx_sem, row_sem):
    i = pl.program_id(0)
    n_steps = pl.num_programs(0)
    toks = gbuf.shape[2]

    def fetch(blk, slot):
        cp = pltpu.make_async_copy(pos_hbm.at[blk], idx_smem.at[slot], idx_sem)
        cp.start()
        cp.wait()

        def body(r, _):
            for k in range(TOP_K):
                _row_copy(y_hbm, idx_smem[slot, r * TOP_K + k], gbuf.at[slot, k], r,
                          row_sem.at[slot]).start(priority=k % 2)
            return 0

        lax.fori_loop(0, toks, body, 0)

    @pl.when(i == 0)
    def _():
        fetch(0, 0)

    @pl.when(i + 1 < n_steps)
    def _():
        fetch(i + 1, (i + 1) % 2)

    slot = i % 2
    for k in range(TOP_K):
        pltpu.make_async_copy(y_hbm.at[pl.ds(0, toks)], gbuf.at[slot, k], row_sem.at[slot]).wait()
    wt = wt_ref[...]
    acc_lo, acc_hi = None, None
    for k in range(TOP_K):
        lo, hi = _unpack_rows(gbuf[slot, k])
        wk = wt[:, k:k + 1]
        acc_lo = lo * wk if acc_lo is None else acc_lo + lo * wk
        acc_hi = hi * wk if acc_hi is None else acc_hi + hi * wk
    h = base_ref[...] + jnp.concatenate([acc_lo, acc_hi], axis=1)
    h_ref[...] = h
    ms = jnp.mean(h * h, axis=-1, keepdims=True)
    hn_ref[...] = (h * lax.rsqrt(ms + EPS) * g_ref[...]).astype(hn_ref.dtype)


def combine(pos, y_sorted, base, wts, gain, tokens=COMBINE_TOKENS):
    m, d = base.shape
    toks = _pick(m, tokens)
    steps, per = m // toks, toks * TOP_K
    pos = pos.reshape(steps, per)
    return pl.pallas_call(
        _combine_kernel,
        out_shape=(jax.ShapeDtypeStruct((m, d), F32), jax.ShapeDtypeStruct((m, d), BF16)),
        grid=(steps,),
        in_specs=[pl.BlockSpec(memory_space=pl.ANY),
                  pl.BlockSpec(memory_space=pl.ANY),
                  pl.BlockSpec((toks, d), lambda i: (i, 0)),
                  pl.BlockSpec((toks, TOP_K), lambda i: (i, 0)),
                  pl.BlockSpec((1, d), lambda i: (0, 0))],
        out_specs=(pl.BlockSpec((toks, d), lambda i: (i, 0)),
                   pl.BlockSpec((toks, d), lambda i: (i, 0))),
        scratch_shapes=[pltpu.SMEM((2, per), jnp.int32),
                        pltpu.VMEM((2, TOP_K, toks, y_sorted.shape[1]), y_sorted.dtype),
                        pltpu.SemaphoreType.DMA(()),
                        pltpu.SemaphoreType.DMA((2,))],
        compiler_params=_params(("arbitrary",)),
        name="moe_combine",
    )(pos, y_sorted, base, wts, gain.reshape(1, d).astype(F32))


def dispatch_tables(eidx, rank, counts, rows):
    n_tok = eidx.shape[0]
    n_exp = counts.shape[0]
    nb = n_tok * TOP_K // rows + n_exp
    pad_end = jnp.cumsum((counts + rows - 1) // rows * rows)
    pad_start = pad_end - (counts + rows - 1) // rows * rows
    experts = jnp.arange(n_exp, dtype=jnp.int32)
    pos = rank + jnp.sum(jnp.where(eidx[..., None] == experts, pad_start, 0), axis=-1)
    blocks = jnp.arange(nb, dtype=jnp.int32)
    block_expert = jnp.minimum(jnp.sum(pad_end[None, :] // rows <= blocks[:, None], axis=-1), n_exp - 1)
    n_used = pad_end[-1:] // rows
    i32 = lambda a: a.astype(jnp.int32)
    return i32(pos).reshape(-1), i32(block_expert), i32(n_used), i32(pad_start), i32(pad_end), nb * rows


def _ple_kernel(hn_ref, wg_ref, p_ref, wp_ref, h_ref, o_ref):
    gate = jax.nn.sigmoid(_dot(hn_ref[...], wg_ref[...]))
    o_ref[...] = h_ref[...] + gate * _dot(p_ref[...].astype(BF16), wp_ref[...])


def ple_gate(hn, w_gate, p, w_ple, h, bm=512, bn=1024):
    m, d = hn.shape
    n = w_gate.shape[1]
    pd = p.shape[1]
    bm, bn = _pick(m, bm), _pick(n, bn)
    return pl.pallas_call(
        _ple_kernel,
        out_shape=jax.ShapeDtypeStruct((m, n), F32),
        grid=(m // bm, n // bn),
        in_specs=[pl.BlockSpec((bm, d), lambda i, j: (i, 0)),
                  pl.BlockSpec((d, bn), lambda i, j: (0, j)),
                  pl.BlockSpec((bm, pd), lambda i, j: (i, 0)),
                  pl.BlockSpec((pd, bn), lambda i, j: (0, j)),
                  pl.BlockSpec((bm, bn), lambda i, j: (i, j))],
        out_specs=pl.BlockSpec((bm, bn), lambda i, j: (i, j)),
        compiler_params=_params(("parallel", "parallel")),
        name="ple_gate",
    )(hn, w_gate, p, w_ple, h)


def _layer(h, p_i, prm, batch):
    (mix_norm, w_in, a_re, a_im, log_step, b_re, b_im, c_re, c_im, ssm_d, w_glu_a, w_glu_b, q_norm, k_norm,
     sinks, w_bs, w_ba, w_out, moe_norm, w_router, router_bias, we_gate, we_up, we_down, ws_gate, ws_up,
     ws_down, ple_norm, w_ple, w_ple_gate) = prm
    n_tok, d = h.shape
    seq = n_tok // batch
    ssm_w = w_glu_a.shape[0]
    n_q = sinks.shape[0]
    attn_w = n_q * HEAD_DIM
    kv_w = attn_w // Q_PER_KV
    n_exp = w_router.shape[1]
    c0, c1 = ssm_w, ssm_w + attn_w + 2 * kv_w
    bf = lambda w: w.astype(BF16)

    hn = rmsnorm(h, mix_norm, BF16)
    u = matmul(hn, bf(w_in[:, :c0]), F32, name="proj_u")
    qkv = matmul(hn, bf(w_in[:, c0:c1]), BF16, bn=(c1 - c0) // 2, name="proj_qkv")
    gates = matmul(hn, bf(w_in[:, c1:]), BF16, name="proj_gates")
    y_pre = s5_scan(u, batch, a_re, a_im, log_step, b_re, b_im, c_re, c_im, ssm_d)
    y_ssm = glu(y_pre, bf(w_glu_a), bf(w_glu_b))
    y_attn = swa_attention(qkv, batch, n_q, q_norm, k_norm, sinks)
    merged = branch_merge(y_ssm, bf(w_bs), y_attn, bf(w_ba), gates)
    h = matmul_residual(merged, bf(w_out), h)

    hm, hm_rows = rmsnorm_packed(h, moe_norm)
    eidx, ew, rank, counts = router(hm, bf(w_router), router_bias)
    counts = counts.reshape(n_exp)
    pos, block_expert, n_used, pad_start, pad_end, cap = dispatch_tables(eidx, rank, counts, MOE_ROWS)
    x_sorted = dispatch_rows(hm_rows, pos, counts, pad_start, pad_end, cap, MOE_ROWS)
    base = shared_expert_residual(hm, bf(ws_gate), bf(ws_up), bf(ws_down), h)
    y_sorted = routed_experts_sorted(x_sorted, block_expert, n_used, bf(we_gate), bf(we_up), bf(we_down), MOE_ROWS)
    h, hn3 = combine(pos, y_sorted, base, ew, ple_norm)

    return ple_gate(hn3, bf(w_ple_gate), p_i, bf(w_ple), h)


def kernel(x, p, mix_norm, w_in, ssm_a_re, ssm_a_im, ssm_log_step, ssm_b_re, ssm_b_im, ssm_c_re, ssm_c_im, ssm_d, w_glu_a, w_glu_b, q_norm, k_norm, attn_sinks, w_branch_ssm, w_branch_attn, w_out, moe_norm, w_router, router_bias, we_gate, we_up, we_down, ws_gate, ws_up, ws_down, ple_norm, w_ple, w_ple_gate):
    bsz, seq, d = x.shape
    layer_params = (mix_norm, w_in, ssm_a_re, ssm_a_im, ssm_log_step, ssm_b_re, ssm_b_im, ssm_c_re, ssm_c_im,
                    ssm_d, w_glu_a, w_glu_b, q_norm, k_norm, attn_sinks, w_branch_ssm, w_branch_attn, w_out,
                    moe_norm, w_router, router_bias, we_gate, we_up, we_down, ws_gate, ws_up, ws_down,
                    ple_norm, w_ple, w_ple_gate)
    h = x.reshape(bsz * seq, d)
    for i in range(mix_norm.shape[0]):
        h = _layer(h, p[i].reshape(bsz * seq, -1), tuple(w[i] for w in layer_params), bsz)
    return h.reshape(bsz, seq, d)
```

```python
import functools
import math

import jax
import jax.numpy as jnp
from jax import lax
from jax.experimental import pallas as pl
from jax.experimental.pallas import tpu as pltpu

SSM_GROUP = 16
SSM_STATE = 64
S5_CHUNK = 8
S5_TILE_LANES = 128
HEAD_DIM = 64
Q_PER_KV = 8
ATTN_BLOCK = 128
ALIBI_MAX_BIAS = 8.0
N_EXPERT_GROUPS = 8
TOPK_GROUPS = 4
TOP_K = 8
ROUTED_SCALE = 2.5
EPS = 1e-6
MOE_ROWS = 256
COMBINE_TOKENS = 64
V7X_VMEM_LIMIT = 56 * 1024 * 1024

BF16 = jnp.bfloat16
F32 = jnp.float32


def _dot(a, b):
    return jnp.dot(a, b, preferred_element_type=F32)


def _params(sem, vmem=V7X_VMEM_LIMIT):
    return pltpu.CompilerParams(dimension_semantics=sem, vmem_limit_bytes=vmem)


def _pick(n, pref):
    b = min(n, pref)
    while n % b:
        b //= 2
    return b


def _rmsnorm_kernel(x_ref, g_ref, o_ref):
    x = x_ref[...]
    ms = jnp.mean(x * x, axis=-1, keepdims=True)
    o_ref[...] = (x * lax.rsqrt(ms + EPS) * g_ref[...]).astype(o_ref.dtype)


def rmsnorm(x, g, out_dtype, bm=256):
    m, d = x.shape
    bm = _pick(m, bm)
    return pl.pallas_call(
        _rmsnorm_kernel,
        out_shape=jax.ShapeDtypeStruct((m, d), out_dtype),
        grid=(m // bm,),
        in_specs=[pl.BlockSpec((bm, d), lambda i: (i, 0)),
                  pl.BlockSpec((1, d), lambda i: (0, 0))],
        out_specs=pl.BlockSpec((bm, d), lambda i: (i, 0)),
        compiler_params=_params(("parallel",)),
        name="rmsnorm",
    )(x, g.reshape(1, d).astype(F32))


def _mm_kernel(x_ref, w_ref, o_ref):
    o_ref[...] = _dot(x_ref[...], w_ref[...]).astype(o_ref.dtype)


def matmul(x, w, out_dtype, bm=1024, bn=1024, name="matmul"):
    m, k = x.shape
    n = w.shape[1]
    bm, bn = _pick(m, bm), _pick(n, bn)
    return pl.pallas_call(
        _mm_kernel,
        out_shape=jax.ShapeDtypeStruct((m, n), out_dtype),
        grid=(m // bm, n // bn),
        in_specs=[pl.BlockSpec((bm, k), lambda i, j: (i, 0)),
                  pl.BlockSpec((k, bn), lambda i, j: (0, j))],
        out_specs=pl.BlockSpec((bm, bn), lambda i, j: (i, j)),
        compiler_params=_params(("parallel", "parallel")),
        name=name,
    )(x, w)


def _spread_groups(compact, n_out, src_of, group_of, rows_per_group):
    n_in = compact.shape[1]
    k = lax.broadcasted_iota(jnp.int32, (n_in, n_out), 0)
    q = lax.broadcasted_iota(jnp.int32, (n_in, n_out), 1)
    spread = _dot(compact, jnp.where(src_of(q) == k, 1.0, 0.0).astype(BF16))
    r = lax.broadcasted_iota(jnp.int32, spread.shape, 0)
    q = lax.broadcasted_iota(jnp.int32, spread.shape, 1)
    return jnp.where(group_of(q) == r // rows_per_group, spread, 0.0).astype(BF16)


def _s5_chunk_matrices(kc_ref, fc_ref, ec_ref, w_s, e_s):
    tc, lanes, cg, ns = S5_CHUNK, S5_TILE_LANES, SSM_GROUP, SSM_STATE
    half = e_s.shape[0] // 2
    w_s[...] = jnp.zeros_like(w_s)
    for tau in range(tc):
        blk = _spread_groups(kc_ref[tau], lanes, lambda q: q % cg, lambda q: q // cg, cg)
        for i in range(tc - tau):
            w_s[i * lanes:(i + 1) * lanes, (i + tau) * lanes:(i + tau + 1) * lanes] = blk
    for i in range(tc):
        f = fc_ref[i]
        rows = slice(i * lanes, (i + 1) * lanes)
        w_s[rows, tc * lanes:tc * lanes + half] = _spread_groups(f, half, lambda q: q % ns, lambda q: q // ns, cg)
        w_s[rows, tc * lanes + half:] = _spread_groups(f, half, lambda q: q % ns + ns, lambda q: q // ns, cg)
    for part in range(2):
        e_s[part * half:(part + 1) * half, :] = _spread_groups(
            ec_ref[part], tc * lanes, lambda q: (q // lanes) * cg + q % cg, lambda q: (q // cg) % (lanes // cg), ns)


def _s5_kernel(u_ref, kc_ref, fc_ref, ec_ref, pw_ref, d_ref, o_ref, fu_ref, w_ref, e_ref):
    @pl.when(pl.program_id(1) == 0)
    def _():
        _s5_chunk_matrices(kc_ref, fc_ref, ec_ref, w_ref, e_ref)

    tc = S5_CHUNK
    n_chunks = u_ref.shape[0] // tc
    lanes = u_ref.shape[1]
    half = fu_ref.shape[1] // 2
    x = jnp.concatenate([u_ref[pl.ds(i, n_chunks, stride=tc), :] for i in range(tc)], axis=1)
    z = _dot(x.astype(BF16), w_ref[...])
    fu_ref[...] = z[:, tc * lanes:]
    row = lax.broadcasted_iota(jnp.int32, (8, half), 0)
    pw = [pw_ref[k] for k in range(10)]

    def cmul(ar, ai, xr, xi):
        return ar * xr - ai * xi, ar * xi + ai * xr

    def body(t, carry):
        c_re, c_im = carry
        r0 = pl.multiple_of(t * 8, 8)
        x_re = fu_ref[pl.ds(r0, 8), pl.ds(0, half)]
        x_im = fu_ref[pl.ds(r0, 8), pl.ds(half, half)]
        for k, sh in enumerate((1, 2, 4)):
            s_re = jnp.where(row >= sh, pltpu.roll(x_re, sh, 0), 0.0)
            s_im = jnp.where(row >= sh, pltpu.roll(x_im, sh, 0), 0.0)
            m_re, m_im = cmul(pw[2 * k], pw[2 * k + 1], s_re, s_im)
            x_re, x_im = x_re + m_re, x_im + m_im
        p_re, p_im = cmul(pw[6], pw[7], c_re, c_im)
        fu_ref[pl.ds(r0, 8), pl.ds(0, half)] = jnp.where(row >= 1, pltpu.roll(x_re, 1, 0), 0.0) + p_re
        fu_ref[pl.ds(r0, 8), pl.ds(half, half)] = jnp.where(row >= 1, pltpu.roll(x_im, 1, 0), 0.0) + p_im
        f_re, f_im = cmul(pw[8], pw[9], c_re, c_im)
        f_re, f_im = f_re + x_re, f_im + x_im
        return jnp.broadcast_to(f_re[7:8], (8, half)), jnp.broadcast_to(f_im[7:8], (8, half))

    zero = jnp.zeros((8, half), F32)
    lax.fori_loop(0, n_chunks // 8, body, (zero, zero))
    y = z[:, :tc * lanes] + _dot(fu_ref[...].astype(BF16), e_ref[...]) + d_ref[...] * x
    y = jax.nn.gelu(y)
    for j in range(tc):
        o_ref[pl.ds(j, n_chunks, stride=tc), :] = y[:, j * lanes:(j + 1) * lanes]


def s5_scan(u, batch, a_re, a_im, log_step, b_re, b_im, c_re, c_im, d_skip):
    rows, width = u.shape
    seq = rows // batch
    tc, lanes = S5_CHUNK, S5_TILE_LANES
    gpt = lanes // SSM_GROUP
    n_tiles = width // lanes
    half = gpt * SSM_STATE
    assert seq % (8 * tc) == 0 and width % lanes == 0
    lam = lax.complex(a_re.astype(F32), a_im.astype(F32))
    lam_dt = lam * jnp.exp(log_step.astype(F32))[:, None]
    b_bar = ((jnp.exp(lam_dt) - 1.0) / lam)[..., None] * lax.complex(b_re.astype(F32), b_im.astype(F32))
    c_mat = lax.complex(c_re.astype(F32), c_im.astype(F32))
    lp = jnp.exp(lam_dt[None] * jnp.arange(tc + 1, dtype=F32)[:, None, None])
    kern = jnp.real(jnp.einsum('gcn,tgn,gnd->tgcd', c_mat, lp[:tc], b_bar))
    kc = kern.transpose(1, 0, 3, 2).reshape(n_tiles, gpt, tc, SSM_GROUP, SSM_GROUP)
    kc = kc.transpose(0, 2, 1, 3, 4).reshape(n_tiles, tc, lanes, SSM_GROUP).astype(BF16)
    f_c = lp[tc - 1 - jnp.arange(tc)][..., None] * b_bar[None]
    f_t = lambda a: (a.transpose(1, 0, 3, 2).reshape(n_tiles, gpt, tc, SSM_GROUP, SSM_STATE)
                     .transpose(0, 2, 1, 3, 4).reshape(n_tiles, tc, lanes, SSM_STATE))
    fc = jnp.concatenate([f_t(jnp.real(f_c)), f_t(jnp.imag(f_c))], axis=-1).astype(BF16)
    e_c = c_mat[None] * lp[1:, :, None, :]
    e_t = lambda a: a.transpose(1, 3, 0, 2).reshape(n_tiles, half, tc * SSM_GROUP)
    ec = jnp.stack([e_t(jnp.real(e_c)), e_t(-jnp.imag(e_c))], axis=1).astype(BF16)
    r8 = jnp.arange(8, dtype=F32)[:, None, None]
    a_pow = lambda e: jnp.exp(lam_dt[None] * (tc * e))
    plist = [a_pow(jnp.full_like(r8, e)) for e in (1.0, 2.0, 4.0)] + [a_pow(r8), a_pow(r8 + 1.0)]
    pws = jnp.stack([f(p) for p in plist for f in (jnp.real, jnp.imag)])
    pws = pws.reshape(10, 8, n_tiles, half).transpose(2, 0, 1, 3)
    dsk = jnp.tile(d_skip.astype(F32).reshape(n_tiles, 1, lanes), (1, 1, tc))
    return pl.pallas_call(
        _s5_kernel,
        out_shape=jax.ShapeDtypeStruct((rows, width), F32),
        grid=(n_tiles, batch),
        in_specs=[pl.BlockSpec((seq, lanes), lambda s, b: (b, s)),
                  pl.BlockSpec((None, tc, lanes, SSM_GROUP), lambda s, b: (s, 0, 0, 0)),
                  pl.BlockSpec((None, tc, lanes, 2 * SSM_STATE), lambda s, b: (s, 0, 0, 0)),
                  pl.BlockSpec((None, 2, half, tc * SSM_GROUP), lambda s, b: (s, 0, 0, 0)),
                  pl.BlockSpec((None, 10, 8, half), lambda s, b: (s, 0, 0, 0)),
                  pl.BlockSpec((None, 1, tc * lanes), lambda s, b: (s, 0, 0))],
        out_specs=pl.BlockSpec((seq, lanes), lambda s, b: (b, s)),
        scratch_shapes=[pltpu.VMEM((seq // tc, 2 * half), F32),
                        pltpu.VMEM((tc * lanes, tc * lanes + 2 * half), BF16),
                        pltpu.VMEM((2 * half, tc * lanes), BF16)],
        compiler_params=_params(("parallel", "arbitrary")),
        name="s5_scan",
    )(u, kc, fc, ec, pws, dsk)


def _glu_kernel(x_ref, wa_ref, wb_ref, o_ref):
    x = x_ref[...].astype(BF16)
    o_ref[...] = (_dot(x, wa_ref[...]) * jax.nn.sigmoid(_dot(x, wb_ref[...]))).astype(o_ref.dtype)


def glu(x, wa, wb, bm=1024, bn=1024):
    m, k = x.shape
    n = wa.shape[1]
    bm, bn = _pick(m, bm), _pick(n, bn)
    return pl.pallas_call(
        _glu_kernel,
        out_shape=jax.ShapeDtypeStruct((m, n), BF16),
        grid=(m // bm, n // bn),
        in_specs=[pl.BlockSpec((bm, k), lambda i, j: (i, 0)),
                  pl.BlockSpec((k, bn), lambda i, j: (0, j)),
                  pl.BlockSpec((k, bn), lambda i, j: (0, j))],
        out_specs=pl.BlockSpec((bm, bn), lambda i, j: (i, j)),
        compiler_params=_params(("parallel", "parallel")),
        name="ssm_glu",
    )(x, wa, wb)


def _attn_kernel(sink_ref, q_ref, kc_ref, kp_ref, vc_ref, vp_ref, qg_ref, kg_ref, o_ref, *, n_kv):
    blk, hd = ATTN_BLOCK, HEAD_DIM
    first = pl.program_id(1) == 0
    n_q = n_kv * Q_PER_KV
    kj = lax.broadcasted_iota(jnp.int32, (2 * blk, blk), 0)
    qi = lax.broadcasted_iota(jnp.int32, (2 * blk, blk), 1)
    dist = qi + blk - kj
    kmin = jnp.where(first, blk, 0)
    valid = (dist >= 0) & (dist < blk) & (kj >= kmin)
    dist_f = dist.astype(F32)

    def head_norm_t(t, heads, gain):
        t3 = t.reshape(heads, hd, t.shape[1])
        ms = jnp.mean(t3 * t3, axis=1, keepdims=True)
        return t3 * lax.rsqrt(ms + EPS) * gain[None]

    qn = (head_norm_t(q_ref[...].astype(F32).T, n_q, qg_ref[...]) * (hd ** -0.5)).astype(BF16)
    kk = jnp.concatenate([kp_ref[...], kc_ref[...]], axis=0).astype(F32)
    kn = head_norm_t(kk.T, n_kv, kg_ref[...]).reshape(n_kv * hd, 2 * blk).T.astype(BF16)
    vt = jnp.concatenate([vp_ref[...], vc_ref[...]], axis=0).astype(F32).T.astype(BF16)

    for h in range(n_kv):
        qt = jnp.concatenate([qn[h * Q_PER_KV + g] for g in range(Q_PER_KV)], axis=1)
        st = _dot(kn[:, h * hd:(h + 1) * hd], qt)
        probs = []
        for g in range(Q_PER_KV):
            head = h * Q_PER_KV + g
            slope = 2.0 ** (-ALIBI_MAX_BIAS * (head + 1) / n_q)
            s = jnp.where(valid, st[:, g * blk:(g + 1) * blk] - slope * dist_f, -jnp.inf)
            sink = sink_ref[head]
            m = jnp.maximum(jnp.max(s, axis=0, keepdims=True), sink)
            p = jnp.exp(s - m)
            denom = jnp.sum(p, axis=0, keepdims=True) + jnp.exp(sink - m)
            probs.append((p * (1.0 / denom)).astype(BF16))
        ot = _dot(vt[h * hd:(h + 1) * hd, :], jnp.concatenate(probs, axis=1))
        ot = jnp.concatenate([ot[:, g * blk:(g + 1) * blk] for g in range(Q_PER_KV)], axis=0)
        o_ref[:, h * Q_PER_KV * hd:(h + 1) * Q_PER_KV * hd] = ot.T.astype(o_ref.dtype)


def swa_attention(qkv, batch, n_q, q_gain, k_gain, sinks):
    rows = qkv.shape[0]
    n_kv = n_q // Q_PER_KV
    qw, kw = n_q * HEAD_DIM, n_kv * HEAD_DIM
    blk = ATTN_BLOCK
    nb = rows // batch // blk
    kcol = qw // kw
    cur = lambda b, j: b * nb + j
    prev = lambda b, j: b * nb + jnp.maximum(j - 1, 0)
    return pl.pallas_call(
        functools.partial(_attn_kernel, n_kv=n_kv),
        out_shape=jax.ShapeDtypeStruct((rows, qw), BF16),
        grid=(batch, nb),
        in_specs=[pl.BlockSpec(memory_space=pltpu.SMEM),
                  pl.BlockSpec((blk, qw), lambda b, j: (cur(b, j), 0)),
                  pl.BlockSpec((blk, kw), lambda b, j: (cur(b, j), kcol)),
                  pl.BlockSpec((blk, kw), lambda b, j: (prev(b, j), kcol)),
                  pl.BlockSpec((blk, kw), lambda b, j: (cur(b, j), kcol + 1)),
                  pl.BlockSpec((blk, kw), lambda b, j: (prev(b, j), kcol + 1)),
                  pl.BlockSpec((HEAD_DIM, blk), lambda b, j: (0, 0)),
                  pl.BlockSpec((HEAD_DIM, 2 * blk), lambda b, j: (0, 0))],
        out_specs=pl.BlockSpec((blk, qw), lambda b, j: (cur(b, j), 0)),
        compiler_params=_params(("parallel", "arbitrary")),
        name="swa_attention",
    )(sinks.astype(F32), qkv, qkv, qkv, qkv, qkv,
      jnp.broadcast_to(q_gain.astype(F32)[:, None], (HEAD_DIM, blk)),
      jnp.broadcast_to(k_gain.astype(F32)[:, None], (HEAD_DIM, 2 * blk)))


def _merge_kernel(ys_ref, ws_ref, ya_ref, wa_ref, gs_ref, ga_ref, o_ref):
    s = jax.nn.sigmoid(gs_ref[...].astype(F32)) * _dot(ys_ref[...], ws_ref[...])
    a = jax.nn.sigmoid(ga_ref[...].astype(F32)) * _dot(ya_ref[...], wa_ref[...])
    o_ref[...] = (s + a).astype(o_ref.dtype)


def branch_merge(y_ssm, w_ssm, y_attn, w_attn, gates, bm=1024, bn=1024):
    m, ks = y_ssm.shape
    ka = y_attn.shape[1]
    n = w_ssm.shape[1]
    bm, bn = _pick(m, bm), _pick(n, bn)
    nj = n // bn
    return pl.pallas_call(
        _merge_kernel,
        out_shape=jax.ShapeDtypeStruct((m, n), BF16),
        grid=(m // bm, nj),
        in_specs=[pl.BlockSpec((bm, ks), lambda i, j: (i, 0)),
                  pl.BlockSpec((ks, bn), lambda i, j: (0, j)),
                  pl.BlockSpec((bm, ka), lambda i, j: (i, 0)),
                  pl.BlockSpec((ka, bn), lambda i, j: (0, j)),
                  pl.BlockSpec((bm, bn), lambda i, j: (i, j)),
                  pl.BlockSpec((bm, bn), lambda i, j: (i, nj + j))],
        out_specs=pl.BlockSpec((bm, bn), lambda i, j: (i, j)),
        compiler_params=_params(("parallel", "parallel")),
        name="branch_merge",
    )(y_ssm, w_ssm, y_attn, w_attn, gates, gates)


def _mm_res_kernel(x_ref, w_ref, r_ref, o_ref):
    o_ref[...] = r_ref[...] + _dot(x_ref[...], w_ref[...])


def matmul_residual(x, w, res, bm=1024, bn=1024):
    m, k = x.shape
    n = w.shape[1]
    bm, bn = _pick(m, bm), _pick(n, bn)
    return pl.pallas_call(
        _mm_res_kernel,
        out_shape=jax.ShapeDtypeStruct((m, n), F32),
        grid=(m // bm, n // bn),
        in_specs=[pl.BlockSpec((bm, k), lambda i, j: (i, 0)),
                  pl.BlockSpec((k, bn), lambda i, j: (0, j)),
                  pl.BlockSpec((bm, bn), lambda i, j: (i, j))],
        out_specs=pl.BlockSpec((bm, bn), lambda i, j: (i, j)),
        compiler_params=_params(("parallel", "parallel")),
        name="out_proj_residual",
    )(x, w, res)


def _router_kernel(x_ref, w_ref, b_ref, idx_ref, wt_ref, rank_ref, cnt_ref, seen_ref, *, n_exp):
    @pl.when(pl.program_id(0) == 0)
    def _():
        seen_ref[...] = jnp.zeros_like(seen_ref)

    s = jax.nn.sigmoid(_dot(x_ref[...], w_ref[...]))
    sel = s + b_ref[...]
    rows = s.shape[0]
    lane = lax.broadcasted_iota(jnp.int32, (rows, n_exp), 1)
    per_group = n_exp // N_EXPERT_GROUPS
    grp = lane // per_group
    neg = -jnp.inf
    gscore = jnp.zeros_like(sel)
    for g in range(N_EXPERT_GROUPS):
        in_g = grp == g
        v = jnp.where(in_g, sel, neg)
        m1 = jnp.max(v, axis=-1, keepdims=True)
        i1 = jnp.min(jnp.where(v == m1, lane, n_exp), axis=-1, keepdims=True)
        m2 = jnp.max(jnp.where(lane == i1, neg, v), axis=-1, keepdims=True)
        gscore = jnp.where(in_g, m1 + m2, gscore)
    cand = jnp.full_like(sel, neg)
    remaining = gscore
    for _ in range(TOPK_GROUPS):
        gm = jnp.max(remaining, axis=-1, keepdims=True)
        gi = jnp.min(jnp.where(remaining == gm, grp, N_EXPERT_GROUPS), axis=-1, keepdims=True)
        hit = grp == gi
        cand = jnp.where(hit, sel, cand)
        remaining = jnp.where(hit, neg, remaining)
    slot = lax.broadcasted_iota(jnp.int32, (rows, TOP_K), 1)
    idx = jnp.zeros((rows, TOP_K), jnp.int32)
    wts = jnp.zeros((rows, TOP_K), F32)
    total = jnp.zeros((rows, 1), F32)
    picked = jnp.zeros_like(sel)
    hits = []
    for k in range(TOP_K):
        mx = jnp.max(cand, axis=-1, keepdims=True)
        ei = jnp.min(jnp.where(cand == mx, lane, n_exp), axis=-1, keepdims=True)
        hit = lane == ei
        wk = jnp.sum(jnp.where(hit, s, 0.0), axis=-1, keepdims=True)
        idx = jnp.where(slot == k, ei, idx)
        wts = jnp.where(slot == k, wk, wts)
        total = total + wk
        cand = jnp.where(hit, neg, cand)
        picked = jnp.where(hit, 1.0, picked)
        hits.append(hit)
    idx_ref[...] = idx
    wt_ref[...] = wts / total * ROUTED_SCALE
    r_i = lax.broadcasted_iota(jnp.int32, (rows, rows), 0)
    c_i = lax.broadcasted_iota(jnp.int32, (rows, rows), 1)
    lower = jnp.where(c_i < r_i, 1.0, 0.0).astype(BF16)
    before = _dot(lower, picked.astype(BF16)) + seen_ref[...]
    rank = jnp.zeros((rows, TOP_K), F32)
    for k in range(TOP_K):
        rank = jnp.where(slot == k, jnp.sum(jnp.where(hits[k], before, 0.0), axis=-1, keepdims=True), rank)
    rank_ref[...] = rank.astype(jnp.int32)
    seen = seen_ref[...] + jnp.sum(picked, axis=0, keepdims=True)
    seen_ref[...] = seen
    cnt_ref[...] = seen.astype(jnp.int32)


def router(hm, w_router, bias, bm=512):
    m, d = hm.shape
    n_exp = w_router.shape[1]
    bm = _pick(m, bm)
    tk = lambda dt: jax.ShapeDtypeStruct((m, TOP_K), dt)
    tk_spec = pl.BlockSpec((bm, TOP_K), lambda i: (i, 0))
    return pl.pallas_call(
        functools.partial(_router_kernel, n_exp=n_exp),
        out_shape=(tk(jnp.int32), tk(F32), tk(jnp.int32), jax.ShapeDtypeStruct((1, n_exp), jnp.int32)),
        grid=(m // bm,),
        in_specs=[pl.BlockSpec((bm, d), lambda i: (i, 0)),
                  pl.BlockSpec((d, n_exp), lambda i: (0, 0)),
                  pl.BlockSpec((1, n_exp), lambda i: (0, 0))],
        out_specs=(tk_spec, tk_spec, tk_spec, pl.BlockSpec((1, n_exp), lambda i: (0, 0))),
        scratch_shapes=[pltpu.VMEM((1, n_exp), F32)],
        compiler_params=_params(("arbitrary",)),
        name="router",
    )(hm, w_router, bias.reshape(1, n_exp).astype(F32))


def _swiglu(x, wg, wu, wd):
    act = (jax.nn.silu(_dot(x, wg)) * _dot(x, wu)).astype(BF16)
    return _dot(act, wd)


def _shared_kernel(x_ref, wg_ref, wu_ref, wd_ref, r_ref, o_ref):
    o_ref[...] = r_ref[...] + _swiglu(x_ref[...], wg_ref[...], wu_ref[...], wd_ref[...])


def shared_expert_residual(hm, wg, wu, wd, res, bm=256):
    m, d = hm.shape
    ff = wg.shape[1]
    bm = _pick(m, bm)
    return pl.pallas_call(
        _shared_kernel,
        out_shape=jax.ShapeDtypeStruct((m, d), F32),
        grid=(m // bm,),
        in_specs=[pl.BlockSpec((bm, d), lambda i: (i, 0)),
                  pl.BlockSpec((d, ff), lambda i: (0, 0)),
                  pl.BlockSpec((d, ff), lambda i: (0, 0)),
                  pl.BlockSpec((ff, d), lambda i: (0, 0)),
                  pl.BlockSpec((bm, d), lambda i: (i, 0))],
        out_specs=pl.BlockSpec((bm, d), lambda i: (i, 0)),
        compiler_params=_params(("parallel",)),
        name="shared_expert",
    )(hm, wg, wu, wd, res)


def _pack_rows(v):
    c = v.shape[1] // 2
    lo = lax.bitcast_convert_type(v[:, :c].astype(BF16).astype(F32), jnp.uint32)
    hi = lax.bitcast_convert_type(v[:, c:].astype(BF16).astype(F32), jnp.uint32)
    return hi | (lo >> 16)


def _unpack_rows(w):
    lo = lax.bitcast_convert_type(w << 16, F32)
    hi = lax.bitcast_convert_type(w & jnp.uint32(0xFFFF0000), F32)
    return lo, hi


def _rmsnorm_pack_kernel(x_ref, g_ref, o_ref, p_ref):
    x = x_ref[...]
    ms = jnp.mean(x * x, axis=-1, keepdims=True)
    y = x * lax.rsqrt(ms + EPS) * g_ref[...]
    o_ref[...] = y.astype(o_ref.dtype)
    p_ref[...] = _pack_rows(y)


def rmsnorm_packed(x, g, bm=256):
    m, d = x.shape
    bm = _pick(m, bm)
    packed = jax.eval_shape(_pack_rows, jax.ShapeDtypeStruct((bm, d), F32))
    pw = packed.shape[1]
    return pl.pallas_call(
        _rmsnorm_pack_kernel,
        out_shape=(jax.ShapeDtypeStruct((m, d), BF16), jax.ShapeDtypeStruct((m, pw), packed.dtype)),
        grid=(m // bm,),
        in_specs=[pl.BlockSpec((bm, d), lambda i: (i, 0)),
                  pl.BlockSpec((1, d), lambda i: (0, 0))],
        out_specs=(pl.BlockSpec((bm, d), lambda i: (i, 0)),
                   pl.BlockSpec((bm, pw), lambda i: (i, 0))),
        compiler_params=_params(("parallel",)),
        name="rmsnorm_packed",
    )(x, g.reshape(1, d).astype(F32))


def _row_copy(src, src_row, dst, dst_row, sem):
    return pltpu.make_async_copy(src.at[pl.ds(src_row, 1)], dst.at[pl.ds(dst_row, 1)], sem)


def _dispatch_kernel(cnt_ref, pstart_ref, pend_ref, pos_hbm, x_hbm, o_hbm,
                     idx_smem, xbuf, zbuf, idx_sem, load_sem, row_sem):
    i = pl.program_id(0)
    n_steps = pl.num_programs(0)
    n_slots, toks = xbuf.shape[0], xbuf.shape[1]
    per = toks * TOP_K

    def idx_copy(step):
        return pltpu.make_async_copy(pos_hbm.at[step], idx_smem.at[step % 2], idx_sem.at[step % 2])

    def load(step):
        src = x_hbm.at[pl.ds(pl.multiple_of(step * toks, toks), toks)]
        return pltpu.make_async_copy(src, xbuf.at[step % n_slots], load_sem.at[step % n_slots])

    def wait_rows(step):
        pltpu.make_async_copy(o_hbm.at[pl.ds(0, per)], o_hbm.at[pl.ds(0, per)], row_sem.at[step % n_slots]).wait()

    @pl.when(i == 0)
    def _():
        zbuf[...] = jnp.zeros_like(zbuf)
        idx_copy(0).start()
        load(0).start()

    @pl.when(i >= 2)
    def _():
        wait_rows(i - 2)

    @pl.when(i + 1 < n_steps)
    def _():
        idx_copy(i + 1).start()
        load(i + 1).start()

    idx_copy(i).wait()
    load(i).wait()
    slot, islot = i % n_slots, i % 2

    def body(g, _):
        t0 = pl.multiple_of(g * 8, 8)
        for tt in range(8):
            for k in range(TOP_K):
                dst = idx_smem[islot, (t0 + tt) * TOP_K + k]
                _row_copy(xbuf.at[slot], t0 + tt, o_hbm, dst, row_sem.at[slot]).start()
        return 0

    lax.fori_loop(0, toks // 8, body, 0)

    @pl.when(i == n_steps - 1)
    def _():
        @pl.when(i >= 1)
        def _():
            wait_rows(i - 1)

        wait_rows(i)

        def expert_padding(e, _):
            first = pstart_ref[e] + cnt_ref[e]
            n_pad = pend_ref[e] - first

            def zbody(r, _):
                _row_copy(zbuf, 0, o_hbm, first + r, row_sem.at[0]).start()
                return 0

            def zwait(r, _):
                _row_copy(zbuf, 0, o_hbm, first, row_sem.at[0]).wait()
                return 0

            lax.fori_loop(0, n_pad, zbody, 0)
            lax.fori_loop(0, n_pad, zwait, 0)
            return 0

        lax.fori_loop(0, cnt_ref.shape[0], expert_padding, 0)

        blk_rows = zbuf.shape[0]
        n_exp = cnt_ref.shape[0]
        first_blk = pend_ref[n_exp - 1] // blk_rows
        n_tail = o_hbm.shape[0] // blk_rows - first_blk

        def tail_copy(b):
            dst = o_hbm.at[pl.ds(pl.multiple_of((first_blk + b) * blk_rows, blk_rows), blk_rows)]
            return pltpu.make_async_copy(zbuf, dst, row_sem.at[0])

        def tbody(b, _):
            tail_copy(b).start()
            return 0

        def twait(b, _):
            tail_copy(b).wait()
            return 0

        lax.fori_loop(0, n_tail, tbody, 0)
        lax.fori_loop(0, n_tail, twait, 0)


def dispatch_rows(x_rows, pos, counts, pad_start, pad_end, cap, blk_rows, chunk_tokens=512):
    n_tok, width = x_rows.shape
    toks = _pick(n_tok, chunk_tokens)
    steps = n_tok // toks
    n_slots = 3
    grid_spec = pltpu.PrefetchScalarGridSpec(
        num_scalar_prefetch=3,
        grid=(steps,),
        in_specs=[pl.BlockSpec(memory_space=pl.ANY)] * 2,
        out_specs=pl.BlockSpec(memory_space=pl.ANY),
        scratch_shapes=[pltpu.SMEM((2, toks * TOP_K), jnp.int32),
                        pltpu.VMEM((n_slots, toks, width), x_rows.dtype),
                        pltpu.VMEM((blk_rows, width), x_rows.dtype),
                        pltpu.SemaphoreType.DMA((2,)),
                        pltpu.SemaphoreType.DMA((n_slots,)),
                        pltpu.SemaphoreType.DMA((n_slots,))],
    )
    return pl.pallas_call(
        _dispatch_kernel,
        out_shape=jax.ShapeDtypeStruct((cap, width), x_rows.dtype),
        grid_spec=grid_spec,
        compiler_params=_params(("arbitrary",)),
        name="moe_dispatch",
    )(counts, pad_start, pad_end, pos.reshape(steps, toks * TOP_K), x_rows)


def _expert_kernel(bexp_ref, nused_ref, x_ref, wg_ref, wu_ref, wd_ref, o_ref):
    i = pl.program_id(0)

    @pl.when(i < nused_ref[0])
    def _():
        lo, hi = _unpack_rows(x_ref[...])
        x = jnp.concatenate([lo, hi], axis=1).astype(BF16)
        o_ref[...] = _pack_rows(_swiglu(x, wg_ref[...], wu_ref[...], wd_ref[...]))

    @pl.when(i >= nused_ref[0])
    def _():
        o_ref[...] = jnp.zeros_like(o_ref)


def routed_experts_sorted(x_sorted, block_expert, n_used, wg, wu, wd, rows):
    cap, width = x_sorted.shape
    nb = cap // rows
    d, ff = wg.shape[1], wg.shape[2]
    used = lambda i, be, nu: (jnp.maximum(jnp.minimum(i, nu[0] - 1), 0), 0)
    grid_spec = pltpu.PrefetchScalarGridSpec(
        num_scalar_prefetch=2,
        grid=(nb,),
        in_specs=[pl.BlockSpec((rows, width), used),
                  pl.BlockSpec((None, d, ff), lambda i, be, nu: (be[i], 0, 0)),
                  pl.BlockSpec((None, d, ff), lambda i, be, nu: (be[i], 0, 0)),
                  pl.BlockSpec((None, ff, d), lambda i, be, nu: (be[i], 0, 0))],
        out_specs=pl.BlockSpec((rows, width), lambda i, be, nu: (i, 0)),
    )
    return pl.pallas_call(
        _expert_kernel,
        out_shape=jax.ShapeDtypeStruct((cap, width), x_sorted.dtype),
        grid_spec=grid_spec,
        compiler_params=_params(("arbitrary",)),
        name="routed_experts",
    )(block_expert, n_used, x_sorted, wg, wu, wd)


def _combine_kernel(pos_hbm, y_hbm, base_ref, wt_ref, g_ref, h_ref, hn_ref, idx_smem, gbuf, idx_sem, row_sem):
    i = pl.program_id(0)
    n_steps = pl.num_programs(0)
    toks = gbuf.shape[2]

    def fetch(blk, slot):
        cp = pltpu.make_async_copy(pos_hbm.at[blk], idx_smem.at[slot], idx_sem)
        cp.start()
        cp.wait()

        def body(g, _):
            r0 = pl.multiple_of(g * 8, 8)
            for rr in range(8):
                for k in range(TOP_K):
                    src = idx_smem[slot, (r0 + rr) * TOP_K + k]
                    _row_copy(y_hbm, src, gbuf.at[slot, k], r0 + rr, row_sem.at[slot]).start()
            return 0

        lax.fori_loop(0, toks // 8, body, 0)

    @pl.when(i == 0)
    def _():
        fetch(0, 0)

    @pl.when(i + 1 < n_steps)
    def _():
        fetch(i + 1, (i + 1) % 2)

    slot = i % 2
    for k in range(TOP_K):
        pltpu.make_async_copy(y_hbm.at[pl.ds(0, toks)], gbuf.at[slot, k], row_sem.at[slot]).wait()
    wt = wt_ref[...]
    acc_lo, acc_hi = None, None
    for k in range(TOP_K):
        lo, hi = _unpack_rows(gbuf[slot, k])
        wk = wt[:, k:k + 1]
        acc_lo = lo * wk if acc_lo is None else acc_lo + lo * wk
        acc_hi = hi * wk if acc_hi is None else acc_hi + hi * wk
    h = base_ref[...] + jnp.concatenate([acc_lo, acc_hi], axis=1)
    h_ref[...] = h
    ms = jnp.mean(h * h, axis=-1, keepdims=True)
    hn_ref[...] = (h * lax.rsqrt(ms + EPS) * g_ref[...]).astype(hn_ref.dtype)


def combine(pos, y_sorted, base, wts, gain, tokens=COMBINE_TOKENS):
    m, d = base.shape
    toks = _pick(m, tokens)
    steps, per = m // toks, toks * TOP_K
    pos = pos.reshape(steps, per)
    return pl.pallas_call(
        _combine_kernel,
        out_shape=(jax.ShapeDtypeStruct((m, d), F32), jax.ShapeDtypeStruct((m, d), BF16)),
        grid=(steps,),
        in_specs=[pl.BlockSpec(memory_space=pl.ANY),
                  pl.BlockSpec(memory_space=pl.ANY),
                  pl.BlockSpec((toks, d), lambda i: (i, 0)),
                  pl.BlockSpec((toks, TOP_K), lambda i: (i, 0)),
                  pl.BlockSpec((1, d), lambda i: (0, 0))],
        out_specs=(pl.BlockSpec((toks, d), lambda i: (i, 0)),
                   pl.BlockSpec((toks, d), lambda i: (i, 0))),
        scratch_shapes=[pltpu.SMEM((2, per), jnp.int32),
                        pltpu.VMEM((2, TOP_K, toks, y_sorted.shape[1]), y_sorted.dtype),
                        pltpu.SemaphoreType.DMA(()),
                        pltpu.SemaphoreType.DMA((2,))],
        compiler_params=_params(("arbitrary",)),
        name="moe_combine",
    )(pos, y_sorted, base, wts, gain.reshape(1, d).astype(F32))


def dispatch_tables(eidx, rank, counts, rows):
    n_tok = eidx.shape[0]
    n_exp = counts.shape[0]
    nb = n_tok * TOP_K // rows + n_exp
    pad_end = jnp.cumsum((counts + rows - 1) // rows * rows)
    pad_start = pad_end - (counts + rows - 1) // rows * rows
    experts = jnp.arange(n_exp, dtype=jnp.int32)
    pos = rank + jnp.sum(jnp.where(eidx[..., None] == experts, pad_start, 0), axis=-1)
    blocks = jnp.arange(nb, dtype=jnp.int32)
    block_expert = jnp.minimum(jnp.sum(pad_end[None, :] // rows <= blocks[:, None], axis=-1), n_exp - 1)
    n_used = pad_end[-1:] // rows
    i32 = lambda a: a.astype(jnp.int32)
    return i32(pos).reshape(-1), i32(block_expert), i32(n_used), i32(pad_start), i32(pad_end), nb * rows


def _ple_kernel(hn_ref, wg_ref, p_ref, wp_ref, h_ref, o_ref):
    gate = jax.nn.sigmoid(_dot(hn_ref[...], wg_ref[...]))
    o_ref[...] = h_ref[...] + gate * _dot(p_ref[...].astype(BF16), wp_ref[...])


def ple_gate(hn, w_gate, p, w_ple, h, bm=512, bn=1024):
    m, d = hn.shape
    n = w_gate.shape[1]
    pd = p.shape[1]
    bm, bn = _pick(m, bm), _pick(n, bn)
    return pl.pallas_call(
        _ple_kernel,
        out_shape=jax.ShapeDtypeStruct((m, n), F32),
        grid=(m // bm, n // bn),
        in_specs=[pl.BlockSpec((bm, d), lambda i, j: (i, 0)),
                  pl.BlockSpec((d, bn), lambda i, j: (0, j)),
                  pl.BlockSpec((bm, pd), lambda i, j: (i, 0)),
                  pl.BlockSpec((pd, bn), lambda i, j: (0, j)),
                  pl.BlockSpec((bm, bn), lambda i, j: (i, j))],
        out_specs=pl.BlockSpec((bm, bn), lambda i, j: (i, j)),
        compiler_params=_params(("parallel", "parallel")),
        name="ple_gate",
    )(hn, w_gate, p, w_ple, h)


def _layer(h, p_i, prm, batch):
    (mix_norm, w_in, a_re, a_im, log_step, b_re, b_im, c_re, c_im, ssm_d, w_glu_a, w_glu_b, q_norm, k_norm,
     sinks, w_bs, w_ba, w_out, moe_norm, w_router, router_bias, we_gate, we_up, we_down, ws_gate, ws_up,
     ws_down, ple_norm, w_ple, w_ple_gate) = prm
    n_tok, d = h.shape
    seq = n_tok // batch
    ssm_w = w_glu_a.shape[0]
    n_q = sinks.shape[0]
    attn_w = n_q * HEAD_DIM
    kv_w = attn_w // Q_PER_KV
    n_exp = w_router.shape[1]
    c0, c1 = ssm_w, ssm_w + attn_w + 2 * kv_w
    bf = lambda w: w.astype(BF16)

    hn = rmsnorm(h, mix_norm, BF16)
    u = matmul(hn, bf(w_in[:, :c0]), F32, name="proj_u")
    qkv = matmul(hn, bf(w_in[:, c0:c1]), BF16, bn=(c1 - c0) // 2, name="proj_qkv")
    gates = matmul(hn, bf(w_in[:, c1:]), BF16, name="proj_gates")
    y_pre = s5_scan(u, batch, a_re, a_im, log_step, b_re, b_im, c_re, c_im, ssm_d)
    y_ssm = glu(y_pre, bf(w_glu_a), bf(w_glu_b))
    y_attn = swa_attention(qkv, batch, n_q, q_norm, k_norm, sinks)
    merged = branch_merge(y_ssm, bf(w_bs), y_attn, bf(w_ba), gates)
    h = matmul_residual(merged, bf(w_out), h)

    hm, hm_rows = rmsnorm_packed(h, moe_norm)
    eidx, ew, rank, counts = router(hm, bf(w_router), router_bias)
    counts = counts.reshape(n_exp)
    pos, block_expert, n_used, pad_start, pad_end, cap = dispatch_tables(eidx, rank, counts, MOE_ROWS)
    x_sorted = dispatch_rows(hm_rows, pos, counts, pad_start, pad_end, cap, MOE_ROWS)
    base = shared_expert_residual(hm, bf(ws_gate), bf(ws_up), bf(ws_down), h)
    y_sorted = routed_experts_sorted(x_sorted, block_expert, n_used, bf(we_gate), bf(we_up), bf(we_down), MOE_ROWS)
    h, hn3 = combine(pos, y_sorted, base, ew, ple_norm)

    return ple_gate(hn3, bf(w_ple_gate), p_i, bf(w_ple), h)


def kernel(x, p, mix_norm, w_in, ssm_a_re, ssm_a_im, ssm_log_step, ssm_b_re, ssm_b_im, ssm_c_re, ssm_c_im, ssm_d, w_glu_a, w_glu_b, q_norm, k_norm, attn_sinks, w_branch_ssm, w_branch_attn, w_out, moe_norm, w_router, router_bias, we_gate, we_up, we_down, ws_gate, ws_up, ws_down, ple_norm, w_ple, w_ple_gate):
    bsz, seq, d = x.shape
    layer_params = (mix_norm, w_in, ssm_a_re, ssm_a_im, ssm_log_step, ssm_b_re, ssm_b_im, ssm_c_re, ssm_c_im,
                    ssm_d, w_glu_a, w_glu_b, q_norm, k_norm, attn_sinks, w_branch_ssm, w_branch_attn, w_out,
                    moe_norm, w_router, router_bias, we_gate, we_up, we_down, ws_gate, ws_up, ws_down,
                    ple_norm, w_ple, w_ple_gate)
    h = x.reshape(bsz * seq, d)
    for i in range(mix_norm.shape[0]):
        h = _layer(h, p[i].reshape(bsz * seq, -1), tuple(w[i] for w in layer_params), bsz)
    return h.reshape(bsz, seq, d)
```

```python
import functools
import math

import jax
import jax.numpy as jnp
from jax import lax
from jax.experimental import pallas as pl
from jax.experimental.pallas import tpu as pltpu

SSM_GROUP = 16
SSM_STATE = 64
S5_CHUNK = 8
S5_TILE_LANES = 128
HEAD_DIM = 64
Q_PER_KV = 8
ATTN_BLOCK = 128
ALIBI_MAX_BIAS = 8.0
N_EXPERT_GROUPS = 8
TOPK_GROUPS = 4
TOP_K = 8
ROUTED_SCALE = 2.5
EPS = 1e-6
MOE_ROWS = 256
COMBINE_TOKENS = 128
V7X_VMEM_LIMIT = 56 * 1024 * 1024
EXPERT_VMEM_LIMIT = 62 * 1024 * 1024

BF16 = jnp.bfloat16
F32 = jnp.float32


def _dot(a, b):
    return jnp.dot(a, b, preferred_element_type=F32)


def _params(sem, vmem=V7X_VMEM_LIMIT):
    return pltpu.CompilerParams(dimension_semantics=sem, vmem_limit_bytes=vmem)


def _pick(n, pref):
    b = min(n, pref)
    while n % b:
        b //= 2
    return b


def _rmsnorm_kernel(x_ref, g_ref, o_ref):
    x = x_ref[...]
    ms = jnp.mean(x * x, axis=-1, keepdims=True)
    o_ref[...] = (x * lax.rsqrt(ms + EPS) * g_ref[...]).astype(o_ref.dtype)


def rmsnorm(x, g, out_dtype, bm=256):
    m, d = x.shape
    bm = _pick(m, bm)
    return pl.pallas_call(
        _rmsnorm_kernel,
        out_shape=jax.ShapeDtypeStruct((m, d), out_dtype),
        grid=(m // bm,),
        in_specs=[pl.BlockSpec((bm, d), lambda i: (i, 0)),
                  pl.BlockSpec((1, d), lambda i: (0, 0))],
        out_specs=pl.BlockSpec((bm, d), lambda i: (i, 0)),
        compiler_params=_params(("parallel",)),
        name="rmsnorm",
    )(x, g.reshape(1, d).astype(F32))


def _mm_kernel(x_ref, w_ref, o_ref):
    o_ref[...] = _dot(x_ref[...], w_ref[...]).astype(o_ref.dtype)


def matmul(x, w, out_dtype, bm=1024, bn=1024, name="matmul"):
    m, k = x.shape
    n = w.shape[1]
    bm, bn = _pick(m, bm), _pick(n, bn)
    return pl.pallas_call(
        _mm_kernel,
        out_shape=jax.ShapeDtypeStruct((m, n), out_dtype),
        grid=(m // bm, n // bn),
        in_specs=[pl.BlockSpec((bm, k), lambda i, j: (i, 0)),
                  pl.BlockSpec((k, bn), lambda i, j: (0, j))],
        out_specs=pl.BlockSpec((bm, bn), lambda i, j: (i, j)),
        compiler_params=_params(("parallel", "parallel")),
        name=name,
    )(x, w)


def _spread_groups(compact, n_out, src_of, group_of, rows_per_group):
    n_in = compact.shape[1]
    k = lax.broadcasted_iota(jnp.int32, (n_in, n_out), 0)
    q = lax.broadcasted_iota(jnp.int32, (n_in, n_out), 1)
    spread = _dot(compact, jnp.where(src_of(q) == k, 1.0, 0.0).astype(BF16))
    r = lax.broadcasted_iota(jnp.int32, spread.shape, 0)
    q = lax.broadcasted_iota(jnp.int32, spread.shape, 1)
    return jnp.where(group_of(q) == r // rows_per_group, spread, 0.0).astype(BF16)


def _s5_chunk_matrices(kc_ref, fc_ref, ec_ref, w_s, e_s):
    tc, lanes, cg, ns = S5_CHUNK, S5_TILE_LANES, SSM_GROUP, SSM_STATE
    half = e_s.shape[0] // 2
    w_s[...] = jnp.zeros_like(w_s)
    for tau in range(tc):
        blk = _spread_groups(kc_ref[tau], lanes, lambda q: q % cg, lambda q: q // cg, cg)
        for i in range(tc - tau):
            w_s[i * lanes:(i + 1) * lanes, (i + tau) * lanes:(i + tau + 1) * lanes] = blk
    for i in range(tc):
        f = fc_ref[i]
        rows = slice(i * lanes, (i + 1) * lanes)
        w_s[rows, tc * lanes:tc * lanes + half] = _spread_groups(f, half, lambda q: q % ns, lambda q: q // ns, cg)
        w_s[rows, tc * lanes + half:] = _spread_groups(f, half, lambda q: q % ns + ns, lambda q: q // ns, cg)
    for part in range(2):
        e_s[part * half:(part + 1) * half, :] = _spread_groups(
            ec_ref[part], tc * lanes, lambda q: (q // lanes) * cg + q % cg, lambda q: (q // cg) % (lanes // cg), ns)


def _s5_kernel(u_ref, kc_ref, fc_ref, ec_ref, pw_ref, d_ref, o_ref, fu_ref, w_ref, e_ref):
    @pl.when(pl.program_id(1) == 0)
    def _():
        _s5_chunk_matrices(kc_ref, fc_ref, ec_ref, w_ref, e_ref)

    tc = S5_CHUNK
    n_chunks = u_ref.shape[0] // tc
    lanes = u_ref.shape[1]
    half = fu_ref.shape[1] // 2
    x = jnp.concatenate([u_ref[pl.ds(i, n_chunks, stride=tc), :] for i in range(tc)], axis=1)
    z = _dot(x.astype(BF16), w_ref[...])
    fu_ref[...] = z[:, tc * lanes:]
    row = lax.broadcasted_iota(jnp.int32, (8, half), 0)
    pw = [pw_ref[k] for k in range(10)]

    def cmul(ar, ai, xr, xi):
        return ar * xr - ai * xi, ar * xi + ai * xr

    def body(t, carry):
        c_re, c_im = carry
        r0 = pl.multiple_of(t * 8, 8)
        x_re = fu_ref[pl.ds(r0, 8), pl.ds(0, half)]
        x_im = fu_ref[pl.ds(r0, 8), pl.ds(half, half)]
        for k, sh in enumerate((1, 2, 4)):
            s_re = jnp.where(row >= sh, pltpu.roll(x_re, sh, 0), 0.0)
            s_im = jnp.where(row >= sh, pltpu.roll(x_im, sh, 0), 0.0)
            m_re, m_im = cmul(pw[2 * k], pw[2 * k + 1], s_re, s_im)
            x_re, x_im = x_re + m_re, x_im + m_im
        p_re, p_im = cmul(pw[6], pw[7], c_re, c_im)
        fu_ref[pl.ds(r0, 8), pl.ds(0, half)] = jnp.where(row >= 1, pltpu.roll(x_re, 1, 0), 0.0) + p_re
        fu_ref[pl.ds(r0, 8), pl.ds(half, half)] = jnp.where(row >= 1, pltpu.roll(x_im, 1, 0), 0.0) + p_im
        f_re, f_im = cmul(pw[8], pw[9], c_re, c_im)
        f_re, f_im = f_re + x_re, f_im + x_im
        return jnp.broadcast_to(f_re[7:8], (8, half)), jnp.broadcast_to(f_im[7:8], (8, half))

    zero = jnp.zeros((8, half), F32)
    lax.fori_loop(0, n_chunks // 8, body, (zero, zero))
    y = z[:, :tc * lanes] + _dot(fu_ref[...].astype(BF16), e_ref[...]) + d_ref[...] * x
    y = jax.nn.gelu(y)
    for j in range(tc):
        o_ref[pl.ds(j, n_chunks, stride=tc), :] = y[:, j * lanes:(j + 1) * lanes]


def s5_scan(u, batch, a_re, a_im, log_step, b_re, b_im, c_re, c_im, d_skip):
    rows, width = u.shape
    seq = rows // batch
    tc, lanes = S5_CHUNK, S5_TILE_LANES
    gpt = lanes // SSM_GROUP
    n_tiles = width // lanes
    half = gpt * SSM_STATE
    assert seq % (8 * tc) == 0 and width % lanes == 0
    lam = lax.complex(a_re.astype(F32), a_im.astype(F32))
    lam_dt = lam * jnp.exp(log_step.astype(F32))[:, None]
    b_bar = ((jnp.exp(lam_dt) - 1.0) / lam)[..., None] * lax.complex(b_re.astype(F32), b_im.astype(F32))
    c_mat = lax.complex(c_re.astype(F32), c_im.astype(F32))
    lp = jnp.exp(lam_dt[None] * jnp.arange(tc + 1, dtype=F32)[:, None, None])
    kern = jnp.real(jnp.einsum('gcn,tgn,gnd->tgcd', c_mat, lp[:tc], b_bar))
    kc = kern.transpose(1, 0, 3, 2).reshape(n_tiles, gpt, tc, SSM_GROUP, SSM_GROUP)
    kc = kc.transpose(0, 2, 1, 3, 4).reshape(n_tiles, tc, lanes, SSM_GROUP).astype(BF16)
    f_c = lp[tc - 1 - jnp.arange(tc)][..., None] * b_bar[None]
    f_t = lambda a: (a.transpose(1, 0, 3, 2).reshape(n_tiles, gpt, tc, SSM_GROUP, SSM_STATE)
                     .transpose(0, 2, 1, 3, 4).reshape(n_tiles, tc, lanes, SSM_STATE))
    fc = jnp.concatenate([f_t(jnp.real(f_c)), f_t(jnp.imag(f_c))], axis=-1).astype(BF16)
    e_c = c_mat[None] * lp[1:, :, None, :]
    e_t = lambda a: a.transpose(1, 3, 0, 2).reshape(n_tiles, half, tc * SSM_GROUP)
    ec = jnp.stack([e_t(jnp.real(e_c)), e_t(-jnp.imag(e_c))], axis=1).astype(BF16)
    r8 = jnp.arange(8, dtype=F32)[:, None, None]
    a_pow = lambda e: jnp.exp(lam_dt[None] * (tc * e))
    plist = [a_pow(jnp.full_like(r8, e)) for e in (1.0, 2.0, 4.0)] + [a_pow(r8), a_pow(r8 + 1.0)]
    pws = jnp.stack([f(p) for p in plist for f in (jnp.real, jnp.imag)])
    pws = pws.reshape(10, 8, n_tiles, half).transpose(2, 0, 1, 3)
    dsk = jnp.tile(d_skip.astype(F32).reshape(n_tiles, 1, lanes), (1, 1, tc))
    return pl.pallas_call(
        _s5_kernel,
        out_shape=jax.ShapeDtypeStruct((rows, width), F32),
        grid=(n_tiles, batch),
        in_specs=[pl.BlockSpec((seq, lanes), lambda s, b: (b, s)),
                  pl.BlockSpec((None, tc, lanes, SSM_GROUP), lambda s, b: (s, 0, 0, 0)),
                  pl.BlockSpec((None, tc, lanes, 2 * SSM_STATE), lambda s, b: (s, 0, 0, 0)),
                  pl.BlockSpec((None, 2, half, tc * SSM_GROUP), lambda s, b: (s, 0, 0, 0)),
                  pl.BlockSpec((None, 10, 8, half), lambda s, b: (s, 0, 0, 0)),
                  pl.BlockSpec((None, 1, tc * lanes), lambda s, b: (s, 0, 0))],
        out_specs=pl.BlockSpec((seq, lanes), lambda s, b: (b, s)),
        scratch_shapes=[pltpu.VMEM((seq // tc, 2 * half), F32),
                        pltpu.VMEM((tc * lanes, tc * lanes + 2 * half), BF16),
                        pltpu.VMEM((2 * half, tc * lanes), BF16)],
        compiler_params=_params(("parallel", "arbitrary")),
        name="s5_scan",
    )(u, kc, fc, ec, pws, dsk)


def _glu_kernel(x_ref, wa_ref, wb_ref, o_ref):
    x = x_ref[...].astype(BF16)
    o_ref[...] = (_dot(x, wa_ref[...]) * jax.nn.sigmoid(_dot(x, wb_ref[...]))).astype(o_ref.dtype)


def glu(x, wa, wb, bm=1024, bn=1024):
    m, k = x.shape
    n = wa.shape[1]
    bm, bn = _pick(m, bm), _pick(n, bn)
    return pl.pallas_call(
        _glu_kernel,
        out_shape=jax.ShapeDtypeStruct((m, n), BF16),
        grid=(m // bm, n // bn),
        in_specs=[pl.BlockSpec((bm, k), lambda i, j: (i, 0)),
                  pl.BlockSpec((k, bn), lambda i, j: (0, j)),
                  pl.BlockSpec((k, bn), lambda i, j: (0, j))],
        out_specs=pl.BlockSpec((bm, bn), lambda i, j: (i, j)),
        compiler_params=_params(("parallel", "parallel")),
        name="ssm_glu",
    )(x, wa, wb)


def _attn_kernel(sink_ref, q_ref, kc_ref, kp_ref, vc_ref, vp_ref, qg_ref, kg_ref, o_ref, *, n_kv):
    blk, hd = ATTN_BLOCK, HEAD_DIM
    first = pl.program_id(1) == 0
    n_q = n_kv * Q_PER_KV
    kj = lax.broadcasted_iota(jnp.int32, (2 * blk, blk), 0)
    qi = lax.broadcasted_iota(jnp.int32, (2 * blk, blk), 1)
    dist = qi + blk - kj
    kmin = jnp.where(first, blk, 0)
    valid = (dist >= 0) & (dist < blk) & (kj >= kmin)
    dist_f = dist.astype(F32)

    def head_norm_t(t, heads, gain):
        t3 = t.reshape(heads, hd, t.shape[1])
        ms = jnp.mean(t3 * t3, axis=1, keepdims=True)
        return t3 * lax.rsqrt(ms + EPS) * gain[None]

    qn = (head_norm_t(q_ref[...].astype(F32).T, n_q, qg_ref[...]) * (hd ** -0.5)).astype(BF16)
    kk = jnp.concatenate([kp_ref[...], kc_ref[...]], axis=0).astype(F32)
    kn = head_norm_t(kk.T, n_kv, kg_ref[...]).reshape(n_kv * hd, 2 * blk).T.astype(BF16)
    vt = jnp.concatenate([vp_ref[...], vc_ref[...]], axis=0).astype(F32).T.astype(BF16)

    for h in range(n_kv):
        qt = jnp.concatenate([qn[h * Q_PER_KV + g] for g in range(Q_PER_KV)], axis=1)
        st = _dot(kn[:, h * hd:(h + 1) * hd], qt)
        probs = []
        for g in range(Q_PER_KV):
            head = h * Q_PER_KV + g
            slope = 2.0 ** (-ALIBI_MAX_BIAS * (head + 1) / n_q)
            s = jnp.where(valid, st[:, g * blk:(g + 1) * blk] - slope * dist_f, -jnp.inf)
            sink = sink_ref[head]
            m = jnp.maximum(jnp.max(s, axis=0, keepdims=True), sink)
            p = jnp.exp(s - m)
            denom = jnp.sum(p, axis=0, keepdims=True) + jnp.exp(sink - m)
            probs.append((p * (1.0 / denom)).astype(BF16))
        ot = _dot(vt[h * hd:(h + 1) * hd, :], jnp.concatenate(probs, axis=1))
        ot = jnp.concatenate([ot[:, g * blk:(g + 1) * blk] for g in range(Q_PER_KV)], axis=0)
        o_ref[:, h * Q_PER_KV * hd:(h + 1) * Q_PER_KV * hd] = ot.T.astype(o_ref.dtype)


def swa_attention(qkv, batch, n_q, q_gain, k_gain, sinks):
    rows = qkv.shape[0]
    n_kv = n_q // Q_PER_KV
    qw, kw = n_q * HEAD_DIM, n_kv * HEAD_DIM
    blk = ATTN_BLOCK
    nb = rows // batch // blk
    kcol = qw // kw
    cur = lambda b, j: b * nb + j
    prev = lambda b, j: b * nb + jnp.maximum(j - 1, 0)
    return pl.pallas_call(
        functools.partial(_attn_kernel, n_kv=n_kv),
        out_shape=jax.ShapeDtypeStruct((rows, qw), BF16),
        grid=(batch, nb),
        in_specs=[pl.BlockSpec(memory_space=pltpu.SMEM),
                  pl.BlockSpec((blk, qw), lambda b, j: (cur(b, j), 0)),
                  pl.BlockSpec((blk, kw), lambda b, j: (cur(b, j), kcol)),
                  pl.BlockSpec((blk, kw), lambda b, j: (prev(b, j), kcol)),
                  pl.BlockSpec((blk, kw), lambda b, j: (cur(b, j), kcol + 1)),
                  pl.BlockSpec((blk, kw), lambda b, j: (prev(b, j), kcol + 1)),
                  pl.BlockSpec((HEAD_DIM, blk), lambda b, j: (0, 0)),
                  pl.BlockSpec((HEAD_DIM, 2 * blk), lambda b, j: (0, 0))],
        out_specs=pl.BlockSpec((blk, qw), lambda b, j: (cur(b, j), 0)),
        compiler_params=_params(("parallel", "arbitrary")),
        name="swa_attention",
    )(sinks.astype(F32), qkv, qkv, qkv, qkv, qkv,
      jnp.broadcast_to(q_gain.astype(F32)[:, None], (HEAD_DIM, blk)),
      jnp.broadcast_to(k_gain.astype(F32)[:, None], (HEAD_DIM, 2 * blk)))


def _merge_kernel(ys_ref, ws_ref, ya_ref, wa_ref, gs_ref, ga_ref, o_ref):
    s = jax.nn.sigmoid(gs_ref[...].astype(F32)) * _dot(ys_ref[...], ws_ref[...])
    a = jax.nn.sigmoid(ga_ref[...].astype(F32)) * _dot(ya_ref[...], wa_ref[...])
    o_ref[...] = (s + a).astype(o_ref.dtype)


def branch_merge(y_ssm, w_ssm, y_attn, w_attn, gates, bm=1024, bn=1024):
    m, ks = y_ssm.shape
    ka = y_attn.shape[1]
    n = w_ssm.shape[1]
    bm, bn = _pick(m, bm), _pick(n, bn)
    nj = n // bn
    return pl.pallas_call(
        _merge_kernel,
        out_shape=jax.ShapeDtypeStruct((m, n), BF16),
        grid=(m // bm, nj),
        in_specs=[pl.BlockSpec((bm, ks), lambda i, j: (i, 0)),
                  pl.BlockSpec((ks, bn), lambda i, j: (0, j)),
                  pl.BlockSpec((bm, ka), lambda i, j: (i, 0)),
                  pl.BlockSpec((ka, bn), lambda i, j: (0, j)),
                  pl.BlockSpec((bm, bn), lambda i, j: (i, j)),
                  pl.BlockSpec((bm, bn), lambda i, j: (i, nj + j))],
        out_specs=pl.BlockSpec((bm, bn), lambda i, j: (i, j)),
        compiler_params=_params(("parallel", "parallel")),
        name="branch_merge",
    )(y_ssm, w_ssm, y_attn, w_attn, gates, gates)


def _mm_res_kernel(x_ref, w_ref, r_ref, o_ref):
    o_ref[...] = r_ref[...] + _dot(x_ref[...], w_ref[...])


def matmul_residual(x, w, res, bm=1024, bn=1024):
    m, k = x.shape
    n = w.shape[1]
    bm, bn = _pick(m, bm), _pick(n, bn)
    return pl.pallas_call(
        _mm_res_kernel,
        out_shape=jax.ShapeDtypeStruct((m, n), F32),
        grid=(m // bm, n // bn),
        in_specs=[pl.BlockSpec((bm, k), lambda i, j: (i, 0)),
                  pl.BlockSpec((k, bn), lambda i, j: (0, j)),
                  pl.BlockSpec((bm, bn), lambda i, j: (i, j))],
        out_specs=pl.BlockSpec((bm, bn), lambda i, j: (i, j)),
        compiler_params=_params(("parallel", "parallel")),
        name="out_proj_residual",
    )(x, w, res)


def _router_kernel(x_ref, w_ref, b_ref, idx_ref, wt_ref, rank_ref, cnt_ref, seen_ref, *, n_exp):
    @pl.when(pl.program_id(0) == 0)
    def _():
        seen_ref[...] = jnp.zeros_like(seen_ref)

    s = jax.nn.sigmoid(_dot(x_ref[...], w_ref[...]))
    sel = s + b_ref[...]
    rows = s.shape[0]
    lane = lax.broadcasted_iota(jnp.int32, (rows, n_exp), 1)
    per_group = n_exp // N_EXPERT_GROUPS
    grp = lane // per_group
    neg = -jnp.inf
    gscore = jnp.zeros_like(sel)
    for g in range(N_EXPERT_GROUPS):
        in_g = grp == g
        v = jnp.where(in_g, sel, neg)
        m1 = jnp.max(v, axis=-1, keepdims=True)
        i1 = jnp.min(jnp.where(v == m1, lane, n_exp), axis=-1, keepdims=True)
        m2 = jnp.max(jnp.where(lane == i1, neg, v), axis=-1, keepdims=True)
        gscore = jnp.where(in_g, m1 + m2, gscore)
    cand = jnp.full_like(sel, neg)
    remaining = gscore
    for _ in range(TOPK_GROUPS):
        gm = jnp.max(remaining, axis=-1, keepdims=True)
        gi = jnp.min(jnp.where(remaining == gm, grp, N_EXPERT_GROUPS), axis=-1, keepdims=True)
        hit = grp == gi
        cand = jnp.where(hit, sel, cand)
        remaining = jnp.where(hit, neg, remaining)
    slot = lax.broadcasted_iota(jnp.int32, (rows, TOP_K), 1)
    idx = jnp.zeros((rows, TOP_K), jnp.int32)
    wts = jnp.zeros((rows, TOP_K), F32)
    total = jnp.zeros((rows, 1), F32)
    picked = jnp.zeros_like(sel)
    hits = []
    for k in range(TOP_K):
        mx = jnp.max(cand, axis=-1, keepdims=True)
        ei = jnp.min(jnp.where(cand == mx, lane, n_exp), axis=-1, keepdims=True)
        hit = lane == ei
        wk = jnp.sum(jnp.where(hit, s, 0.0), axis=-1, keepdims=True)
        idx = jnp.where(slot == k, ei, idx)
        wts = jnp.where(slot == k, wk, wts)
        total = total + wk
        cand = jnp.where(hit, neg, cand)
        picked = jnp.where(hit, 1.0, picked)
        hits.append(hit)
    idx_ref[...] = idx
    wt_ref[...] = wts / total * ROUTED_SCALE
    r_i = lax.broadcasted_iota(jnp.int32, (rows, rows), 0)
    c_i = lax.broadcasted_iota(jnp.int32, (rows, rows), 1)
    lower = jnp.where(c_i < r_i, 1.0, 0.0).astype(BF16)
    before = _dot(lower, picked.astype(BF16)) + seen_ref[...]
    rank = jnp.zeros((rows, TOP_K), F32)
    for k in range(TOP_K):
        rank = jnp.where(slot == k, jnp.sum(jnp.where(hits[k], before, 0.0), axis=-1, keepdims=True), rank)
    rank_ref[...] = rank.astype(jnp.int32)
    seen = seen_ref[...] + jnp.sum(picked, axis=0, keepdims=True)
    seen_ref[...] = seen
    cnt_ref[...] = seen.astype(jnp.int32)


def router(hm, w_router, bias, bm=512):
    m, d = hm.shape
    n_exp = w_router.shape[1]
    bm = _pick(m, bm)
    tk = lambda dt: jax.ShapeDtypeStruct((m, TOP_K), dt)
    tk_spec = pl.BlockSpec((bm, TOP_K), lambda i: (i, 0))
    return pl.pallas_call(
        functools.partial(_router_kernel, n_exp=n_exp),
        out_shape=(tk(jnp.int32), tk(F32), tk(jnp.int32), jax.ShapeDtypeStruct((1, n_exp), jnp.int32)),
        grid=(m // bm,),
        in_specs=[pl.BlockSpec((bm, d), lambda i: (i, 0)),
                  pl.BlockSpec((d, n_exp), lambda i: (0, 0)),
                  pl.BlockSpec((1, n_exp), lambda i: (0, 0))],
        out_specs=(tk_spec, tk_spec, tk_spec, pl.BlockSpec((1, n_exp), lambda i: (0, 0))),
        scratch_shapes=[pltpu.VMEM((1, n_exp), F32)],
        compiler_params=_params(("arbitrary",)),
        name="router",
    )(hm, w_router, bias.reshape(1, n_exp).astype(F32))


def _swiglu(x, wg, wu, wd):
    act = (jax.nn.silu(_dot(x, wg)) * _dot(x, wu)).astype(BF16)
    return _dot(act, wd)


def _shared_kernel(x_ref, wg_ref, wu_ref, wd_ref, r_ref, o_ref):
    o_ref[...] = r_ref[...] + _swiglu(x_ref[...], wg_ref[...], wu_ref[...], wd_ref[...])


def shared_expert_residual(hm, wg, wu, wd, res, bm=256):
    m, d = hm.shape
    ff = wg.shape[1]
    bm = _pick(m, bm)
    return pl.pallas_call(
        _shared_kernel,
        out_shape=jax.ShapeDtypeStruct((m, d), F32),
        grid=(m // bm,),
        in_specs=[pl.BlockSpec((bm, d), lambda i: (i, 0)),
                  pl.BlockSpec((d, ff), lambda i: (0, 0)),
                  pl.BlockSpec((d, ff), lambda i: (0, 0)),
                  pl.BlockSpec((ff, d), lambda i: (0, 0)),
                  pl.BlockSpec((bm, d), lambda i: (i, 0))],
        out_specs=pl.BlockSpec((bm, d), lambda i: (i, 0)),
        compiler_params=_params(("parallel",)),
        name="shared_expert",
    )(hm, wg, wu, wd, res)


def _pack_rows(v):
    c = v.shape[1] // 2
    lo = lax.bitcast_convert_type(v[:, :c].astype(BF16).astype(F32), jnp.uint32)
    hi = lax.bitcast_convert_type(v[:, c:].astype(BF16).astype(F32), jnp.uint32)
    return hi | (lo >> 16)


def _unpack_rows(w):
    lo = lax.bitcast_convert_type(w << 16, F32)
    hi = lax.bitcast_convert_type(w & jnp.uint32(0xFFFF0000), F32)
    return lo, hi


def _rmsnorm_pack_kernel(x_ref, g_ref, o_ref, p_ref):
    x = x_ref[...]
    ms = jnp.mean(x * x, axis=-1, keepdims=True)
    y = x * lax.rsqrt(ms + EPS) * g_ref[...]
    o_ref[...] = y.astype(o_ref.dtype)
    p_ref[...] = _pack_rows(y)


def rmsnorm_packed(x, g, bm=256):
    m, d = x.shape
    bm = _pick(m, bm)
    packed = jax.eval_shape(_pack_rows, jax.ShapeDtypeStruct((bm, d), F32))
    pw = packed.shape[1]
    return pl.pallas_call(
        _rmsnorm_pack_kernel,
        out_shape=(jax.ShapeDtypeStruct((m, d), BF16), jax.ShapeDtypeStruct((m, pw), packed.dtype)),
        grid=(m // bm,),
        in_specs=[pl.BlockSpec((bm, d), lambda i: (i, 0)),
                  pl.BlockSpec((1, d), lambda i: (0, 0))],
        out_specs=(pl.BlockSpec((bm, d), lambda i: (i, 0)),
                   pl.BlockSpec((bm, pw), lambda i: (i, 0))),
        compiler_params=_params(("parallel",)),
        name="rmsnorm_packed",
    )(x, g.reshape(1, d).astype(F32))


def _row_copy(src, src_row, dst, dst_row, sem):
    return pltpu.make_async_copy(src.at[pl.ds(src_row, 1)], dst.at[pl.ds(dst_row, 1)], sem)


def _dispatch_kernel(cnt_ref, pstart_ref, pend_ref, pos_hbm, x_hbm, o_hbm,
                     idx_smem, xbuf, zbuf, idx_sem, load_sem, row_sem):
    i = pl.program_id(0)
    n_steps = pl.num_programs(0)
    n_slots, toks = xbuf.shape[0], xbuf.shape[1]
    per = toks * TOP_K

    def idx_copy(step):
        dst = idx_smem.at[pl.ds(pl.multiple_of((step % 2) * per, per), per)]
        return pltpu.make_async_copy(pos_hbm.at[step], dst, idx_sem.at[step % 2])

    def load(step):
        src = x_hbm.at[pl.ds(pl.multiple_of(step * toks, toks), toks)]
        return pltpu.make_async_copy(src, xbuf.at[step % n_slots], load_sem.at[step % n_slots])

    def wait_rows(step):
        pltpu.make_async_copy(o_hbm.at[pl.ds(0, per)], o_hbm.at[pl.ds(0, per)], row_sem.at[step % n_slots]).wait()

    @pl.when(i == 0)
    def _():
        zbuf[...] = jnp.zeros_like(zbuf)
        idx_copy(0).start()
        load(0).start()

    @pl.when(i >= 2)
    def _():
        wait_rows(i - 2)

    @pl.when(i + 1 < n_steps)
    def _():
        idx_copy(i + 1).start()
        load(i + 1).start()

    idx_copy(i).wait()
    load(i).wait()
    slot, islot = i % n_slots, i % 2

    def body(g, _):
        t0 = pl.multiple_of(g * 8, 8)
        for tt in range(8):
            for k in range(TOP_K):
                dst = idx_smem[islot * per + t0 * TOP_K + (tt * TOP_K + k)]
                _row_copy(xbuf.at[slot], t0 + tt, o_hbm, dst, row_sem.at[slot]).start()
        return 0

    lax.fori_loop(0, toks // 8, body, 0)

    @pl.when(i == n_steps - 1)
    def _():
        @pl.when(i >= 1)
        def _():
            wait_rows(i - 1)

        wait_rows(i)

        def expert_padding(e, _):
            first = pstart_ref[e] + cnt_ref[e]
            n_pad = pend_ref[e] - first

            def zbody(r, _):
                _row_copy(zbuf, 0, o_hbm, first + r, row_sem.at[0]).start()
                return 0

            def zwait(r, _):
                _row_copy(zbuf, 0, o_hbm, first, row_sem.at[0]).wait()
                return 0

            lax.fori_loop(0, n_pad, zbody, 0)
            lax.fori_loop(0, n_pad, zwait, 0)
            return 0

        lax.fori_loop(0, cnt_ref.shape[0], expert_padding, 0)

        blk_rows = zbuf.shape[0]
        n_exp = cnt_ref.shape[0]
        first_blk = pend_ref[n_exp - 1] // blk_rows
        n_tail = o_hbm.shape[0] // blk_rows - first_blk

        def tail_copy(b):
            dst = o_hbm.at[pl.ds(pl.multiple_of((first_blk + b) * blk_rows, blk_rows), blk_rows)]
            return pltpu.make_async_copy(zbuf, dst, row_sem.at[0])

        def tbody(b, _):
            tail_copy(b).start()
            return 0

        def twait(b, _):
            tail_copy(b).wait()
            return 0

        lax.fori_loop(0, n_tail, tbody, 0)
        lax.fori_loop(0, n_tail, twait, 0)


def dispatch_rows(x_rows, pos, counts, pad_start, pad_end, cap, blk_rows, chunk_tokens=512):
    n_tok, width = x_rows.shape
    toks = _pick(n_tok, chunk_tokens)
    steps = n_tok // toks
    n_slots = 3
    grid_spec = pltpu.PrefetchScalarGridSpec(
        num_scalar_prefetch=3,
        grid=(steps,),
        in_specs=[pl.BlockSpec(memory_space=pl.ANY)] * 2,
        out_specs=pl.BlockSpec(memory_space=pl.ANY),
        scratch_shapes=[pltpu.SMEM((2 * toks * TOP_K,), jnp.int32),
                        pltpu.VMEM((n_slots, toks, width), x_rows.dtype),
                        pltpu.VMEM((blk_rows, width), x_rows.dtype),
                        pltpu.SemaphoreType.DMA((2,)),
                        pltpu.SemaphoreType.DMA((n_slots,)),
                        pltpu.SemaphoreType.DMA((n_slots,))],
    )
    return pl.pallas_call(
        _dispatch_kernel,
        out_shape=jax.ShapeDtypeStruct((cap, width), x_rows.dtype),
        grid_spec=grid_spec,
        compiler_params=_params(("arbitrary",)),
        name="moe_dispatch",
    )(counts, pad_start, pad_end, pos.reshape(steps, toks * TOP_K), x_rows)


def _expert_kernel(bexp_ref, nused_ref, next_ref, x_ref, wg_hbm, wu_hbm, wd_hbm, o_ref,
                   sg_ref, su_ref, sd_ref, wg_ref, wu_ref, wd_ref, sem):
    i = pl.program_id(0)
    e = bexp_ref[i]

    def fetches(ex):
        return [pltpu.make_async_copy(src.at[ex], dst, sem.at[s])
                for s, (src, dst) in enumerate(((wg_hbm, sg_ref), (wu_hbm, su_ref), (wd_hbm, sd_ref)))]

    def to_bf16(src, dst):
        chunk = _pick(src.shape[0], 256)

        def body(r, _):
            rows = pl.ds(pl.multiple_of(r * chunk, chunk), chunk)
            dst[rows, :] = src[rows, :].astype(BF16)
            return 0

        lax.fori_loop(0, src.shape[0] // chunk, body, 0)

    @pl.when(i < nused_ref[0])
    def _():
        first_block_of_expert = jnp.logical_or(i == 0, bexp_ref[jnp.maximum(i - 1, 0)] != e)

        @pl.when(first_block_of_expert)
        def _():
            @pl.when(i == 0)
            def _():
                for cp in fetches(e):
                    cp.start()

            for cp in fetches(e):
                cp.wait()
            to_bf16(sg_ref, wg_ref)
            to_bf16(su_ref, wu_ref)
            to_bf16(sd_ref, wd_ref)

            @pl.when(next_ref[e] >= 0)
            def _():
                for cp in fetches(next_ref[e]):
                    cp.start()

        lo, hi = _unpack_rows(x_ref[...])
        x = jnp.concatenate([lo, hi], axis=1).astype(BF16)
        o_ref[...] = _pack_rows(_swiglu(x, wg_ref[...], wu_ref[...], wd_ref[...]))

    @pl.when(i >= nused_ref[0])
    def _():
        o_ref[...] = jnp.zeros_like(o_ref)


def routed_experts_sorted(x_sorted, block_expert, n_used, next_expert, wg, wu, wd, rows):
    cap, width = x_sorted.shape
    nb = cap // rows
    d, ff = wg.shape[1], wg.shape[2]
    used = lambda i, be, nu, nx: (jnp.maximum(jnp.minimum(i, nu[0] - 1), 0), 0)
    grid_spec = pltpu.PrefetchScalarGridSpec(
        num_scalar_prefetch=3,
        grid=(nb,),
        in_specs=[pl.BlockSpec((rows, width), used)] + [pl.BlockSpec(memory_space=pl.ANY)] * 3,
        out_specs=pl.BlockSpec((rows, width), lambda i, be, nu, nx: (i, 0)),
        scratch_shapes=[pltpu.VMEM((d, ff), wg.dtype), pltpu.VMEM((d, ff), wu.dtype), pltpu.VMEM((ff, d), wd.dtype),
                        pltpu.VMEM((d, ff), BF16), pltpu.VMEM((d, ff), BF16), pltpu.VMEM((ff, d), BF16),
                        pltpu.SemaphoreType.DMA((3,))],
    )
    return pl.pallas_call(
        _expert_kernel,
        out_shape=jax.ShapeDtypeStruct((cap, width), x_sorted.dtype),
        grid_spec=grid_spec,
        compiler_params=_params(("arbitrary",), EXPERT_VMEM_LIMIT),
        name="routed_experts",
    )(block_expert, n_used, next_expert, x_sorted, wg, wu, wd)


def _combine_kernel(pos_hbm, y_hbm, base_ref, wt_ref, g_ref, h_ref, hn_ref, idx_smem, gbuf, idx_sem, row_sem):
    i = pl.program_id(0)
    n_steps = pl.num_programs(0)
    toks = gbuf.shape[2]
    per = toks * TOP_K

    def fetch(blk, slot):
        cp = pltpu.make_async_copy(pos_hbm.at[blk], idx_smem.at[pl.ds(pl.multiple_of(slot * per, per), per)], idx_sem)
        cp.start()
        cp.wait()

        def body(g, _):
            r0 = pl.multiple_of(g * 8, 8)
            for rr in range(8):
                for k in range(TOP_K):
                    src = idx_smem[slot * per + r0 * TOP_K + (rr * TOP_K + k)]
                    _row_copy(y_hbm, src, gbuf.at[slot, k], r0 + rr, row_sem.at[slot]).start()
            return 0

        lax.fori_loop(0, toks // 8, body, 0)

    @pl.when(i == 0)
    def _():
        fetch(0, 0)

    @pl.when(i + 1 < n_steps)
    def _():
        fetch(i + 1, (i + 1) % 2)

    slot = i % 2
    for k in range(TOP_K):
        pltpu.make_async_copy(y_hbm.at[pl.ds(0, toks)], gbuf.at[slot, k], row_sem.at[slot]).wait()
    wt = wt_ref[...]
    acc_lo, acc_hi = None, None
    for k in range(TOP_K):
        lo, hi = _unpack_rows(gbuf[slot, k])
        wk = wt[:, k:k + 1]
        acc_lo = lo * wk if acc_lo is None else acc_lo + lo * wk
        acc_hi = hi * wk if acc_hi is None else acc_hi + hi * wk
    h = base_ref[...] + jnp.concatenate([acc_lo, acc_hi], axis=1)
    h_ref[...] = h
    ms = jnp.mean(h * h, axis=-1, keepdims=True)
    hn_ref[...] = (h * lax.rsqrt(ms + EPS) * g_ref[...]).astype(hn_ref.dtype)


def combine(pos, y_sorted, base, wts, gain, tokens=COMBINE_TOKENS):
    m, d = base.shape
    toks = _pick(m, tokens)
    steps, per = m // toks, toks * TOP_K
    pos = pos.reshape(steps, per)
    return pl.pallas_call(
        _combine_kernel,
        out_shape=(jax.ShapeDtypeStruct((m, d), F32), jax.ShapeDtypeStruct((m, d), BF16)),
        grid=(steps,),
        in_specs=[pl.BlockSpec(memory_space=pl.ANY),
                  pl.BlockSpec(memory_space=pl.ANY),
                  pl.BlockSpec((toks, d), lambda i: (i, 0)),
                  pl.BlockSpec((toks, TOP_K), lambda i: (i, 0)),
                  pl.BlockSpec((1, d), lambda i: (0, 0))],
        out_specs=(pl.BlockSpec((toks, d), lambda i: (i, 0)),
                   pl.BlockSpec((toks, d), lambda i: (i, 0))),
        scratch_shapes=[pltpu.SMEM((2 * per,), jnp.int32),
                        pltpu.VMEM((2, TOP_K, toks, y_sorted.shape[1]), y_sorted.dtype),
                        pltpu.SemaphoreType.DMA(()),
                        pltpu.SemaphoreType.DMA((2,))],
        compiler_params=_params(("arbitrary",)),
        name="moe_combine",
    )(pos, y_sorted, base, wts, gain.reshape(1, d).astype(F32))


def dispatch_tables(eidx, rank, counts, rows):
    n_tok = eidx.shape[0]
    n_exp = counts.shape[0]
    nb = n_tok * TOP_K // rows + n_exp
    pad_end = jnp.cumsum((counts + rows - 1) // rows * rows)
    pad_start = pad_end - (counts + rows - 1) // rows * rows
    experts = jnp.arange(n_exp, dtype=jnp.int32)
    pos = rank + jnp.sum(jnp.where(eidx[..., None] == experts, pad_start, 0), axis=-1)
    blocks = jnp.arange(nb, dtype=jnp.int32)
    block_expert = jnp.minimum(jnp.sum(pad_end[None, :] // rows <= blocks[:, None], axis=-1), n_exp - 1)
    n_used = pad_end[-1:] // rows
    i32 = lambda a: a.astype(jnp.int32)
    later = (experts[None, :] > experts[:, None]) & (counts[None, :] > 0)
    next_expert = jnp.where(jnp.any(later, axis=1), jnp.argmax(later, axis=1), -1)
    return (i32(pos).reshape(-1), i32(block_expert), i32(n_used), i32(next_expert), i32(pad_start), i32(pad_end),
            nb * rows)


def _ple_kernel(hn_ref, wg_ref, p_ref, wp_ref, h_ref, o_ref):
    gate = jax.nn.sigmoid(_dot(hn_ref[...], wg_ref[...]))
    o_ref[...] = h_ref[...] + gate * _dot(p_ref[...].astype(BF16), wp_ref[...])


def ple_gate(hn, w_gate, p, w_ple, h, bm=512, bn=1024):
    m, d = hn.shape
    n = w_gate.shape[1]
    pd = p.shape[1]
    bm, bn = _pick(m, bm), _pick(n, bn)
    return pl.pallas_call(
        _ple_kernel,
        out_shape=jax.ShapeDtypeStruct((m, n), F32),
        grid=(m // bm, n // bn),
        in_specs=[pl.BlockSpec((bm, d), lambda i, j: (i, 0)),
                  pl.BlockSpec((d, bn), lambda i, j: (0, j)),
                  pl.BlockSpec((bm, pd), lambda i, j: (i, 0)),
                  pl.BlockSpec((pd, bn), lambda i, j: (0, j)),
                  pl.BlockSpec((bm, bn), lambda i, j: (i, j))],
        out_specs=pl.BlockSpec((bm, bn), lambda i, j: (i, j)),
        compiler_params=_params(("parallel", "parallel")),
        name="ple_gate",
    )(hn, w_gate, p, w_ple, h)


def _layer(h, p_i, prm, batch):
    (mix_norm, w_in, a_re, a_im, log_step, b_re, b_im, c_re, c_im, ssm_d, w_glu_a, w_glu_b, q_norm, k_norm,
     sinks, w_bs, w_ba, w_out, moe_norm, w_router, router_bias, we_gate, we_up, we_down, ws_gate, ws_up,
     ws_down, ple_norm, w_ple, w_ple_gate) = prm
    n_tok, d = h.shape
    seq = n_tok // batch
    ssm_w = w_glu_a.shape[0]
    n_q = sinks.shape[0]
    attn_w = n_q * HEAD_DIM
    kv_w = attn_w // Q_PER_KV
    n_exp = w_router.shape[1]
    c0, c1 = ssm_w, ssm_w + attn_w + 2 * kv_w
    bf = lambda w: w.astype(BF16)

    hn = rmsnorm(h, mix_norm, BF16)
    u = matmul(hn, bf(w_in[:, :c0]), F32, name="proj_u")
    qkv = matmul(hn, bf(w_in[:, c0:c1]), BF16, bn=(c1 - c0) // 2, name="proj_qkv")
    gates = matmul(hn, bf(w_in[:, c1:]), BF16, name="proj_gates")
    y_pre = s5_scan(u, batch, a_re, a_im, log_step, b_re, b_im, c_re, c_im, ssm_d)
    y_ssm = glu(y_pre, bf(w_glu_a), bf(w_glu_b))
    y_attn = swa_attention(qkv, batch, n_q, q_norm, k_norm, sinks)
    merged = branch_merge(y_ssm, bf(w_bs), y_attn, bf(w_ba), gates)
    h = matmul_residual(merged, bf(w_out), h)

    hm, hm_rows = rmsnorm_packed(h, moe_norm)
    eidx, ew, rank, counts = router(hm, bf(w_router), router_bias)
    counts = counts.reshape(n_exp)
    pos, block_expert, n_used, next_expert, pad_start, pad_end, cap = dispatch_tables(eidx, rank, counts, MOE_ROWS)
    x_sorted = dispatch_rows(hm_rows, pos, counts, pad_start, pad_end, cap, MOE_ROWS)
    base = shared_expert_residual(hm, bf(ws_gate), bf(ws_up), bf(ws_down), h)
    y_sorted = routed_experts_sorted(x_sorted, block_expert, n_used, next_expert, we_gate, we_up, we_down, MOE_ROWS)
    h, hn3 = combine(pos, y_sorted, base, ew, ple_norm)

    return ple_gate(hn3, bf(w_ple_gate), p_i, bf(w_ple), h)


def kernel(x, p, mix_norm, w_in, ssm_a_re, ssm_a_im, ssm_log_step, ssm_b_re, ssm_b_im, ssm_c_re, ssm_c_im, ssm_d, w_glu_a, w_glu_b, q_norm, k_norm, attn_sinks, w_branch_ssm, w_branch_attn, w_out, moe_norm, w_router, router_bias, we_gate, we_up, we_down, ws_gate, ws_up, ws_down, ple_norm, w_ple, w_ple_gate):
    bsz, seq, d = x.shape
    layer_params = (mix_norm, w_in, ssm_a_re, ssm_a_im, ssm_log_step, ssm_b_re, ssm_b_im, ssm_c_re, ssm_c_im,
                    ssm_d, w_glu_a, w_glu_b, q_norm, k_norm, attn_sinks, w_branch_ssm, w_branch_attn, w_out,
                    moe_norm, w_router, router_bias, we_gate, we_up, we_down, ws_gate, ws_up, ws_down,
                    ple_norm, w_ple, w_ple_gate)
    h = x.reshape(bsz * seq, d)
    for i in range(mix_norm.shape[0]):
        h = _layer(h, p[i].reshape(bsz * seq, -1), tuple(w[i] for w in layer_params), bsz)
    return h.reshape(bsz, seq, d)
```

```python
import functools
import math

import jax
import jax.numpy as jnp
from jax import lax
from jax.experimental import pallas as pl
from jax.experimental.pallas import tpu as pltpu

SSM_GROUP = 16
SSM_STATE = 64
S5_CHUNK = 8
S5_TILE_LANES = 128
HEAD_DIM = 64
Q_PER_KV = 8
ATTN_BLOCK = 128
ALIBI_MAX_BIAS = 8.0
N_EXPERT_GROUPS = 8
TOPK_GROUPS = 4
TOP_K = 8
ROUTED_SCALE = 2.5
EPS = 1e-6
MOE_ROWS = 256
ROW_TILE = 1
COMBINE_TOKENS = 128
COMBINE_GROUP = 16
V7X_VMEM_LIMIT = 56 * 1024 * 1024
EXPERT_VMEM_LIMIT = 62 * 1024 * 1024

BF16 = jnp.bfloat16
F32 = jnp.float32


def _dot(a, b):
    return jnp.dot(a, b, preferred_element_type=F32)


def _params(sem, vmem=V7X_VMEM_LIMIT):
    return pltpu.CompilerParams(dimension_semantics=sem, vmem_limit_bytes=vmem)


def _pick(n, pref):
    b = min(n, pref)
    while n % b:
        b //= 2
    return b


def _rmsnorm_kernel(x_ref, g_ref, o_ref):
    x = x_ref[...]
    ms = jnp.mean(x * x, axis=-1, keepdims=True)
    o_ref[...] = (x * lax.rsqrt(ms + EPS) * g_ref[...]).astype(o_ref.dtype)


def rmsnorm(x, g, out_dtype, bm=256):
    m, d = x.shape
    bm = _pick(m, bm)
    return pl.pallas_call(
        _rmsnorm_kernel,
        out_shape=jax.ShapeDtypeStruct((m, d), out_dtype),
        grid=(m // bm,),
        in_specs=[pl.BlockSpec((bm, d), lambda i: (i, 0)),
                  pl.BlockSpec((1, d), lambda i: (0, 0))],
        out_specs=pl.BlockSpec((bm, d), lambda i: (i, 0)),
        compiler_params=_params(("parallel",)),
        name="rmsnorm",
    )(x, g.reshape(1, d).astype(F32))


def _mm_kernel(x_ref, w_ref, o_ref):
    o_ref[...] = _dot(x_ref[...], w_ref[...]).astype(o_ref.dtype)


def matmul(x, w, out_dtype, bm=1024, bn=1024, name="matmul"):
    m, k = x.shape
    n = w.shape[1]
    bm, bn = _pick(m, bm), _pick(n, bn)
    return pl.pallas_call(
        _mm_kernel,
        out_shape=jax.ShapeDtypeStruct((m, n), out_dtype),
        grid=(m // bm, n // bn),
        in_specs=[pl.BlockSpec((bm, k), lambda i, j: (i, 0)),
                  pl.BlockSpec((k, bn), lambda i, j: (0, j))],
        out_specs=pl.BlockSpec((bm, bn), lambda i, j: (i, j)),
        compiler_params=_params(("parallel", "parallel")),
        name=name,
    )(x, w)


def _spread_groups(compact, n_out, src_of, group_of, rows_per_group):
    n_in = compact.shape[1]
    k = lax.broadcasted_iota(jnp.int32, (n_in, n_out), 0)
    q = lax.broadcasted_iota(jnp.int32, (n_in, n_out), 1)
    spread = _dot(compact, jnp.where(src_of(q) == k, 1.0, 0.0).astype(BF16))
    r = lax.broadcasted_iota(jnp.int32, spread.shape, 0)
    q = lax.broadcasted_iota(jnp.int32, spread.shape, 1)
    return jnp.where(group_of(q) == r // rows_per_group, spread, 0.0).astype(BF16)


def _s5_chunk_matrices(kc_ref, fc_ref, ec_ref, w_s, e_s):
    tc, lanes, cg, ns = S5_CHUNK, S5_TILE_LANES, SSM_GROUP, SSM_STATE
    half = e_s.shape[0] // 2
    w_s[...] = jnp.zeros_like(w_s)
    for tau in range(tc):
        blk = _spread_groups(kc_ref[tau], lanes, lambda q: q % cg, lambda q: q // cg, cg)
        for i in range(tc - tau):
            w_s[i * lanes:(i + 1) * lanes, (i + tau) * lanes:(i + tau + 1) * lanes] = blk
    for i in range(tc):
        f = fc_ref[i]
        rows = slice(i * lanes, (i + 1) * lanes)
        w_s[rows, tc * lanes:tc * lanes + half] = _spread_groups(f, half, lambda q: q % ns, lambda q: q // ns, cg)
        w_s[rows, tc * lanes + half:] = _spread_groups(f, half, lambda q: q % ns + ns, lambda q: q // ns, cg)
    for part in range(2):
        e_s[part * half:(part + 1) * half, :] = _spread_groups(
            ec_ref[part], tc * lanes, lambda q: (q // lanes) * cg + q % cg, lambda q: (q // cg) % (lanes // cg), ns)


def _s5_kernel(u_ref, kc_ref, fc_ref, ec_ref, pw_ref, d_ref, o_ref, fu_ref, w_ref, e_ref):
    @pl.when(pl.program_id(1) == 0)
    def _():
        _s5_chunk_matrices(kc_ref, fc_ref, ec_ref, w_ref, e_ref)

    tc = S5_CHUNK
    n_chunks = u_ref.shape[0] // tc
    lanes = u_ref.shape[1]
    half = fu_ref.shape[1] // 2
    x = jnp.concatenate([u_ref[pl.ds(i, n_chunks, stride=tc), :] for i in range(tc)], axis=1)
    z = _dot(x.astype(BF16), w_ref[...])
    fu_ref[...] = z[:, tc * lanes:]
    row = lax.broadcasted_iota(jnp.int32, (8, half), 0)
    pw = [pw_ref[k] for k in range(10)]

    def cmul(ar, ai, xr, xi):
        return ar * xr - ai * xi, ar * xi + ai * xr

    def body(t, carry):
        c_re, c_im = carry
        r0 = pl.multiple_of(t * 8, 8)
        x_re = fu_ref[pl.ds(r0, 8), pl.ds(0, half)]
        x_im = fu_ref[pl.ds(r0, 8), pl.ds(half, half)]
        for k, sh in enumerate((1, 2, 4)):
            s_re = jnp.where(row >= sh, pltpu.roll(x_re, sh, 0), 0.0)
            s_im = jnp.where(row >= sh, pltpu.roll(x_im, sh, 0), 0.0)
            m_re, m_im = cmul(pw[2 * k], pw[2 * k + 1], s_re, s_im)
            x_re, x_im = x_re + m_re, x_im + m_im
        p_re, p_im = cmul(pw[6], pw[7], c_re, c_im)
        fu_ref[pl.ds(r0, 8), pl.ds(0, half)] = jnp.where(row >= 1, pltpu.roll(x_re, 1, 0), 0.0) + p_re
        fu_ref[pl.ds(r0, 8), pl.ds(half, half)] = jnp.where(row >= 1, pltpu.roll(x_im, 1, 0), 0.0) + p_im
        f_re, f_im = cmul(pw[8], pw[9], c_re, c_im)
        f_re, f_im = f_re + x_re, f_im + x_im
        return jnp.broadcast_to(f_re[7:8], (8, half)), jnp.broadcast_to(f_im[7:8], (8, half))

    zero = jnp.zeros((8, half), F32)
    lax.fori_loop(0, n_chunks // 8, body, (zero, zero))
    y = z[:, :tc * lanes] + _dot(fu_ref[...].astype(BF16), e_ref[...]) + d_ref[...] * x
    y = jax.nn.gelu(y)
    for j in range(tc):
        o_ref[pl.ds(j, n_chunks, stride=tc), :] = y[:, j * lanes:(j + 1) * lanes]


def s5_scan(u, batch, a_re, a_im, log_step, b_re, b_im, c_re, c_im, d_skip):
    rows, width = u.shape
    seq = rows // batch
    tc, lanes = S5_CHUNK, S5_TILE_LANES
    gpt = lanes // SSM_GROUP
    n_tiles = width // lanes
    half = gpt * SSM_STATE
    assert seq % (8 * tc) == 0 and width % lanes == 0
    lam = lax.complex(a_re.astype(F32), a_im.astype(F32))
    lam_dt = lam * jnp.exp(log_step.astype(F32))[:, None]
    b_bar = ((jnp.exp(lam_dt) - 1.0) / lam)[..., None] * lax.complex(b_re.astype(F32), b_im.astype(F32))
    c_mat = lax.complex(c_re.astype(F32), c_im.astype(F32))
    lp = jnp.exp(lam_dt[None] * jnp.arange(tc + 1, dtype=F32)[:, None, None])
    kern = jnp.real(jnp.einsum('gcn,tgn,gnd->tgcd', c_mat, lp[:tc], b_bar))
    kc = kern.transpose(1, 0, 3, 2).reshape(n_tiles, gpt, tc, SSM_GROUP, SSM_GROUP)
    kc = kc.transpose(0, 2, 1, 3, 4).reshape(n_tiles, tc, lanes, SSM_GROUP).astype(BF16)
    f_c = lp[tc - 1 - jnp.arange(tc)][..., None] * b_bar[None]
    f_t = lambda a: (a.transpose(1, 0, 3, 2).reshape(n_tiles, gpt, tc, SSM_GROUP, SSM_STATE)
                     .transpose(0, 2, 1, 3, 4).reshape(n_tiles, tc, lanes, SSM_STATE))
    fc = jnp.concatenate([f_t(jnp.real(f_c)), f_t(jnp.imag(f_c))], axis=-1).astype(BF16)
    e_c = c_mat[None] * lp[1:, :, None, :]
    e_t = lambda a: a.transpose(1, 3, 0, 2).reshape(n_tiles, half, tc * SSM_GROUP)
    ec = jnp.stack([e_t(jnp.real(e_c)), e_t(-jnp.imag(e_c))], axis=1).astype(BF16)
    r8 = jnp.arange(8, dtype=F32)[:, None, None]
    a_pow = lambda e: jnp.exp(lam_dt[None] * (tc * e))
    plist = [a_pow(jnp.full_like(r8, e)) for e in (1.0, 2.0, 4.0)] + [a_pow(r8), a_pow(r8 + 1.0)]
    pws = jnp.stack([f(p) for p in plist for f in (jnp.real, jnp.imag)])
    pws = pws.reshape(10, 8, n_tiles, half).transpose(2, 0, 1, 3)
    dsk = jnp.tile(d_skip.astype(F32).reshape(n_tiles, 1, lanes), (1, 1, tc))
    return pl.pallas_call(
        _s5_kernel,
        out_shape=jax.ShapeDtypeStruct((rows, width), F32),
        grid=(n_tiles, batch),
        in_specs=[pl.BlockSpec((seq, lanes), lambda s, b: (b, s)),
                  pl.BlockSpec((None, tc, lanes, SSM_GROUP), lambda s, b: (s, 0, 0, 0)),
                  pl.BlockSpec((None, tc, lanes, 2 * SSM_STATE), lambda s, b: (s, 0, 0, 0)),
                  pl.BlockSpec((None, 2, half, tc * SSM_GROUP), lambda s, b: (s, 0, 0, 0)),
                  pl.BlockSpec((None, 10, 8, half), lambda s, b: (s, 0, 0, 0)),
                  pl.BlockSpec((None, 1, tc * lanes), lambda s, b: (s, 0, 0))],
        out_specs=pl.BlockSpec((seq, lanes), lambda s, b: (b, s)),
        scratch_shapes=[pltpu.VMEM((seq // tc, 2 * half), F32),
                        pltpu.VMEM((tc * lanes, tc * lanes + 2 * half), BF16),
                        pltpu.VMEM((2 * half, tc * lanes), BF16)],
        compiler_params=_params(("parallel", "arbitrary")),
        name="s5_scan",
    )(u, kc, fc, ec, pws, dsk)


def _glu_kernel(x_ref, wa_ref, wb_ref, o_ref):
    x = x_ref[...].astype(BF16)
    o_ref[...] = (_dot(x, wa_ref[...]) * jax.nn.sigmoid(_dot(x, wb_ref[...]))).astype(o_ref.dtype)


def glu(x, wa, wb, bm=1024, bn=1024):
    m, k = x.shape
    n = wa.shape[1]
    bm, bn = _pick(m, bm), _pick(n, bn)
    return pl.pallas_call(
        _glu_kernel,
        out_shape=jax.ShapeDtypeStruct((m, n), BF16),
        grid=(m // bm, n // bn),
        in_specs=[pl.BlockSpec((bm, k), lambda i, j: (i, 0)),
                  pl.BlockSpec((k, bn), lambda i, j: (0, j)),
                  pl.BlockSpec((k, bn), lambda i, j: (0, j))],
        out_specs=pl.BlockSpec((bm, bn), lambda i, j: (i, j)),
        compiler_params=_params(("parallel", "parallel")),
        name="ssm_glu",
    )(x, wa, wb)


def _attn_kernel(sink_ref, q_ref, kc_ref, kp_ref, vc_ref, vp_ref, qg_ref, kg_ref, o_ref, *, n_kv):
    blk, hd = ATTN_BLOCK, HEAD_DIM
    first = pl.program_id(1) == 0
    n_q = n_kv * Q_PER_KV
    kj = lax.broadcasted_iota(jnp.int32, (2 * blk, blk), 0)
    qi = lax.broadcasted_iota(jnp.int32, (2 * blk, blk), 1)
    dist = qi + blk - kj
    kmin = jnp.where(first, blk, 0)
    valid = (dist >= 0) & (dist < blk) & (kj >= kmin)
    dist_f = dist.astype(F32)

    def head_norm_t(t, heads, gain):
        t3 = t.reshape(heads, hd, t.shape[1])
        ms = jnp.mean(t3 * t3, axis=1, keepdims=True)
        return t3 * lax.rsqrt(ms + EPS) * gain[None]

    qn = (head_norm_t(q_ref[...].astype(F32).T, n_q, qg_ref[...]) * (hd ** -0.5)).astype(BF16)
    kk = jnp.concatenate([kp_ref[...], kc_ref[...]], axis=0).astype(F32)
    kn = head_norm_t(kk.T, n_kv, kg_ref[...]).reshape(n_kv * hd, 2 * blk).T.astype(BF16)
    vt = jnp.concatenate([vp_ref[...], vc_ref[...]], axis=0).astype(F32).T.astype(BF16)

    for h in range(n_kv):
        qt = jnp.concatenate([qn[h * Q_PER_KV + g] for g in range(Q_PER_KV)], axis=1)
        st = _dot(kn[:, h * hd:(h + 1) * hd], qt)
        probs = []
        for g in range(Q_PER_KV):
            head = h * Q_PER_KV + g
            slope = 2.0 ** (-ALIBI_MAX_BIAS * (head + 1) / n_q)
            s = jnp.where(valid, st[:, g * blk:(g + 1) * blk] - slope * dist_f, -jnp.inf)
            sink = sink_ref[head]
            m = jnp.maximum(jnp.max(s, axis=0, keepdims=True), sink)
            p = jnp.exp(s - m)
            denom = jnp.sum(p, axis=0, keepdims=True) + jnp.exp(sink - m)
            probs.append((p * (1.0 / denom)).astype(BF16))
        ot = _dot(vt[h * hd:(h + 1) * hd, :], jnp.concatenate(probs, axis=1))
        ot = jnp.concatenate([ot[:, g * blk:(g + 1) * blk] for g in range(Q_PER_KV)], axis=0)
        o_ref[:, h * Q_PER_KV * hd:(h + 1) * Q_PER_KV * hd] = ot.T.astype(o_ref.dtype)


def swa_attention(qkv, batch, n_q, q_gain, k_gain, sinks):
    rows = qkv.shape[0]
    n_kv = n_q // Q_PER_KV
    qw, kw = n_q * HEAD_DIM, n_kv * HEAD_DIM
    blk = ATTN_BLOCK
    nb = rows // batch // blk
    kcol = qw // kw
    cur = lambda b, j: b * nb + j
    prev = lambda b, j: b * nb + jnp.maximum(j - 1, 0)
    return pl.pallas_call(
        functools.partial(_attn_kernel, n_kv=n_kv),
        out_shape=jax.ShapeDtypeStruct((rows, qw), BF16),
        grid=(batch, nb),
        in_specs=[pl.BlockSpec(memory_space=pltpu.SMEM),
                  pl.BlockSpec((blk, qw), lambda b, j: (cur(b, j), 0)),
                  pl.BlockSpec((blk, kw), lambda b, j: (cur(b, j), kcol)),
                  pl.BlockSpec((blk, kw), lambda b, j: (prev(b, j), kcol)),
                  pl.BlockSpec((blk, kw), lambda b, j: (cur(b, j), kcol + 1)),
                  pl.BlockSpec((blk, kw), lambda b, j: (prev(b, j), kcol + 1)),
                  pl.BlockSpec((HEAD_DIM, blk), lambda b, j: (0, 0)),
                  pl.BlockSpec((HEAD_DIM, 2 * blk), lambda b, j: (0, 0))],
        out_specs=pl.BlockSpec((blk, qw), lambda b, j: (cur(b, j), 0)),
        compiler_params=_params(("parallel", "arbitrary")),
        name="swa_attention",
    )(sinks.astype(F32), qkv, qkv, qkv, qkv, qkv,
      jnp.broadcast_to(q_gain.astype(F32)[:, None], (HEAD_DIM, blk)),
      jnp.broadcast_to(k_gain.astype(F32)[:, None], (HEAD_DIM, 2 * blk)))


def _merge_kernel(ys_ref, ws_ref, ya_ref, wa_ref, gs_ref, ga_ref, o_ref):
    s = jax.nn.sigmoid(gs_ref[...].astype(F32)) * _dot(ys_ref[...], ws_ref[...])
    a = jax.nn.sigmoid(ga_ref[...].astype(F32)) * _dot(ya_ref[...], wa_ref[...])
    o_ref[...] = (s + a).astype(o_ref.dtype)


def branch_merge(y_ssm, w_ssm, y_attn, w_attn, gates, bm=1024, bn=1024):
    m, ks = y_ssm.shape
    ka = y_attn.shape[1]
    n = w_ssm.shape[1]
    bm, bn = _pick(m, bm), _pick(n, bn)
    nj = n // bn
    return pl.pallas_call(
        _merge_kernel,
        out_shape=jax.ShapeDtypeStruct((m, n), BF16),
        grid=(m // bm, nj),
        in_specs=[pl.BlockSpec((bm, ks), lambda i, j: (i, 0)),
                  pl.BlockSpec((ks, bn), lambda i, j: (0, j)),
                  pl.BlockSpec((bm, ka), lambda i, j: (i, 0)),
                  pl.BlockSpec((ka, bn), lambda i, j: (0, j)),
                  pl.BlockSpec((bm, bn), lambda i, j: (i, j)),
                  pl.BlockSpec((bm, bn), lambda i, j: (i, nj + j))],
        out_specs=pl.BlockSpec((bm, bn), lambda i, j: (i, j)),
        compiler_params=_params(("parallel", "parallel")),
        name="branch_merge",
    )(y_ssm, w_ssm, y_attn, w_attn, gates, gates)


def _mm_res_kernel(x_ref, w_ref, r_ref, o_ref):
    o_ref[...] = r_ref[...] + _dot(x_ref[...], w_ref[...])


def matmul_residual(x, w, res, bm=1024, bn=1024):
    m, k = x.shape
    n = w.shape[1]
    bm, bn = _pick(m, bm), _pick(n, bn)
    return pl.pallas_call(
        _mm_res_kernel,
        out_shape=jax.ShapeDtypeStruct((m, n), F32),
        grid=(m // bm, n // bn),
        in_specs=[pl.BlockSpec((bm, k), lambda i, j: (i, 0)),
                  pl.BlockSpec((k, bn), lambda i, j: (0, j)),
                  pl.BlockSpec((bm, bn), lambda i, j: (i, j))],
        out_specs=pl.BlockSpec((bm, bn), lambda i, j: (i, j)),
        compiler_params=_params(("parallel", "parallel")),
        name="out_proj_residual",
    )(x, w, res)


def _router_kernel(x_ref, w_ref, b_ref, idx_ref, wt_ref, rank_ref, cnt_ref, seen_ref, *, n_exp):
    @pl.when(pl.program_id(0) == 0)
    def _():
        seen_ref[...] = jnp.zeros_like(seen_ref)

    s = jax.nn.sigmoid(_dot(x_ref[...], w_ref[...]))
    sel = s + b_ref[...]
    rows = s.shape[0]
    lane_i = lax.broadcasted_iota(jnp.int32, (rows, n_exp), 1)
    per_group = n_exp // N_EXPERT_GROUPS
    lane = lane_i.astype(F32)
    grp = (lane_i // per_group).astype(F32)
    neg = -jnp.inf
    gscore = jnp.zeros_like(sel)
    for g in range(N_EXPERT_GROUPS):
        in_g = grp == g
        v = jnp.where(in_g, sel, neg)
        m1 = jnp.max(v, axis=-1, keepdims=True)
        i1 = jnp.min(jnp.where(v == m1, lane, n_exp), axis=-1, keepdims=True)
        m2 = jnp.max(jnp.where(lane == i1, neg, v), axis=-1, keepdims=True)
        gscore = jnp.where(in_g, m1 + m2, gscore)
    cand = jnp.full_like(sel, neg)
    remaining = gscore
    for _ in range(TOPK_GROUPS):
        gm = jnp.max(remaining, axis=-1, keepdims=True)
        gi = jnp.min(jnp.where(remaining == gm, grp, N_EXPERT_GROUPS), axis=-1, keepdims=True)
        hit = grp == gi
        cand = jnp.where(hit, sel, cand)
        remaining = jnp.where(hit, neg, remaining)
    slot = lax.broadcasted_iota(jnp.int32, (rows, TOP_K), 1)
    idx = jnp.zeros((rows, TOP_K), F32)
    wts = jnp.zeros((rows, TOP_K), F32)
    total = jnp.zeros((rows, 1), F32)
    picked = jnp.zeros_like(sel)
    hits = []
    for k in range(TOP_K):
        mx = jnp.max(cand, axis=-1, keepdims=True)
        ei = jnp.min(jnp.where(cand == mx, lane, n_exp), axis=-1, keepdims=True)
        hit = lane == ei
        wk = jnp.sum(jnp.where(hit, s, 0.0), axis=-1, keepdims=True)
        idx = jnp.where(slot == k, ei, idx)
        wts = jnp.where(slot == k, wk, wts)
        total = total + wk
        cand = jnp.where(hit, neg, cand)
        picked = jnp.where(hit, 1.0, picked)
        hits.append(hit)
    idx_ref[...] = idx.astype(jnp.int32)
    wt_ref[...] = wts / total * ROUTED_SCALE
    r_i = lax.broadcasted_iota(jnp.int32, (rows, rows), 0)
    c_i = lax.broadcasted_iota(jnp.int32, (rows, rows), 1)
    lower = jnp.where(c_i < r_i, 1.0, 0.0).astype(BF16)
    before = _dot(lower, picked.astype(BF16)) + seen_ref[...]
    rank = jnp.zeros((rows, TOP_K), F32)
    for k in range(TOP_K):
        rank = jnp.where(slot == k, jnp.sum(jnp.where(hits[k], before, 0.0), axis=-1, keepdims=True), rank)
    rank_ref[...] = rank.astype(jnp.int32)
    seen = seen_ref[...] + jnp.sum(picked, axis=0, keepdims=True)
    seen_ref[...] = seen
    cnt_ref[...] = seen.astype(jnp.int32)


def router(hm, w_router, bias, bm=512):
    m, d = hm.shape
    n_exp = w_router.shape[1]
    bm = _pick(m, bm)
    tk = lambda dt: jax.ShapeDtypeStruct((m, TOP_K), dt)
    tk_spec = pl.BlockSpec((bm, TOP_K), lambda i: (i, 0))
    return pl.pallas_call(
        functools.partial(_router_kernel, n_exp=n_exp),
        out_shape=(tk(jnp.int32), tk(F32), tk(jnp.int32), jax.ShapeDtypeStruct((1, n_exp), jnp.int32)),
        grid=(m // bm,),
        in_specs=[pl.BlockSpec((bm, d), lambda i: (i, 0)),
                  pl.BlockSpec((d, n_exp), lambda i: (0, 0)),
                  pl.BlockSpec((1, n_exp), lambda i: (0, 0))],
        out_specs=(tk_spec, tk_spec, tk_spec, pl.BlockSpec((1, n_exp), lambda i: (0, 0))),
        scratch_shapes=[pltpu.VMEM((1, n_exp), F32)],
        compiler_params=_params(("arbitrary",)),
        name="router",
    )(hm, w_router, bias.reshape(1, n_exp).astype(F32))


def _swiglu(x, wg, wu, wd):
    act = (jax.nn.silu(_dot(x, wg)) * _dot(x, wu)).astype(BF16)
    return _dot(act, wd)


def _shared_kernel(x_ref, wg_ref, wu_ref, wd_ref, r_ref, o_ref):
    o_ref[...] = r_ref[...] + _swiglu(x_ref[...], wg_ref[...], wu_ref[...], wd_ref[...])


def shared_expert_residual(hm, wg, wu, wd, res, bm=256):
    m, d = hm.shape
    ff = wg.shape[1]
    bm = _pick(m, bm)
    return pl.pallas_call(
        _shared_kernel,
        out_shape=jax.ShapeDtypeStruct((m, d), F32),
        grid=(m // bm,),
        in_specs=[pl.BlockSpec((bm, d), lambda i: (i, 0)),
                  pl.BlockSpec((d, ff), lambda i: (0, 0)),
                  pl.BlockSpec((d, ff), lambda i: (0, 0)),
                  pl.BlockSpec((ff, d), lambda i: (0, 0)),
                  pl.BlockSpec((bm, d), lambda i: (i, 0))],
        out_specs=pl.BlockSpec((bm, d), lambda i: (i, 0)),
        compiler_params=_params(("parallel",)),
        name="shared_expert",
    )(hm, wg, wu, wd, res)


def _pack_rows(v):
    c = v.shape[1] // 2
    lo = lax.bitcast_convert_type(v[:, :c].astype(BF16).astype(F32), jnp.uint32)
    hi = lax.bitcast_convert_type(v[:, c:].astype(BF16).astype(F32), jnp.uint32)
    return hi | (lo >> 16)


def _unpack_rows(w):
    lo = lax.bitcast_convert_type(w << 16, F32)
    hi = lax.bitcast_convert_type(w & jnp.uint32(0xFFFF0000), F32)
    return lo, hi


def _store_row_tiles(ref, v):
    r, w = v.shape
    sub = w // ROW_TILE
    for a in range(ROW_TILE):
        ref[pl.ds(a, r, stride=ROW_TILE), :] = v[:, a * sub:(a + 1) * sub]


def _load_row_tiles(ref, rows):
    return jnp.concatenate([ref[pl.ds(a, rows, stride=ROW_TILE), :] for a in range(ROW_TILE)], axis=1)


def _rmsnorm_pack_kernel(x_ref, g_ref, o_ref, p_ref):
    x = x_ref[...]
    ms = jnp.mean(x * x, axis=-1, keepdims=True)
    y = x * lax.rsqrt(ms + EPS) * g_ref[...]
    o_ref[...] = y.astype(o_ref.dtype)
    _store_row_tiles(p_ref, _pack_rows(y))


def rmsnorm_packed(x, g, bm=256):
    m, d = x.shape
    bm = _pick(m, bm)
    packed = jax.eval_shape(_pack_rows, jax.ShapeDtypeStruct((bm, d), F32))
    sub = packed.shape[1] // ROW_TILE
    return pl.pallas_call(
        _rmsnorm_pack_kernel,
        out_shape=(jax.ShapeDtypeStruct((m, d), BF16), jax.ShapeDtypeStruct((m * ROW_TILE, sub), packed.dtype)),
        grid=(m // bm,),
        in_specs=[pl.BlockSpec((bm, d), lambda i: (i, 0)),
                  pl.BlockSpec((1, d), lambda i: (0, 0))],
        out_specs=(pl.BlockSpec((bm, d), lambda i: (i, 0)),
                   pl.BlockSpec((bm * ROW_TILE, sub), lambda i: (i, 0))),
        compiler_params=_params(("parallel",)),
        name="rmsnorm_packed",
    )(x, g.reshape(1, d).astype(F32))


def _row_copy(src, src_row, dst, dst_row, sem):
    tile = lambda ref, row: ref.at[pl.ds(pl.multiple_of(row * ROW_TILE, ROW_TILE), ROW_TILE)]
    return pltpu.make_async_copy(tile(src, src_row), tile(dst, dst_row), sem)


def _dispatch_kernel(cnt_ref, pstart_ref, pend_ref, pos_hbm, x_hbm, o_hbm,
                     idx_smem, xbuf, zbuf, idx_sem, load_sem, row_sem):
    i = pl.program_id(0)
    n_steps = pl.num_programs(0)
    n_slots, toks = xbuf.shape[0], xbuf.shape[1] // ROW_TILE
    per = toks * TOP_K

    def idx_copy(step):
        dst = idx_smem.at[pl.ds(pl.multiple_of((step % 2) * per, per), per)]
        return pltpu.make_async_copy(pos_hbm.at[step], dst, idx_sem.at[step % 2])

    def load(step):
        src = x_hbm.at[pl.ds(pl.multiple_of(step * toks * ROW_TILE, toks * ROW_TILE), toks * ROW_TILE)]
        return pltpu.make_async_copy(src, xbuf.at[step % n_slots], load_sem.at[step % n_slots])

    def wait_rows(step):
        whole = o_hbm.at[pl.ds(0, per * ROW_TILE)]
        pltpu.make_async_copy(whole, whole, row_sem.at[step % n_slots]).wait()

    @pl.when(i == 0)
    def _():
        zbuf[...] = jnp.zeros_like(zbuf)
        idx_copy(0).start()
        load(0).start()

    @pl.when(i >= 2)
    def _():
        wait_rows(i - 2)

    @pl.when(i + 1 < n_steps)
    def _():
        idx_copy(i + 1).start()
        load(i + 1).start()

    idx_copy(i).wait()
    load(i).wait()
    slot, islot = i % n_slots, i % 2

    def body(g, _):
        t0 = pl.multiple_of(g * 8, 8)
        for tt in range(8):
            for k in range(TOP_K):
                dst = idx_smem[islot * per + t0 * TOP_K + (tt * TOP_K + k)]
                _row_copy(xbuf.at[slot], t0 + tt, o_hbm, dst, row_sem.at[slot]).start()
        return 0

    lax.fori_loop(0, toks // 8, body, 0)

    @pl.when(i == n_steps - 1)
    def _():
        @pl.when(i >= 1)
        def _():
            wait_rows(i - 1)

        wait_rows(i)

        def expert_padding(e, _):
            first = pstart_ref[e] + cnt_ref[e]
            n_pad = pend_ref[e] - first

            def zbody(r, _):
                _row_copy(zbuf, 0, o_hbm, first + r, row_sem.at[0]).start()
                return 0

            def zwait(r, _):
                _row_copy(zbuf, 0, o_hbm, first, row_sem.at[0]).wait()
                return 0

            lax.fori_loop(0, n_pad, zbody, 0)
            lax.fori_loop(0, n_pad, zwait, 0)
            return 0

        lax.fori_loop(0, cnt_ref.shape[0], expert_padding, 0)

        blk_sub = zbuf.shape[0]
        n_exp = cnt_ref.shape[0]
        first_blk = pend_ref[n_exp - 1] // (blk_sub // ROW_TILE)
        n_tail = o_hbm.shape[0] // blk_sub - first_blk

        def tail_copy(b):
            dst = o_hbm.at[pl.ds(pl.multiple_of((first_blk + b) * blk_sub, blk_sub), blk_sub)]
            return pltpu.make_async_copy(zbuf, dst, row_sem.at[0])

        def tbody(b, _):
            tail_copy(b).start()
            return 0

        def twait(b, _):
            tail_copy(b).wait()
            return 0

        lax.fori_loop(0, n_tail, tbody, 0)
        lax.fori_loop(0, n_tail, twait, 0)


def dispatch_rows(x_rows, pos, counts, pad_start, pad_end, cap, blk_rows, chunk_tokens=512):
    n_tok, width = x_rows.shape[0] // ROW_TILE, x_rows.shape[1]
    toks = _pick(n_tok, chunk_tokens)
    steps = n_tok // toks
    n_slots = 3
    grid_spec = pltpu.PrefetchScalarGridSpec(
        num_scalar_prefetch=3,
        grid=(steps,),
        in_specs=[pl.BlockSpec(memory_space=pl.ANY)] * 2,
        out_specs=pl.BlockSpec(memory_space=pl.ANY),
        scratch_shapes=[pltpu.SMEM((2 * toks * TOP_K,), jnp.int32),
                        pltpu.VMEM((n_slots, toks * ROW_TILE, width), x_rows.dtype),
                        pltpu.VMEM((blk_rows * ROW_TILE, width), x_rows.dtype),
                        pltpu.SemaphoreType.DMA((2,)),
                        pltpu.SemaphoreType.DMA((n_slots,)),
                        pltpu.SemaphoreType.DMA((n_slots,))],
    )
    return pl.pallas_call(
        _dispatch_kernel,
        out_shape=jax.ShapeDtypeStruct((cap * ROW_TILE, width), x_rows.dtype),
        grid_spec=grid_spec,
        compiler_params=_params(("arbitrary",)),
        name="moe_dispatch",
    )(counts, pad_start, pad_end, pos.reshape(steps, toks * TOP_K), x_rows)


def _expert_kernel(bexp_ref, nused_ref, next_ref, x_ref, wg_hbm, wu_hbm, wd_hbm, o_ref,
                   sg_ref, su_ref, sd_ref, wg_ref, wu_ref, wd_ref, sem):
    i = pl.program_id(0)
    e = bexp_ref[i]

    def fetches(ex):
        return [pltpu.make_async_copy(src.at[ex], dst, sem.at[s])
                for s, (src, dst) in enumerate(((wg_hbm, sg_ref), (wu_hbm, su_ref), (wd_hbm, sd_ref)))]

    def to_bf16(src, dst):
        chunk = _pick(src.shape[0], 256)

        def body(r, _):
            rows = pl.ds(pl.multiple_of(r * chunk, chunk), chunk)
            dst[rows, :] = src[rows, :].astype(BF16)
            return 0

        lax.fori_loop(0, src.shape[0] // chunk, body, 0)

    @pl.when(i < nused_ref[0])
    def _():
        first_block_of_expert = jnp.logical_or(i == 0, bexp_ref[jnp.maximum(i - 1, 0)] != e)

        @pl.when(first_block_of_expert)
        def _():
            @pl.when(i == 0)
            def _():
                for cp in fetches(e):
                    cp.start()

            for cp in fetches(e):
                cp.wait()
            to_bf16(sg_ref, wg_ref)
            to_bf16(su_ref, wu_ref)
            to_bf16(sd_ref, wd_ref)

            @pl.when(next_ref[e] >= 0)
            def _():
                for cp in fetches(next_ref[e]):
                    cp.start()

        lo, hi = _unpack_rows(_load_row_tiles(x_ref, x_ref.shape[0] // ROW_TILE))
        x = jnp.concatenate([lo, hi], axis=1).astype(BF16)
        _store_row_tiles(o_ref, _pack_rows(_swiglu(x, wg_ref[...], wu_ref[...], wd_ref[...])))

    @pl.when(i >= nused_ref[0])
    def _():
        o_ref[...] = jnp.zeros_like(o_ref)


def routed_experts_sorted(x_sorted, block_expert, n_used, next_expert, wg, wu, wd, rows):
    sub_rows, width = x_sorted.shape
    nb = sub_rows // (rows * ROW_TILE)
    d, ff = wg.shape[1], wg.shape[2]
    used = lambda i, be, nu, nx: (jnp.maximum(jnp.minimum(i, nu[0] - 1), 0), 0)
    grid_spec = pltpu.PrefetchScalarGridSpec(
        num_scalar_prefetch=3,
        grid=(nb,),
        in_specs=[pl.BlockSpec((rows * ROW_TILE, width), used)] + [pl.BlockSpec(memory_space=pl.ANY)] * 3,
        out_specs=pl.BlockSpec((rows * ROW_TILE, width), lambda i, be, nu, nx: (i, 0)),
        scratch_shapes=[pltpu.VMEM((d, ff), wg.dtype), pltpu.VMEM((d, ff), wu.dtype), pltpu.VMEM((ff, d), wd.dtype),
                        pltpu.VMEM((d, ff), BF16), pltpu.VMEM((d, ff), BF16), pltpu.VMEM((ff, d), BF16),
                        pltpu.SemaphoreType.DMA((3,))],
    )
    return pl.pallas_call(
        _expert_kernel,
        out_shape=jax.ShapeDtypeStruct((sub_rows, width), x_sorted.dtype),
        grid_spec=grid_spec,
        compiler_params=_params(("arbitrary",), EXPERT_VMEM_LIMIT),
        name="routed_experts",
    )(block_expert, n_used, next_expert, x_sorted, wg, wu, wd)


def _combine_kernel(pos_hbm, y_hbm, base_ref, wt_ref, g_ref, h_ref, hn_ref, idx_smem, gbuf, idx_sem, row_sem):
    i = pl.program_id(0)
    n_steps = pl.num_programs(0)
    toks = gbuf.shape[2]
    per = toks * TOP_K
    grp = COMBINE_GROUP

    def fetch_indices(blk, slot):
        cp = pltpu.make_async_copy(pos_hbm.at[blk], idx_smem.at[pl.ds(pl.multiple_of(slot * per, per), per)], idx_sem)
        cp.start()
        cp.wait()

    def issue_group(slot, g):
        for r in range(g * grp, (g + 1) * grp):
            for k in range(TOP_K):
                _row_copy(y_hbm, idx_smem[slot * per + (r * TOP_K + k)], gbuf.at[slot, k], r, row_sem.at[slot]).start()

    def combine_group(slot, g):
        rows = slice(g * grp, (g + 1) * grp)
        wt = wt_ref[rows, :]
        acc_lo, acc_hi = None, None
        for k in range(TOP_K):
            lo, hi = _unpack_rows(gbuf[slot, k, rows, :])
            wk = wt[:, k:k + 1]
            acc_lo = lo * wk if acc_lo is None else acc_lo + lo * wk
            acc_hi = hi * wk if acc_hi is None else acc_hi + hi * wk
        h = base_ref[rows, :] + jnp.concatenate([acc_lo, acc_hi], axis=1)
        h_ref[rows, :] = h
        ms = jnp.mean(h * h, axis=-1, keepdims=True)
        hn_ref[rows, :] = (h * lax.rsqrt(ms + EPS) * g_ref[...]).astype(hn_ref.dtype)

    @pl.when(i == 0)
    def _():
        fetch_indices(0, 0)
        for g in range(toks // grp):
            issue_group(0, g)

    slot = i % 2
    for k in range(TOP_K):
        pltpu.make_async_copy(y_hbm.at[pl.ds(0, toks)], gbuf.at[slot, k], row_sem.at[slot]).wait()

    @pl.when(i + 1 < n_steps)
    def _():
        fetch_indices(i + 1, 1 - slot)
        for g in range(toks // grp):
            issue_group(1 - slot, g)
            combine_group(slot, g)

    @pl.when(i + 1 >= n_steps)
    def _():
        for g in range(toks // grp):
            combine_group(slot, g)


def combine(pos, y_sorted, base, wts, gain, tokens=COMBINE_TOKENS):
    m, d = base.shape
    toks = _pick(m, tokens)
    steps, per = m // toks, toks * TOP_K
    pos = pos.reshape(steps, per)
    return pl.pallas_call(
        _combine_kernel,
        out_shape=(jax.ShapeDtypeStruct((m, d), F32), jax.ShapeDtypeStruct((m, d), BF16)),
        grid=(steps,),
        in_specs=[pl.BlockSpec(memory_space=pl.ANY),
                  pl.BlockSpec(memory_space=pl.ANY),
                  pl.BlockSpec((toks, d), lambda i: (i, 0)),
                  pl.BlockSpec((toks, TOP_K), lambda i: (i, 0)),
                  pl.BlockSpec((1, d), lambda i: (0, 0))],
        out_specs=(pl.BlockSpec((toks, d), lambda i: (i, 0)),
                   pl.BlockSpec((toks, d), lambda i: (i, 0))),
        scratch_shapes=[pltpu.SMEM((2 * per,), jnp.int32),
                        pltpu.VMEM((2, TOP_K, toks, y_sorted.shape[1]), y_sorted.dtype),
                        pltpu.SemaphoreType.DMA(()),
                        pltpu.SemaphoreType.DMA((2,))],
        compiler_params=_params(("arbitrary",)),
        name="moe_combine",
    )(pos, y_sorted, base, wts, gain.reshape(1, d).astype(F32))


def dispatch_tables(eidx, rank, counts, rows):
    n_tok = eidx.shape[0]
    n_exp = counts.shape[0]
    nb = n_tok * TOP_K // rows + n_exp
    pad_end = jnp.cumsum((counts + rows - 1) // rows * rows)
    pad_start = pad_end - (counts + rows - 1) // rows * rows
    experts = jnp.arange(n_exp, dtype=jnp.int32)
    pos = rank + jnp.sum(jnp.where(eidx[..., None] == experts, pad_start, 0), axis=-1)
    blocks = jnp.arange(nb, dtype=jnp.int32)
    block_expert = jnp.minimum(jnp.sum(pad_end[None, :] // rows <= blocks[:, None], axis=-1), n_exp - 1)
    n_used = pad_end[-1:] // rows
    i32 = lambda a: a.astype(jnp.int32)
    later = (experts[None, :] > experts[:, None]) & (counts[None, :] > 0)
    next_expert = jnp.where(jnp.any(later, axis=1), jnp.argmax(later, axis=1), -1)
    return (i32(pos).reshape(-1), i32(block_expert), i32(n_used), i32(next_expert), i32(pad_start), i32(pad_end),
            nb * rows)


def _ple_kernel(hn_ref, wg_ref, p_ref, wp_ref, h_ref, o_ref):
    gate = jax.nn.sigmoid(_dot(hn_ref[...], wg_ref[...]))
    o_ref[...] = h_ref[...] + gate * _dot(p_ref[...].astype(BF16), wp_ref[...])


def ple_gate(hn, w_gate, p, w_ple, h, bm=512, bn=1024):
    m, d = hn.shape
    n = w_gate.shape[1]
    pd = p.shape[1]
    bm, bn = _pick(m, bm), _pick(n, bn)
    return pl.pallas_call(
        _ple_kernel,
        out_shape=jax.ShapeDtypeStruct((m, n), F32),
        grid=(m // bm, n // bn),
        in_specs=[pl.BlockSpec((bm, d), lambda i, j: (i, 0)),
                  pl.BlockSpec((d, bn), lambda i, j: (0, j)),
                  pl.BlockSpec((bm, pd), lambda i, j: (i, 0)),
                  pl.BlockSpec((pd, bn), lambda i, j: (0, j)),
                  pl.BlockSpec((bm, bn), lambda i, j: (i, j))],
        out_specs=pl.BlockSpec((bm, bn), lambda i, j: (i, j)),
        compiler_params=_params(("parallel", "parallel")),
        name="ple_gate",
    )(hn, w_gate, p, w_ple, h)


def _layer(h, p_i, prm, batch):
    (mix_norm, w_in, a_re, a_im, log_step, b_re, b_im, c_re, c_im, ssm_d, w_glu_a, w_glu_b, q_norm, k_norm,
     sinks, w_bs, w_ba, w_out, moe_norm, w_router, router_bias, we_gate, we_up, we_down, ws_gate, ws_up,
     ws_down, ple_norm, w_ple, w_ple_gate) = prm
    n_tok, d = h.shape
    seq = n_tok // batch
    ssm_w = w_glu_a.shape[0]
    n_q = sinks.shape[0]
    attn_w = n_q * HEAD_DIM
    kv_w = attn_w // Q_PER_KV
    n_exp = w_router.shape[1]
    c0, c1 = ssm_w, ssm_w + attn_w + 2 * kv_w
    bf = lambda w: w.astype(BF16)

    hn = rmsnorm(h, mix_norm, BF16)
    u = matmul(hn, bf(w_in[:, :c0]), F32, name="proj_u")
    qkv = matmul(hn, bf(w_in[:, c0:c1]), BF16, bn=(c1 - c0) // 2, name="proj_qkv")
    gates = matmul(hn, bf(w_in[:, c1:]), BF16, name="proj_gates")
    y_pre = s5_scan(u, batch, a_re, a_im, log_step, b_re, b_im, c_re, c_im, ssm_d)
    y_ssm = glu(y_pre, bf(w_glu_a), bf(w_glu_b))
    y_attn = swa_attention(qkv, batch, n_q, q_norm, k_norm, sinks)
    merged = branch_merge(y_ssm, bf(w_bs), y_attn, bf(w_ba), gates)
    h = matmul_residual(merged, bf(w_out), h)

    hm, hm_rows = rmsnorm_packed(h, moe_norm)
    eidx, ew, rank, counts = router(hm, bf(w_router), router_bias)
    counts = counts.reshape(n_exp)
    pos, block_expert, n_used, next_expert, pad_start, pad_end, cap = dispatch_tables(eidx, rank, counts, MOE_ROWS)
    x_sorted = dispatch_rows(hm_rows, pos, counts, pad_start, pad_end, cap, MOE_ROWS)
    base = shared_expert_residual(hm, bf(ws_gate), bf(ws_up), bf(ws_down), h)
    y_sorted = routed_experts_sorted(x_sorted, block_expert, n_used, next_expert, we_gate, we_up, we_down, MOE_ROWS)
    h, hn3 = combine(pos, y_sorted, base, ew, ple_norm)

    return ple_gate(hn3, bf(w_ple_gate), p_i, bf(w_ple), h)


def kernel(x, p, mix_norm, w_in, ssm_a_re, ssm_a_im, ssm_log_step, ssm_b_re, ssm_b_im, ssm_c_re, ssm_c_im, ssm_d, w_glu_a, w_glu_b, q_norm, k_norm, attn_sinks, w_branch_ssm, w_branch_attn, w_out, moe_norm, w_router, router_bias, we_gate, we_up, we_down, ws_gate, ws_up, ws_down, ple_norm, w_ple, w_ple_gate):
    bsz, seq, d = x.shape
    layer_params = (mix_norm, w_in, ssm_a_re, ssm_a_im, ssm_log_step, ssm_b_re, ssm_b_im, ssm_c_re, ssm_c_im,
                    ssm_d, w_glu_a, w_glu_b, q_norm, k_norm, attn_sinks, w_branch_ssm, w_branch_attn, w_out,
                    moe_norm, w_router, router_bias, we_gate, we_up, we_down, ws_gate, ws_up, ws_down,
                    ple_norm, w_ple, w_ple_gate)
    h = x.reshape(bsz * seq, d)
    for i in range(mix_norm.shape[0]):
        h = _layer(h, p[i].reshape(bsz * seq, -1), tuple(w[i] for w in layer_params), bsz)
    return h.reshape(bsz, seq, d)
```

```python
import functools
import math

import jax
import jax.numpy as jnp
from jax import lax
from jax.experimental import pallas as pl
from jax.experimental.pallas import tpu as pltpu

SSM_GROUP = 16
SSM_STATE = 64
S5_CHUNK = 8
S5_TILE_LANES = 128
HEAD_DIM = 64
Q_PER_KV = 8
ATTN_BLOCK = 128
ALIBI_MAX_BIAS = 8.0
N_EXPERT_GROUPS = 8
TOPK_GROUPS = 4
TOP_K = 8
ROUTED_SCALE = 2.5
EPS = 1e-6
MOE_ROWS = 256
ROW_TILE = 1
COMBINE_TOKENS = 128
COMBINE_GROUP = 16
V7X_VMEM_LIMIT = 56 * 1024 * 1024
EXPERT_VMEM_LIMIT = 62 * 1024 * 1024

BF16 = jnp.bfloat16
F32 = jnp.float32


def _dot(a, b):
    return jnp.dot(a, b, preferred_element_type=F32)


def _params(sem, vmem=V7X_VMEM_LIMIT):
    return pltpu.CompilerParams(dimension_semantics=sem, vmem_limit_bytes=vmem)


def _pick(n, pref):
    b = min(n, pref)
    while n % b:
        b //= 2
    return b


def _rmsnorm_kernel(x_ref, g_ref, o_ref):
    x = x_ref[...]
    ms = jnp.mean(x * x, axis=-1, keepdims=True)
    o_ref[...] = (x * lax.rsqrt(ms + EPS) * g_ref[...]).astype(o_ref.dtype)


def rmsnorm(x, g, out_dtype, bm=256):
    m, d = x.shape
    bm = _pick(m, bm)
    return pl.pallas_call(
        _rmsnorm_kernel,
        out_shape=jax.ShapeDtypeStruct((m, d), out_dtype),
        grid=(m // bm,),
        in_specs=[pl.BlockSpec((bm, d), lambda i: (i, 0)),
                  pl.BlockSpec((1, d), lambda i: (0, 0))],
        out_specs=pl.BlockSpec((bm, d), lambda i: (i, 0)),
        compiler_params=_params(("parallel",)),
        name="rmsnorm",
    )(x, g.reshape(1, d).astype(F32))


def _mm_kernel(x_ref, w_ref, o_ref):
    o_ref[...] = _dot(x_ref[...], w_ref[...]).astype(o_ref.dtype)


def matmul(x, w, out_dtype, bm=1024, bn=1024, name="matmul"):
    m, k = x.shape
    n = w.shape[1]
    bm, bn = _pick(m, bm), _pick(n, bn)
    return pl.pallas_call(
        _mm_kernel,
        out_shape=jax.ShapeDtypeStruct((m, n), out_dtype),
        grid=(m // bm, n // bn),
        in_specs=[pl.BlockSpec((bm, k), lambda i, j: (i, 0)),
                  pl.BlockSpec((k, bn), lambda i, j: (0, j))],
        out_specs=pl.BlockSpec((bm, bn), lambda i, j: (i, j)),
        compiler_params=_params(("parallel", "parallel")),
        name=name,
    )(x, w)


def _spread_groups(compact, n_out, src_of, group_of, rows_per_group):
    n_in = compact.shape[1]
    k = lax.broadcasted_iota(jnp.int32, (n_in, n_out), 0)
    q = lax.broadcasted_iota(jnp.int32, (n_in, n_out), 1)
    spread = _dot(compact, jnp.where(src_of(q) == k, 1.0, 0.0).astype(BF16))
    r = lax.broadcasted_iota(jnp.int32, spread.shape, 0)
    q = lax.broadcasted_iota(jnp.int32, spread.shape, 1)
    return jnp.where(group_of(q) == r // rows_per_group, spread, 0.0).astype(BF16)


def _s5_chunk_matrices(kc_ref, fc_ref, ec_ref, w_s, e_s):
    tc, lanes, cg, ns = S5_CHUNK, S5_TILE_LANES, SSM_GROUP, SSM_STATE
    half = e_s.shape[0] // 2
    w_s[...] = jnp.zeros_like(w_s)
    for tau in range(tc):
        blk = _spread_groups(kc_ref[tau], lanes, lambda q: q % cg, lambda q: q // cg, cg)
        for i in range(tc - tau):
            w_s[i * lanes:(i + 1) * lanes, (i + tau) * lanes:(i + tau + 1) * lanes] = blk
    for i in range(tc):
        f = fc_ref[i]
        rows = slice(i * lanes, (i + 1) * lanes)
        w_s[rows, tc * lanes:tc * lanes + half] = _spread_groups(f, half, lambda q: q % ns, lambda q: q // ns, cg)
        w_s[rows, tc * lanes + half:] = _spread_groups(f, half, lambda q: q % ns + ns, lambda q: q // ns, cg)
    for part in range(2):
        e_s[part * half:(part + 1) * half, :] = _spread_groups(
            ec_ref[part], tc * lanes, lambda q: (q // lanes) * cg + q % cg, lambda q: (q // cg) % (lanes // cg), ns)


def _s5_kernel(u_ref, kc_ref, fc_ref, ec_ref, pw_ref, d_ref, o_ref, fu_ref, w_ref, e_ref):
    @pl.when(pl.program_id(1) == 0)
    def _():
        _s5_chunk_matrices(kc_ref, fc_ref, ec_ref, w_ref, e_ref)

    tc = S5_CHUNK
    n_chunks = u_ref.shape[0] // tc
    lanes = u_ref.shape[1]
    half = fu_ref.shape[1] // 2
    x = jnp.concatenate([u_ref[pl.ds(i, n_chunks, stride=tc), :] for i in range(tc)], axis=1)
    z = _dot(x.astype(BF16), w_ref[...])
    fu_ref[...] = z[:, tc * lanes:]
    row = lax.broadcasted_iota(jnp.int32, (8, half), 0)
    pw = [pw_ref[k] for k in range(10)]

    def cmul(ar, ai, xr, xi):
        return ar * xr - ai * xi, ar * xi + ai * xr

    def body(t, carry):
        c_re, c_im = carry
        r0 = pl.multiple_of(t * 8, 8)
        x_re = fu_ref[pl.ds(r0, 8), pl.ds(0, half)]
        x_im = fu_ref[pl.ds(r0, 8), pl.ds(half, half)]
        for k, sh in enumerate((1, 2, 4)):
            s_re = jnp.where(row >= sh, pltpu.roll(x_re, sh, 0), 0.0)
            s_im = jnp.where(row >= sh, pltpu.roll(x_im, sh, 0), 0.0)
            m_re, m_im = cmul(pw[2 * k], pw[2 * k + 1], s_re, s_im)
            x_re, x_im = x_re + m_re, x_im + m_im
        p_re, p_im = cmul(pw[6], pw[7], c_re, c_im)
        fu_ref[pl.ds(r0, 8), pl.ds(0, half)] = jnp.where(row >= 1, pltpu.roll(x_re, 1, 0), 0.0) + p_re
        fu_ref[pl.ds(r0, 8), pl.ds(half, half)] = jnp.where(row >= 1, pltpu.roll(x_im, 1, 0), 0.0) + p_im
        f_re, f_im = cmul(pw[8], pw[9], c_re, c_im)
        f_re, f_im = f_re + x_re, f_im + x_im
        return jnp.broadcast_to(f_re[7:8], (8, half)), jnp.broadcast_to(f_im[7:8], (8, half))

    zero = jnp.zeros((8, half), F32)
    lax.fori_loop(0, n_chunks // 8, body, (zero, zero))
    y = z[:, :tc * lanes] + _dot(fu_ref[...].astype(BF16), e_ref[...]) + d_ref[...] * x
    y = jax.nn.gelu(y)
    for j in range(tc):
        o_ref[pl.ds(j, n_chunks, stride=tc), :] = y[:, j * lanes:(j + 1) * lanes]


def s5_scan(u, batch, a_re, a_im, log_step, b_re, b_im, c_re, c_im, d_skip):
    rows, width = u.shape
    seq = rows // batch
    tc, lanes = S5_CHUNK, S5_TILE_LANES
    gpt = lanes // SSM_GROUP
    n_tiles = width // lanes
    half = gpt * SSM_STATE
    assert seq % (8 * tc) == 0 and width % lanes == 0
    lam = lax.complex(a_re.astype(F32), a_im.astype(F32))
    lam_dt = lam * jnp.exp(log_step.astype(F32))[:, None]
    b_bar = ((jnp.exp(lam_dt) - 1.0) / lam)[..., None] * lax.complex(b_re.astype(F32), b_im.astype(F32))
    c_mat = lax.complex(c_re.astype(F32), c_im.astype(F32))
    lp = jnp.exp(lam_dt[None] * jnp.arange(tc + 1, dtype=F32)[:, None, None])
    kern = jnp.real(jnp.einsum('gcn,tgn,gnd->tgcd', c_mat, lp[:tc], b_bar))
    kc = kern.transpose(1, 0, 3, 2).reshape(n_tiles, gpt, tc, SSM_GROUP, SSM_GROUP)
    kc = kc.transpose(0, 2, 1, 3, 4).reshape(n_tiles, tc, lanes, SSM_GROUP).astype(BF16)
    f_c = lp[tc - 1 - jnp.arange(tc)][..., None] * b_bar[None]
    f_t = lambda a: (a.transpose(1, 0, 3, 2).reshape(n_tiles, gpt, tc, SSM_GROUP, SSM_STATE)
                     .transpose(0, 2, 1, 3, 4).reshape(n_tiles, tc, lanes, SSM_STATE))
    fc = jnp.concatenate([f_t(jnp.real(f_c)), f_t(jnp.imag(f_c))], axis=-1).astype(BF16)
    e_c = c_mat[None] * lp[1:, :, None, :]
    e_t = lambda a: a.transpose(1, 3, 0, 2).reshape(n_tiles, half, tc * SSM_GROUP)
    ec = jnp.stack([e_t(jnp.real(e_c)), e_t(-jnp.imag(e_c))], axis=1).astype(BF16)
    r8 = jnp.arange(8, dtype=F32)[:, None, None]
    a_pow = lambda e: jnp.exp(lam_dt[None] * (tc * e))
    plist = [a_pow(jnp.full_like(r8, e)) for e in (1.0, 2.0, 4.0)] + [a_pow(r8), a_pow(r8 + 1.0)]
    pws = jnp.stack([f(p) for p in plist for f in (jnp.real, jnp.imag)])
    pws = pws.reshape(10, 8, n_tiles, half).transpose(2, 0, 1, 3)
    dsk = jnp.tile(d_skip.astype(F32).reshape(n_tiles, 1, lanes), (1, 1, tc))
    return pl.pallas_call(
        _s5_kernel,
        out_shape=jax.ShapeDtypeStruct((rows, width), F32),
        grid=(n_tiles, batch),
        in_specs=[pl.BlockSpec((seq, lanes), lambda s, b: (b, s)),
                  pl.BlockSpec((None, tc, lanes, SSM_GROUP), lambda s, b: (s, 0, 0, 0)),
                  pl.BlockSpec((None, tc, lanes, 2 * SSM_STATE), lambda s, b: (s, 0, 0, 0)),
                  pl.BlockSpec((None, 2, half, tc * SSM_GROUP), lambda s, b: (s, 0, 0, 0)),
                  pl.BlockSpec((None, 10, 8, half), lambda s, b: (s, 0, 0, 0)),
                  pl.BlockSpec((None, 1, tc * lanes), lambda s, b: (s, 0, 0))],
        out_specs=pl.BlockSpec((seq, lanes), lambda s, b: (b, s)),
        scratch_shapes=[pltpu.VMEM((seq // tc, 2 * half), F32),
                        pltpu.VMEM((tc * lanes, tc * lanes + 2 * half), BF16),
                        pltpu.VMEM((2 * half, tc * lanes), BF16)],
        compiler_params=_params(("parallel", "arbitrary")),
        name="s5_scan",
    )(u, kc, fc, ec, pws, dsk)


def _glu_kernel(x_ref, wa_ref, wb_ref, o_ref):
    x = x_ref[...].astype(BF16)
    o_ref[...] = (_dot(x, wa_ref[...]) * jax.nn.sigmoid(_dot(x, wb_ref[...]))).astype(o_ref.dtype)


def glu(x, wa, wb, bm=1024, bn=1024):
    m, k = x.shape
    n = wa.shape[1]
    bm, bn = _pick(m, bm), _pick(n, bn)
    return pl.pallas_call(
        _glu_kernel,
        out_shape=jax.ShapeDtypeStruct((m, n), BF16),
        grid=(m // bm, n // bn),
        in_specs=[pl.BlockSpec((bm, k), lambda i, j: (i, 0)),
                  pl.BlockSpec((k, bn), lambda i, j: (0, j)),
                  pl.BlockSpec((k, bn), lambda i, j: (0, j))],
        out_specs=pl.BlockSpec((bm, bn), lambda i, j: (i, j)),
        compiler_params=_params(("parallel", "parallel")),
        name="ssm_glu",
    )(x, wa, wb)


def _attn_kernel(sink_ref, q_ref, kc_ref, kp_ref, vc_ref, vp_ref, qg_ref, kg_ref, o_ref, *, n_kv):
    blk, hd = ATTN_BLOCK, HEAD_DIM
    first = pl.program_id(1) == 0
    n_q = n_kv * Q_PER_KV
    kj = lax.broadcasted_iota(jnp.int32, (2 * blk, blk), 0)
    qi = lax.broadcasted_iota(jnp.int32, (2 * blk, blk), 1)
    dist = qi + blk - kj
    kmin = jnp.where(first, blk, 0)
    valid = (dist >= 0) & (dist < blk) & (kj >= kmin)
    dist_f = dist.astype(F32)

    def head_norm_t(t, heads, gain):
        t3 = t.reshape(heads, hd, t.shape[1])
        ms = jnp.mean(t3 * t3, axis=1, keepdims=True)
        return t3 * lax.rsqrt(ms + EPS) * gain[None]

    qn = (head_norm_t(q_ref[...].astype(F32).T, n_q, qg_ref[...]) * (hd ** -0.5)).astype(BF16)
    kk = jnp.concatenate([kp_ref[...], kc_ref[...]], axis=0).astype(F32)
    kn = head_norm_t(kk.T, n_kv, kg_ref[...]).reshape(n_kv * hd, 2 * blk).T.astype(BF16)
    vt = jnp.concatenate([vp_ref[...], vc_ref[...]], axis=0).astype(F32).T.astype(BF16)

    for h in range(n_kv):
        qt = jnp.concatenate([qn[h * Q_PER_KV + g] for g in range(Q_PER_KV)], axis=1)
        st = _dot(kn[:, h * hd:(h + 1) * hd], qt)
        probs = []
        for g in range(Q_PER_KV):
            head = h * Q_PER_KV + g
            slope = 2.0 ** (-ALIBI_MAX_BIAS * (head + 1) / n_q)
            s = jnp.where(valid, st[:, g * blk:(g + 1) * blk] - slope * dist_f, -jnp.inf)
            sink = sink_ref[head]
            m = jnp.maximum(jnp.max(s, axis=0, keepdims=True), sink)
            p = jnp.exp(s - m)
            denom = jnp.sum(p, axis=0, keepdims=True) + jnp.exp(sink - m)
            probs.append((p * (1.0 / denom)).astype(BF16))
        ot = _dot(vt[h * hd:(h + 1) * hd, :], jnp.concatenate(probs, axis=1))
        ot = jnp.concatenate([ot[:, g * blk:(g + 1) * blk] for g in range(Q_PER_KV)], axis=0)
        o_ref[:, h * Q_PER_KV * hd:(h + 1) * Q_PER_KV * hd] = ot.T.astype(o_ref.dtype)


def swa_attention(qkv, batch, n_q, q_gain, k_gain, sinks):
    rows = qkv.shape[0]
    n_kv = n_q // Q_PER_KV
    qw, kw = n_q * HEAD_DIM, n_kv * HEAD_DIM
    blk = ATTN_BLOCK
    nb = rows // batch // blk
    kcol = qw // kw
    cur = lambda b, j: b * nb + j
    prev = lambda b, j: b * nb + jnp.maximum(j - 1, 0)
    return pl.pallas_call(
        functools.partial(_attn_kernel, n_kv=n_kv),
        out_shape=jax.ShapeDtypeStruct((rows, qw), BF16),
        grid=(batch, nb),
        in_specs=[pl.BlockSpec(memory_space=pltpu.SMEM),
                  pl.BlockSpec((blk, qw), lambda b, j: (cur(b, j), 0)),
                  pl.BlockSpec((blk, kw), lambda b, j: (cur(b, j), kcol)),
                  pl.BlockSpec((blk, kw), lambda b, j: (prev(b, j), kcol)),
                  pl.BlockSpec((blk, kw), lambda b, j: (cur(b, j), kcol + 1)),
                  pl.BlockSpec((blk, kw), lambda b, j: (prev(b, j), kcol + 1)),
                  pl.BlockSpec((HEAD_DIM, blk), lambda b, j: (0, 0)),
                  pl.BlockSpec((HEAD_DIM, 2 * blk), lambda b, j: (0, 0))],
        out_specs=pl.BlockSpec((blk, qw), lambda b, j: (cur(b, j), 0)),
        compiler_params=_params(("parallel", "arbitrary")),
        name="swa_attention",
    )(sinks.astype(F32), qkv, qkv, qkv, qkv, qkv,
      jnp.broadcast_to(q_gain.astype(F32)[:, None], (HEAD_DIM, blk)),
      jnp.broadcast_to(k_gain.astype(F32)[:, None], (HEAD_DIM, 2 * blk)))


def _merge_kernel(ys_ref, ws_ref, ya_ref, wa_ref, gs_ref, ga_ref, o_ref):
    s = jax.nn.sigmoid(gs_ref[...].astype(F32)) * _dot(ys_ref[...], ws_ref[...])
    a = jax.nn.sigmoid(ga_ref[...].astype(F32)) * _dot(ya_ref[...], wa_ref[...])
    o_ref[...] = (s + a).astype(o_ref.dtype)


def branch_merge(y_ssm, w_ssm, y_attn, w_attn, gates, bm=1024, bn=1024):
    m, ks = y_ssm.shape
    ka = y_attn.shape[1]
    n = w_ssm.shape[1]
    bm, bn = _pick(m, bm), _pick(n, bn)
    nj = n // bn
    return pl.pallas_call(
        _merge_kernel,
        out_shape=jax.ShapeDtypeStruct((m, n), BF16),
        grid=(m // bm, nj),
        in_specs=[pl.BlockSpec((bm, ks), lambda i, j: (i, 0)),
                  pl.BlockSpec((ks, bn), lambda i, j: (0, j)),
                  pl.BlockSpec((bm, ka), lambda i, j: (i, 0)),
                  pl.BlockSpec((ka, bn), lambda i, j: (0, j)),
                  pl.BlockSpec((bm, bn), lambda i, j: (i, j)),
                  pl.BlockSpec((bm, bn), lambda i, j: (i, nj + j))],
        out_specs=pl.BlockSpec((bm, bn), lambda i, j: (i, j)),
        compiler_params=_params(("parallel", "parallel")),
        name="branch_merge",
    )(y_ssm, w_ssm, y_attn, w_attn, gates, gates)


def _mm_res_kernel(x_ref, w_ref, r_ref, o_ref):
    o_ref[...] = r_ref[...] + _dot(x_ref[...], w_ref[...])


def matmul_residual(x, w, res, bm=1024, bn=1024):
    m, k = x.shape
    n = w.shape[1]
    bm, bn = _pick(m, bm), _pick(n, bn)
    return pl.pallas_call(
        _mm_res_kernel,
        out_shape=jax.ShapeDtypeStruct((m, n), F32),
        grid=(m // bm, n // bn),
        in_specs=[pl.BlockSpec((bm, k), lambda i, j: (i, 0)),
                  pl.BlockSpec((k, bn), lambda i, j: (0, j)),
                  pl.BlockSpec((bm, bn), lambda i, j: (i, j))],
        out_specs=pl.BlockSpec((bm, bn), lambda i, j: (i, j)),
        compiler_params=_params(("parallel", "parallel")),
        name="out_proj_residual",
    )(x, w, res)


def _router_kernel(h_ref, g_ref, w_ref, b_ref, hm_ref, rows_ref, idx_ref, wt_ref, rank_ref, cnt_ref, seen_ref,
                   *, n_exp):
    @pl.when(pl.program_id(0) == 0)
    def _():
        seen_ref[...] = jnp.zeros_like(seen_ref)

    h = h_ref[...]
    hn = h * lax.rsqrt(jnp.mean(h * h, axis=-1, keepdims=True) + EPS) * g_ref[...]
    hm = hn.astype(BF16)
    hm_ref[...] = hm
    _store_row_tiles(rows_ref, _pack_rows(hn))
    s = jax.nn.sigmoid(_dot(hm, w_ref[...]))
    sel = s + b_ref[...]
    rows = s.shape[0]
    lane_i = lax.broadcasted_iota(jnp.int32, (rows, n_exp), 1)
    per_group = n_exp // N_EXPERT_GROUPS
    lane = lane_i.astype(F32)
    grp = (lane_i // per_group).astype(F32)
    neg = -jnp.inf
    gscore = jnp.zeros_like(sel)
    for g in range(N_EXPERT_GROUPS):
        in_g = grp == g
        v = jnp.where(in_g, sel, neg)
        m1 = jnp.max(v, axis=-1, keepdims=True)
        i1 = jnp.min(jnp.where(v == m1, lane, n_exp), axis=-1, keepdims=True)
        m2 = jnp.max(jnp.where(lane == i1, neg, v), axis=-1, keepdims=True)
        gscore = jnp.where(in_g, m1 + m2, gscore)
    cand = jnp.full_like(sel, neg)
    remaining = gscore
    for _ in range(TOPK_GROUPS):
        gm = jnp.max(remaining, axis=-1, keepdims=True)
        gi = jnp.min(jnp.where(remaining == gm, grp, N_EXPERT_GROUPS), axis=-1, keepdims=True)
        hit = grp == gi
        cand = jnp.where(hit, sel, cand)
        remaining = jnp.where(hit, neg, remaining)
    slot = lax.broadcasted_iota(jnp.int32, (rows, TOP_K), 1)
    idx = jnp.zeros((rows, TOP_K), F32)
    wts = jnp.zeros((rows, TOP_K), F32)
    total = jnp.zeros((rows, 1), F32)
    picked = jnp.zeros_like(sel)
    hits = []
    for k in range(TOP_K):
        mx = jnp.max(cand, axis=-1, keepdims=True)
        ei = jnp.min(jnp.where(cand == mx, lane, n_exp), axis=-1, keepdims=True)
        hit = lane == ei
        wk = jnp.sum(jnp.where(hit, s, 0.0), axis=-1, keepdims=True)
        idx = jnp.where(slot == k, ei, idx)
        wts = jnp.where(slot == k, wk, wts)
        total = total + wk
        cand = jnp.where(hit, neg, cand)
        picked = jnp.where(hit, 1.0, picked)
        hits.append(hit)
    idx_ref[...] = idx.astype(jnp.int32)
    wt_ref[...] = wts / total * ROUTED_SCALE
    r_i = lax.broadcasted_iota(jnp.int32, (rows, rows), 0)
    c_i = lax.broadcasted_iota(jnp.int32, (rows, rows), 1)
    lower = jnp.where(c_i < r_i, 1.0, 0.0).astype(BF16)
    before = _dot(lower, picked.astype(BF16)) + seen_ref[...]
    rank = jnp.zeros((rows, TOP_K), F32)
    for k in range(TOP_K):
        rank = jnp.where(slot == k, jnp.sum(jnp.where(hits[k], before, 0.0), axis=-1, keepdims=True), rank)
    rank_ref[...] = rank.astype(jnp.int32)
    seen = seen_ref[...] + jnp.sum(picked, axis=0, keepdims=True)
    seen_ref[...] = seen
    cnt_ref[...] = seen.astype(jnp.int32)


def norm_and_route(h, gain, w_router, bias, bm=512):
    m, d = h.shape
    n_exp = w_router.shape[1]
    bm = _pick(m, bm)
    packed = jax.eval_shape(_pack_rows, jax.ShapeDtypeStruct((bm, d), F32))
    sub = packed.shape[1] // ROW_TILE
    tk = lambda dt: jax.ShapeDtypeStruct((m, TOP_K), dt)
    tk_spec = pl.BlockSpec((bm, TOP_K), lambda i: (i, 0))
    return pl.pallas_call(
        functools.partial(_router_kernel, n_exp=n_exp),
        out_shape=(jax.ShapeDtypeStruct((m, d), BF16), jax.ShapeDtypeStruct((m * ROW_TILE, sub), packed.dtype),
                   tk(jnp.int32), tk(F32), tk(jnp.int32), jax.ShapeDtypeStruct((1, n_exp), jnp.int32)),
        grid=(m // bm,),
        in_specs=[pl.BlockSpec((bm, d), lambda i: (i, 0)),
                  pl.BlockSpec((1, d), lambda i: (0, 0)),
                  pl.BlockSpec((d, n_exp), lambda i: (0, 0)),
                  pl.BlockSpec((1, n_exp), lambda i: (0, 0))],
        out_specs=(pl.BlockSpec((bm, d), lambda i: (i, 0)),
                   pl.BlockSpec((bm * ROW_TILE, sub), lambda i: (i, 0)),
                   tk_spec, tk_spec, tk_spec, pl.BlockSpec((1, n_exp), lambda i: (0, 0))),
        scratch_shapes=[pltpu.VMEM((1, n_exp), F32)],
        compiler_params=_params(("arbitrary",)),
        name="norm_and_route",
    )(h, gain.reshape(1, d).astype(F32), w_router, bias.reshape(1, n_exp).astype(F32))


def _swiglu(x, wg, wu, wd):
    act = (jax.nn.silu(_dot(x, wg)) * _dot(x, wu)).astype(BF16)
    return _dot(act, wd)


def _shared_kernel(x_ref, wg_ref, wu_ref, wd_ref, r_ref, o_ref):
    o_ref[...] = r_ref[...] + _swiglu(x_ref[...], wg_ref[...], wu_ref[...], wd_ref[...])


def shared_expert_residual(hm, wg, wu, wd, res, bm=256):
    m, d = hm.shape
    ff = wg.shape[1]
    bm = _pick(m, bm)
    return pl.pallas_call(
        _shared_kernel,
        out_shape=jax.ShapeDtypeStruct((m, d), F32),
        grid=(m // bm,),
        in_specs=[pl.BlockSpec((bm, d), lambda i: (i, 0)),
                  pl.BlockSpec((d, ff), lambda i: (0, 0)),
                  pl.BlockSpec((d, ff), lambda i: (0, 0)),
                  pl.BlockSpec((ff, d), lambda i: (0, 0)),
                  pl.BlockSpec((bm, d), lambda i: (i, 0))],
        out_specs=pl.BlockSpec((bm, d), lambda i: (i, 0)),
        compiler_params=_params(("parallel",)),
        name="shared_expert",
    )(hm, wg, wu, wd, res)


def _pack_rows(v):
    c = v.shape[1] // 2
    lo = lax.bitcast_convert_type(v[:, :c].astype(BF16).astype(F32), jnp.uint32)
    hi = lax.bitcast_convert_type(v[:, c:].astype(BF16).astype(F32), jnp.uint32)
    return hi | (lo >> 16)


def _unpack_rows(w):
    lo = lax.bitcast_convert_type(w << 16, F32)
    hi = lax.bitcast_convert_type(w & jnp.uint32(0xFFFF0000), F32)
    return lo, hi


def _store_row_tiles(ref, v):
    r, w = v.shape
    sub = w // ROW_TILE
    for a in range(ROW_TILE):
        ref[pl.ds(a, r, stride=ROW_TILE), :] = v[:, a * sub:(a + 1) * sub]


def _load_row_tiles(ref, rows):
    return jnp.concatenate([ref[pl.ds(a, rows, stride=ROW_TILE), :] for a in range(ROW_TILE)], axis=1)


def _row_copy(src, src_row, dst, dst_row, sem):
    tile = lambda ref, row: ref.at[pl.ds(pl.multiple_of(row * ROW_TILE, ROW_TILE), ROW_TILE)]
    return pltpu.make_async_copy(tile(src, src_row), tile(dst, dst_row), sem)


def _dispatch_kernel(cnt_ref, pstart_ref, pend_ref, pos_hbm, x_hbm, o_hbm,
                     idx_smem, xbuf, zbuf, idx_sem, load_sem, row_sem):
    i = pl.program_id(0)
    n_steps = pl.num_programs(0)
    n_slots, toks = xbuf.shape[0], xbuf.shape[1] // ROW_TILE
    per = toks * TOP_K

    def idx_copy(step):
        dst = idx_smem.at[pl.ds(pl.multiple_of((step % 2) * per, per), per)]
        return pltpu.make_async_copy(pos_hbm.at[step], dst, idx_sem.at[step % 2])

    def load(step):
        src = x_hbm.at[pl.ds(pl.multiple_of(step * toks * ROW_TILE, toks * ROW_TILE), toks * ROW_TILE)]
        return pltpu.make_async_copy(src, xbuf.at[step % n_slots], load_sem.at[step % n_slots])

    def wait_rows(step):
        whole = o_hbm.at[pl.ds(0, per * ROW_TILE)]
        pltpu.make_async_copy(whole, whole, row_sem.at[step % n_slots]).wait()

    @pl.when(i == 0)
    def _():
        zbuf[...] = jnp.zeros_like(zbuf)
        idx_copy(0).start()
        load(0).start()

    @pl.when(i >= 2)
    def _():
        wait_rows(i - 2)

    @pl.when(i + 1 < n_steps)
    def _():
        idx_copy(i + 1).start()
        load(i + 1).start()

    idx_copy(i).wait()
    load(i).wait()
    slot, islot = i % n_slots, i % 2

    def body(g, _):
        t0 = pl.multiple_of(g * 8, 8)
        for tt in range(8):
            for k in range(TOP_K):
                dst = idx_smem[islot * per + t0 * TOP_K + (tt * TOP_K + k)]
                _row_copy(xbuf.at[slot], t0 + tt, o_hbm, dst, row_sem.at[slot]).start(priority=k % 2)
        return 0

    lax.fori_loop(0, toks // 8, body, 0)

    @pl.when(i == n_steps - 1)
    def _():
        @pl.when(i >= 1)
        def _():
            wait_rows(i - 1)

        wait_rows(i)

        def expert_padding(e, _):
            first = pstart_ref[e] + cnt_ref[e]
            n_pad = pend_ref[e] - first

            def zbody(r, _):
                _row_copy(zbuf, 0, o_hbm, first + r, row_sem.at[0]).start()
                return 0

            def zwait(r, _):
                _row_copy(zbuf, 0, o_hbm, first, row_sem.at[0]).wait()
                return 0

            lax.fori_loop(0, n_pad, zbody, 0)
            lax.fori_loop(0, n_pad, zwait, 0)
            return 0

        lax.fori_loop(0, cnt_ref.shape[0], expert_padding, 0)

        blk_sub = zbuf.shape[0]
        n_exp = cnt_ref.shape[0]
        first_blk = pend_ref[n_exp - 1] // (blk_sub // ROW_TILE)
        n_tail = o_hbm.shape[0] // blk_sub - first_blk

        def tail_copy(b):
            dst = o_hbm.at[pl.ds(pl.multiple_of((first_blk + b) * blk_sub, blk_sub), blk_sub)]
            return pltpu.make_async_copy(zbuf, dst, row_sem.at[0])

        def tbody(b, _):
            tail_copy(b).start()
            return 0

        def twait(b, _):
            tail_copy(b).wait()
            return 0

        lax.fori_loop(0, n_tail, tbody, 0)
        lax.fori_loop(0, n_tail, twait, 0)


def dispatch_rows(x_rows, pos, counts, pad_start, pad_end, cap, blk_rows, chunk_tokens=512):
    n_tok, width = x_rows.shape[0] // ROW_TILE, x_rows.shape[1]
    toks = _pick(n_tok, chunk_tokens)
    steps = n_tok // toks
    n_slots = 3
    grid_spec = pltpu.PrefetchScalarGridSpec(
        num_scalar_prefetch=3,
        grid=(steps,),
        in_specs=[pl.BlockSpec(memory_space=pl.ANY)] * 2,
        out_specs=pl.BlockSpec(memory_space=pl.ANY),
        scratch_shapes=[pltpu.SMEM((2 * toks * TOP_K,), jnp.int32),
                        pltpu.VMEM((n_slots, toks * ROW_TILE, width), x_rows.dtype),
                        pltpu.VMEM((blk_rows * ROW_TILE, width), x_rows.dtype),
                        pltpu.SemaphoreType.DMA((2,)),
                        pltpu.SemaphoreType.DMA((n_slots,)),
                        pltpu.SemaphoreType.DMA((n_slots,))],
    )
    return pl.pallas_call(
        _dispatch_kernel,
        out_shape=jax.ShapeDtypeStruct((cap * ROW_TILE, width), x_rows.dtype),
        grid_spec=grid_spec,
        compiler_params=_params(("arbitrary",)),
        name="moe_dispatch",
    )(counts, pad_start, pad_end, pos.reshape(steps, toks * TOP_K), x_rows)


def _expert_kernel(bexp_ref, nused_ref, next_ref, x_ref, wg_hbm, wu_hbm, wd_hbm, o_ref,
                   sg_ref, su_ref, sd_ref, wg_ref, wu_ref, wd_ref, sem):
    i = pl.program_id(0)
    e = bexp_ref[i]

    def fetches(ex):
        return [pltpu.make_async_copy(src.at[ex], dst, sem.at[s])
                for s, (src, dst) in enumerate(((wg_hbm, sg_ref), (wu_hbm, su_ref), (wd_hbm, sd_ref)))]

    def to_bf16(src, dst):
        chunk = _pick(src.shape[0], 256)

        def body(r, _):
            rows = pl.ds(pl.multiple_of(r * chunk, chunk), chunk)
            dst[rows, :] = src[rows, :].astype(BF16)
            return 0

        lax.fori_loop(0, src.shape[0] // chunk, body, 0)

    @pl.when(i < nused_ref[0])
    def _():
        first_block_of_expert = jnp.logical_or(i == 0, bexp_ref[jnp.maximum(i - 1, 0)] != e)

        @pl.when(first_block_of_expert)
        def _():
            @pl.when(i == 0)
            def _():
                for cp in fetches(e):
                    cp.start()

            for cp in fetches(e):
                cp.wait()
            to_bf16(sg_ref, wg_ref)
            to_bf16(su_ref, wu_ref)
            to_bf16(sd_ref, wd_ref)

            @pl.when(next_ref[e] >= 0)
            def _():
                for cp in fetches(next_ref[e]):
                    cp.start()

        lo, hi = _unpack_rows(_load_row_tiles(x_ref, x_ref.shape[0] // ROW_TILE))
        x = jnp.concatenate([lo, hi], axis=1).astype(BF16)
        _store_row_tiles(o_ref, _pack_rows(_swiglu(x, wg_ref[...], wu_ref[...], wd_ref[...])))

    @pl.when(i >= nused_ref[0])
    def _():
        o_ref[...] = jnp.zeros_like(o_ref)


def routed_experts_sorted(x_sorted, block_expert, n_used, next_expert, wg, wu, wd, rows):
    sub_rows, width = x_sorted.shape
    nb = sub_rows // (rows * ROW_TILE)
    d, ff = wg.shape[1], wg.shape[2]
    used = lambda i, be, nu, nx: (jnp.maximum(jnp.minimum(i, nu[0] - 1), 0), 0)
    grid_spec = pltpu.PrefetchScalarGridSpec(
        num_scalar_prefetch=3,
        grid=(nb,),
        in_specs=[pl.BlockSpec((rows * ROW_TILE, width), used)] + [pl.BlockSpec(memory_space=pl.ANY)] * 3,
        out_specs=pl.BlockSpec((rows * ROW_TILE, width), lambda i, be, nu, nx: (i, 0)),
        scratch_shapes=[pltpu.VMEM((d, ff), wg.dtype), pltpu.VMEM((d, ff), wu.dtype), pltpu.VMEM((ff, d), wd.dtype),
                        pltpu.VMEM((d, ff), BF16), pltpu.VMEM((d, ff), BF16), pltpu.VMEM((ff, d), BF16),
                        pltpu.SemaphoreType.DMA((3,))],
    )
    return pl.pallas_call(
        _expert_kernel,
        out_shape=jax.ShapeDtypeStruct((sub_rows, width), x_sorted.dtype),
        grid_spec=grid_spec,
        compiler_params=_params(("arbitrary",), EXPERT_VMEM_LIMIT),
        name="routed_experts",
    )(block_expert, n_used, next_expert, x_sorted, wg, wu, wd)


def _combine_kernel(pos_hbm, y_hbm, base_ref, wt_ref, g_ref, h_ref, hn_ref, idx_smem, gbuf, idx_sem, row_sem):
    i = pl.program_id(0)
    n_steps = pl.num_programs(0)
    toks = gbuf.shape[2]
    per = toks * TOP_K
    grp = COMBINE_GROUP

    def fetch_indices(blk, slot):
        cp = pltpu.make_async_copy(pos_hbm.at[blk], idx_smem.at[pl.ds(pl.multiple_of(slot * per, per), per)], idx_sem)
        cp.start()
        cp.wait()

    def issue_group(slot, g):
        for r in range(g * grp, (g + 1) * grp):
            for k in range(TOP_K):
                _row_copy(y_hbm, idx_smem[slot * per + (r * TOP_K + k)], gbuf.at[slot, k], r,
                          row_sem.at[slot]).start(priority=k % 2)

    def combine_group(slot, g):
        rows = slice(g * grp, (g + 1) * grp)
        wt = wt_ref[rows, :]
        acc_lo, acc_hi = None, None
        for k in range(TOP_K):
            lo, hi = _unpack_rows(gbuf[slot, k, rows, :])
            wk = wt[:, k:k + 1]
            acc_lo = lo * wk if acc_lo is None else acc_lo + lo * wk
            acc_hi = hi * wk if acc_hi is None else acc_hi + hi * wk
        h = base_ref[rows, :] + jnp.concatenate([acc_lo, acc_hi], axis=1)
        h_ref[rows, :] = h
        ms = jnp.mean(h * h, axis=-1, keepdims=True)
        hn_ref[rows, :] = (h * lax.rsqrt(ms + EPS) * g_ref[...]).astype(hn_ref.dtype)

    @pl.when(i == 0)
    def _():
        fetch_indices(0, 0)
        for g in range(toks // grp):
            issue_group(0, g)

    slot = i % 2
    for k in range(TOP_K):
        pltpu.make_async_copy(y_hbm.at[pl.ds(0, toks)], gbuf.at[slot, k], row_sem.at[slot]).wait()

    @pl.when(i + 1 < n_steps)
    def _():
        fetch_indices(i + 1, 1 - slot)
        for g in range(toks // grp):
            issue_group(1 - slot, g)
            combine_group(slot, g)

    @pl.when(i + 1 >= n_steps)
    def _():
        for g in range(toks // grp):
            combine_group(slot, g)


def combine(pos, y_sorted, base, wts, gain, tokens=COMBINE_TOKENS):
    m, d = base.shape
    toks = _pick(m, tokens)
    steps, per = m // toks, toks * TOP_K
    pos = pos.reshape(steps, per)
    return pl.pallas_call(
        _combine_kernel,
        out_shape=(jax.ShapeDtypeStruct((m, d), F32), jax.ShapeDtypeStruct((m, d), BF16)),
        grid=(steps,),
        in_specs=[pl.BlockSpec(memory_space=pl.ANY),
                  pl.BlockSpec(memory_space=pl.ANY),
                  pl.BlockSpec((toks, d), lambda i: (i, 0)),
                  pl.BlockSpec((toks, TOP_K), lambda i: (i, 0)),
                  pl.BlockSpec((1, d), lambda i: (0, 0))],
        out_specs=(pl.BlockSpec((toks, d), lambda i: (i, 0)),
                   pl.BlockSpec((toks, d), lambda i: (i, 0))),
        scratch_shapes=[pltpu.SMEM((2 * per,), jnp.int32),
                        pltpu.VMEM((2, TOP_K, toks, y_sorted.shape[1]), y_sorted.dtype),
                        pltpu.SemaphoreType.DMA(()),
                        pltpu.SemaphoreType.DMA((2,))],
        compiler_params=_params(("arbitrary",)),
        name="moe_combine",
    )(pos, y_sorted, base, wts, gain.reshape(1, d).astype(F32))


def dispatch_tables(eidx, rank, counts, rows):
    n_tok = eidx.shape[0]
    n_exp = counts.shape[0]
    nb = n_tok * TOP_K // rows + n_exp
    pad_end = jnp.cumsum((counts + rows - 1) // rows * rows)
    pad_start = pad_end - (counts + rows - 1) // rows * rows
    experts = jnp.arange(n_exp, dtype=jnp.int32)
    pos = rank + jnp.sum(jnp.where(eidx[..., None] == experts, pad_start, 0), axis=-1)
    blocks = jnp.arange(nb, dtype=jnp.int32)
    block_expert = jnp.minimum(jnp.sum(pad_end[None, :] // rows <= blocks[:, None], axis=-1), n_exp - 1)
    n_used = pad_end[-1:] // rows
    i32 = lambda a: a.astype(jnp.int32)
    later = (experts[None, :] > experts[:, None]) & (counts[None, :] > 0)
    next_expert = jnp.where(jnp.any(later, axis=1), jnp.argmax(later, axis=1), -1)
    return (i32(pos).reshape(-1), i32(block_expert), i32(n_used), i32(next_expert), i32(pad_start), i32(pad_end),
            nb * rows)


def _ple_kernel(hn_ref, wg_ref, p_ref, wp_ref, h_ref, o_ref):
    gate = jax.nn.sigmoid(_dot(hn_ref[...], wg_ref[...]))
    o_ref[...] = h_ref[...] + gate * _dot(p_ref[...].astype(BF16), wp_ref[...])


def ple_gate(hn, w_gate, p, w_ple, h, bm=512, bn=1024):
    m, d = hn.shape
    n = w_gate.shape[1]
    pd = p.shape[1]
    bm, bn = _pick(m, bm), _pick(n, bn)
    return pl.pallas_call(
        _ple_kernel,
        out_shape=jax.ShapeDtypeStruct((m, n), F32),
        grid=(m // bm, n // bn),
        in_specs=[pl.BlockSpec((bm, d), lambda i, j: (i, 0)),
                  pl.BlockSpec((d, bn), lambda i, j: (0, j)),
                  pl.BlockSpec((bm, pd), lambda i, j: (i, 0)),
                  pl.BlockSpec((pd, bn), lambda i, j: (0, j)),
                  pl.BlockSpec((bm, bn), lambda i, j: (i, j))],
        out_specs=pl.BlockSpec((bm, bn), lambda i, j: (i, j)),
        compiler_params=_params(("parallel", "parallel")),
        name="ple_gate",
    )(hn, w_gate, p, w_ple, h)


def _layer(h, p_i, prm, batch):
    (mix_norm, w_in, a_re, a_im, log_step, b_re, b_im, c_re, c_im, ssm_d, w_glu_a, w_glu_b, q_norm, k_norm,
     sinks, w_bs, w_ba, w_out, moe_norm, w_router, router_bias, we_gate, we_up, we_down, ws_gate, ws_up,
     ws_down, ple_norm, w_ple, w_ple_gate) = prm
    n_tok, d = h.shape
    seq = n_tok // batch
    ssm_w = w_glu_a.shape[0]
    n_q = sinks.shape[0]
    attn_w = n_q * HEAD_DIM
    kv_w = attn_w // Q_PER_KV
    n_exp = w_router.shape[1]
    c0, c1 = ssm_w, ssm_w + attn_w + 2 * kv_w
    bf = lambda w: w.astype(BF16)

    hn = rmsnorm(h, mix_norm, BF16)
    u = matmul(hn, bf(w_in[:, :c0]), F32, name="proj_u")
    qkv = matmul(hn, bf(w_in[:, c0:c1]), BF16, bn=(c1 - c0) // 2, name="proj_qkv")
    gates = matmul(hn, bf(w_in[:, c1:]), BF16, name="proj_gates")
    y_pre = s5_scan(u, batch, a_re, a_im, log_step, b_re, b_im, c_re, c_im, ssm_d)
    y_ssm = glu(y_pre, bf(w_glu_a), bf(w_glu_b))
    y_attn = swa_attention(qkv, batch, n_q, q_norm, k_norm, sinks)
    merged = branch_merge(y_ssm, bf(w_bs), y_attn, bf(w_ba), gates)
    h = matmul_residual(merged, bf(w_out), h)

    hm, hm_rows, eidx, ew, rank, counts = norm_and_route(h, moe_norm, bf(w_router), router_bias)
    counts = counts.reshape(n_exp)
    pos, block_expert, n_used, next_expert, pad_start, pad_end, cap = dispatch_tables(eidx, rank, counts, MOE_ROWS)
    x_sorted = dispatch_rows(hm_rows, pos, counts, pad_start, pad_end, cap, MOE_ROWS)
    base = shared_expert_residual(hm, bf(ws_gate), bf(ws_up), bf(ws_down), h)
    y_sorted = routed_experts_sorted(x_sorted, block_expert, n_used, next_expert, we_gate, we_up, we_down, MOE_ROWS)
    h, hn3 = combine(pos, y_sorted, base, ew, ple_norm)

    return ple_gate(hn3, bf(w_ple_gate), p_i, bf(w_ple), h)


def kernel(x, p, mix_norm, w_in, ssm_a_re, ssm_a_im, ssm_log_step, ssm_b_re, ssm_b_im, ssm_c_re, ssm_c_im, ssm_d, w_glu_a, w_glu_b, q_norm, k_norm, attn_sinks, w_branch_ssm, w_branch_attn, w_out, moe_norm, w_router, router_bias, we_gate, we_up, we_down, ws_gate, ws_up, ws_down, ple_norm, w_ple, w_ple_gate):
    bsz, seq, d = x.shape
    layer_params = (mix_norm, w_in, ssm_a_re, ssm_a_im, ssm_log_step, ssm_b_re, ssm_b_im, ssm_c_re, ssm_c_im,
                    ssm_d, w_glu_a, w_glu_b, q_norm, k_norm, attn_sinks, w_branch_ssm, w_branch_attn, w_out,
                    moe_norm, w_router, router_bias, we_gate, we_up, we_down, ws_gate, ws_up, ws_down,
                    ple_norm, w_ple, w_ple_gate)
    h = x.reshape(bsz * seq, d)
    for i in range(mix_norm.shape[0]):
        h = _layer(h, p[i].reshape(bsz * seq, -1), tuple(w[i] for w in layer_params), bsz)
    return h.reshape(bsz, seq, d)
```

```python
import functools

import jax
import jax.numpy as jnp
from jax import lax
from jax.experimental import pallas as pl
from jax.experimental.pallas import tpu as pltpu

SSM_GROUP = 16
SSM_STATE = 64
S5_CHUNK = 8
S5_TILE_LANES = 128
HEAD_DIM = 64
Q_PER_KV = 8
ATTN_BLOCK = 128
ALIBI_MAX_BIAS = 8.0
N_EXPERT_GROUPS = 8
TOPK_GROUPS = 4
TOP_K = 8
ROUTED_SCALE = 2.5
EPS = 1e-6
MOE_ROWS = 256
COMBINE_TOKENS = 128
COMBINE_GROUP = 16
V7X_VMEM_LIMIT = 56 * 1024 * 1024
EXPERT_VMEM_LIMIT = 62 * 1024 * 1024

BF16 = jnp.bfloat16
F32 = jnp.float32


def _dot(a, b):
    return jnp.dot(a, b, preferred_element_type=F32)


def _params(sem, vmem=V7X_VMEM_LIMIT):
    return pltpu.CompilerParams(dimension_semantics=sem, vmem_limit_bytes=vmem)


def _pick(n, pref):
    b = min(n, pref)
    while n % b:
        b //= 2
    return b


def _rmsnorm_kernel(x_ref, g_ref, o_ref):
    x = x_ref[...]
    ms = jnp.mean(x * x, axis=-1, keepdims=True)
    o_ref[...] = (x * lax.rsqrt(ms + EPS) * g_ref[...]).astype(o_ref.dtype)


def rmsnorm(x, g, out_dtype, bm=256):
    m, d = x.shape
    bm = _pick(m, bm)
    return pl.pallas_call(
        _rmsnorm_kernel,
        out_shape=jax.ShapeDtypeStruct((m, d), out_dtype),
        grid=(m // bm,),
        in_specs=[pl.BlockSpec((bm, d), lambda i: (i, 0)),
                  pl.BlockSpec((1, d), lambda i: (0, 0))],
        out_specs=pl.BlockSpec((bm, d), lambda i: (i, 0)),
        compiler_params=_params(("parallel",)),
        name="rmsnorm",
    )(x, g.reshape(1, d).astype(F32))


def _mm_kernel(x_ref, w_ref, o_ref):
    o_ref[...] = _dot(x_ref[...], w_ref[...]).astype(o_ref.dtype)


def matmul(x, w, out_dtype, bm=1024, bn=1024, name="matmul"):
    m, k = x.shape
    n = w.shape[1]
    bm, bn = _pick(m, bm), _pick(n, bn)
    return pl.pallas_call(
        _mm_kernel,
        out_shape=jax.ShapeDtypeStruct((m, n), out_dtype),
        grid=(m // bm, n // bn),
        in_specs=[pl.BlockSpec((bm, k), lambda i, j: (i, 0)),
                  pl.BlockSpec((k, bn), lambda i, j: (0, j))],
        out_specs=pl.BlockSpec((bm, bn), lambda i, j: (i, j)),
        compiler_params=_params(("parallel", "parallel")),
        name=name,
    )(x, w)


def _spread_groups(compact, n_out, src_of, group_of, rows_per_group):
    n_in = compact.shape[1]
    k = lax.broadcasted_iota(jnp.int32, (n_in, n_out), 0)
    q = lax.broadcasted_iota(jnp.int32, (n_in, n_out), 1)
    spread = _dot(compact, jnp.where(src_of(q) == k, 1.0, 0.0).astype(BF16))
    r = lax.broadcasted_iota(jnp.int32, spread.shape, 0)
    q = lax.broadcasted_iota(jnp.int32, spread.shape, 1)
    return jnp.where(group_of(q) == r // rows_per_group, spread, 0.0).astype(BF16)


def _s5_chunk_matrices(kc_ref, fc_ref, ec_ref, w_s, e_s):
    tc, lanes, cg, ns = S5_CHUNK, S5_TILE_LANES, SSM_GROUP, SSM_STATE
    half = e_s.shape[0] // 2
    w_s[...] = jnp.zeros_like(w_s)
    for tau in range(tc):
        blk = _spread_groups(kc_ref[tau], lanes, lambda q: q % cg, lambda q: q // cg, cg)
        for i in range(tc - tau):
            w_s[i * lanes:(i + 1) * lanes, (i + tau) * lanes:(i + tau + 1) * lanes] = blk
    for i in range(tc):
        f = fc_ref[i]
        rows = slice(i * lanes, (i + 1) * lanes)
        w_s[rows, tc * lanes:tc * lanes + half] = _spread_groups(f, half, lambda q: q % ns, lambda q: q // ns, cg)
        w_s[rows, tc * lanes + half:] = _spread_groups(f, half, lambda q: q % ns + ns, lambda q: q // ns, cg)
    for part in range(2):
        e_s[part * half:(part + 1) * half, :] = _spread_groups(
            ec_ref[part], tc * lanes, lambda q: (q // lanes) * cg + q % cg, lambda q: (q // cg) % (lanes // cg), ns)


def _s5_kernel(u_ref, kc_ref, fc_ref, ec_ref, pw_ref, d_ref, o_ref, fu_ref, w_ref, e_ref):
    @pl.when(pl.program_id(1) == 0)
    def _():
        _s5_chunk_matrices(kc_ref, fc_ref, ec_ref, w_ref, e_ref)

    tc = S5_CHUNK
    n_chunks = u_ref.shape[0] // tc
    lanes = u_ref.shape[1]
    half = fu_ref.shape[1] // 2
    x = jnp.concatenate([u_ref[pl.ds(i, n_chunks, stride=tc), :] for i in range(tc)], axis=1)
    z = _dot(x.astype(BF16), w_ref[...])
    fu_ref[...] = z[:, tc * lanes:]
    row = lax.broadcasted_iota(jnp.int32, (8, half), 0)
    pw = [pw_ref[k] for k in range(10)]

    def cmul(ar, ai, xr, xi):
        return ar * xr - ai * xi, ar * xi + ai * xr

    def body(t, carry):
        c_re, c_im = carry
        r0 = pl.multiple_of(t * 8, 8)
        x_re = fu_ref[pl.ds(r0, 8), pl.ds(0, half)]
        x_im = fu_ref[pl.ds(r0, 8), pl.ds(half, half)]
        for k, sh in enumerate((1, 2, 4)):
            s_re = jnp.where(row >= sh, pltpu.roll(x_re, sh, 0), 0.0)
            s_im = jnp.where(row >= sh, pltpu.roll(x_im, sh, 0), 0.0)
            m_re, m_im = cmul(pw[2 * k], pw[2 * k + 1], s_re, s_im)
            x_re, x_im = x_re + m_re, x_im + m_im
        p_re, p_im = cmul(pw[6], pw[7], c_re, c_im)
        fu_ref[pl.ds(r0, 8), pl.ds(0, half)] = jnp.where(row >= 1, pltpu.roll(x_re, 1, 0), 0.0) + p_re
        fu_ref[pl.ds(r0, 8), pl.ds(half, half)] = jnp.where(row >= 1, pltpu.roll(x_im, 1, 0), 0.0) + p_im
        f_re, f_im = cmul(pw[8], pw[9], c_re, c_im)
        f_re, f_im = f_re + x_re, f_im + x_im
        return jnp.broadcast_to(f_re[7:8], (8, half)), jnp.broadcast_to(f_im[7:8], (8, half))

    zero = jnp.zeros((8, half), F32)
    lax.fori_loop(0, n_chunks // 8, body, (zero, zero))
    y = z[:, :tc * lanes] + _dot(fu_ref[...].astype(BF16), e_ref[...]) + d_ref[...] * x
    y = jax.nn.gelu(y)
    for j in range(tc):
        o_ref[pl.ds(j, n_chunks, stride=tc), :] = y[:, j * lanes:(j + 1) * lanes]


def s5_scan(u, batch, a_re, a_im, log_step, b_re, b_im, c_re, c_im, d_skip):
    rows, width = u.shape
    seq = rows // batch
    tc, lanes = S5_CHUNK, S5_TILE_LANES
    gpt = lanes // SSM_GROUP
    n_tiles = width // lanes
    half = gpt * SSM_STATE
    assert seq % (8 * tc) == 0 and width % lanes == 0
    lam = lax.complex(a_re.astype(F32), a_im.astype(F32))
    lam_dt = lam * jnp.exp(log_step.astype(F32))[:, None]
    b_bar = ((jnp.exp(lam_dt) - 1.0) / lam)[..., None] * lax.complex(b_re.astype(F32), b_im.astype(F32))
    c_mat = lax.complex(c_re.astype(F32), c_im.astype(F32))
    lp = jnp.exp(lam_dt[None] * jnp.arange(tc + 1, dtype=F32)[:, None, None])
    kern = jnp.real(jnp.einsum('gcn,tgn,gnd->tgcd', c_mat, lp[:tc], b_bar))
    kc = kern.transpose(1, 0, 3, 2).reshape(n_tiles, gpt, tc, SSM_GROUP, SSM_GROUP)
    kc = kc.transpose(0, 2, 1, 3, 4).reshape(n_tiles, tc, lanes, SSM_GROUP).astype(BF16)
    f_c = lp[tc - 1 - jnp.arange(tc)][..., None] * b_bar[None]
    f_t = lambda a: (a.transpose(1, 0, 3, 2).reshape(n_tiles, gpt, tc, SSM_GROUP, SSM_STATE)
                     .transpose(0, 2, 1, 3, 4).reshape(n_tiles, tc, lanes, SSM_STATE))
    fc = jnp.concatenate([f_t(jnp.real(f_c)), f_t(jnp.imag(f_c))], axis=-1).astype(BF16)
    e_c = c_mat[None] * lp[1:, :, None, :]
    e_t = lambda a: a.transpose(1, 3, 0, 2).reshape(n_tiles, half, tc * SSM_GROUP)
    ec = jnp.stack([e_t(jnp.real(e_c)), e_t(-jnp.imag(e_c))], axis=1).astype(BF16)
    r8 = jnp.arange(8, dtype=F32)[:, None, None]
    a_pow = lambda e: jnp.exp(lam_dt[None] * (tc * e))
    plist = [a_pow(jnp.full_like(r8, e)) for e in (1.0, 2.0, 4.0)] + [a_pow(r8), a_pow(r8 + 1.0)]
    pws = jnp.stack([f(p) for p in plist for f in (jnp.real, jnp.imag)])
    pws = pws.reshape(10, 8, n_tiles, half).transpose(2, 0, 1, 3)
    dsk = jnp.tile(d_skip.astype(F32).reshape(n_tiles, 1, lanes), (1, 1, tc))
    return pl.pallas_call(
        _s5_kernel,
        out_shape=jax.ShapeDtypeStruct((rows, width), F32),
        grid=(n_tiles, batch),
        in_specs=[pl.BlockSpec((seq, lanes), lambda s, b: (b, s)),
                  pl.BlockSpec((None, tc, lanes, SSM_GROUP), lambda s, b: (s, 0, 0, 0)),
                  pl.BlockSpec((None, tc, lanes, 2 * SSM_STATE), lambda s, b: (s, 0, 0, 0)),
                  pl.BlockSpec((None, 2, half, tc * SSM_GROUP), lambda s, b: (s, 0, 0, 0)),
                  pl.BlockSpec((None, 10, 8, half), lambda s, b: (s, 0, 0, 0)),
                  pl.BlockSpec((None, 1, tc * lanes), lambda s, b: (s, 0, 0))],
        out_specs=pl.BlockSpec((seq, lanes), lambda s, b: (b, s)),
        scratch_shapes=[pltpu.VMEM((seq // tc, 2 * half), F32),
                        pltpu.VMEM((tc * lanes, tc * lanes + 2 * half), BF16),
                        pltpu.VMEM((2 * half, tc * lanes), BF16)],
        compiler_params=_params(("parallel", "arbitrary")),
        name="s5_scan",
    )(u, kc, fc, ec, pws, dsk)


def _glu_kernel(x_ref, wa_ref, wb_ref, o_ref):
    x = x_ref[...].astype(BF16)
    o_ref[...] = (_dot(x, wa_ref[...]) * jax.nn.sigmoid(_dot(x, wb_ref[...]))).astype(o_ref.dtype)


def glu(x, wa, wb, bm=1024, bn=1024):
    m, k = x.shape
    n = wa.shape[1]
    bm, bn = _pick(m, bm), _pick(n, bn)
    return pl.pallas_call(
        _glu_kernel,
        out_shape=jax.ShapeDtypeStruct((m, n), BF16),
        grid=(m // bm, n // bn),
        in_specs=[pl.BlockSpec((bm, k), lambda i, j: (i, 0)),
                  pl.BlockSpec((k, bn), lambda i, j: (0, j)),
                  pl.BlockSpec((k, bn), lambda i, j: (0, j))],
        out_specs=pl.BlockSpec((bm, bn), lambda i, j: (i, j)),
        compiler_params=_params(("parallel", "parallel")),
        name="ssm_glu",
    )(x, wa, wb)


def _attn_kernel(sink_ref, q_ref, kc_ref, kp_ref, vc_ref, vp_ref, qg_ref, kg_ref, o_ref, *, n_kv):
    blk, hd = ATTN_BLOCK, HEAD_DIM
    first = pl.program_id(1) == 0
    n_q = n_kv * Q_PER_KV
    kj = lax.broadcasted_iota(jnp.int32, (2 * blk, blk), 0)
    qi = lax.broadcasted_iota(jnp.int32, (2 * blk, blk), 1)
    dist = qi + blk - kj
    kmin = jnp.where(first, blk, 0)
    valid = (dist >= 0) & (dist < blk) & (kj >= kmin)
    dist_f = dist.astype(F32)

    def head_norm_t(t, heads, gain):
        t3 = t.reshape(heads, hd, t.shape[1])
        ms = jnp.mean(t3 * t3, axis=1, keepdims=True)
        return t3 * lax.rsqrt(ms + EPS) * gain[None]

    qn = (head_norm_t(q_ref[...].astype(F32).T, n_q, qg_ref[...]) * (hd ** -0.5)).astype(BF16)
    kk = jnp.concatenate([kp_ref[...], kc_ref[...]], axis=0).astype(F32)
    kn = head_norm_t(kk.T, n_kv, kg_ref[...]).reshape(n_kv * hd, 2 * blk).T.astype(BF16)
    vt = jnp.concatenate([vp_ref[...], vc_ref[...]], axis=0).astype(F32).T.astype(BF16)

    for h in range(n_kv):
        qt = jnp.concatenate([qn[h * Q_PER_KV + g] for g in range(Q_PER_KV)], axis=1)
        st = _dot(kn[:, h * hd:(h + 1) * hd], qt)
        probs = []
        for g in range(Q_PER_KV):
            head = h * Q_PER_KV + g
            slope = 2.0 ** (-ALIBI_MAX_BIAS * (head + 1) / n_q)
            s = jnp.where(valid, st[:, g * blk:(g + 1) * blk] - slope * dist_f, -jnp.inf)
            sink = sink_ref[head]
            m = jnp.maximum(jnp.max(s, axis=0, keepdims=True), sink)
            p = jnp.exp(s - m)
            denom = jnp.sum(p, axis=0, keepdims=True) + jnp.exp(sink - m)
            probs.append((p * (1.0 / denom)).astype(BF16))
        ot = _dot(vt[h * hd:(h + 1) * hd, :], jnp.concatenate(probs, axis=1))
        ot = jnp.concatenate([ot[:, g * blk:(g + 1) * blk] for g in range(Q_PER_KV)], axis=0)
        o_ref[:, h * Q_PER_KV * hd:(h + 1) * Q_PER_KV * hd] = ot.T.astype(o_ref.dtype)


def swa_attention(qkv, batch, n_q, q_gain, k_gain, sinks):
    rows = qkv.shape[0]
    n_kv = n_q // Q_PER_KV
    qw, kw = n_q * HEAD_DIM, n_kv * HEAD_DIM
    blk = ATTN_BLOCK
    nb = rows // batch // blk
    kcol = qw // kw
    cur = lambda b, j: b * nb + j
    prev = lambda b, j: b * nb + jnp.maximum(j - 1, 0)
    return pl.pallas_call(
        functools.partial(_attn_kernel, n_kv=n_kv),
        out_shape=jax.ShapeDtypeStruct((rows, qw), BF16),
        grid=(batch, nb),
        in_specs=[pl.BlockSpec(memory_space=pltpu.SMEM),
                  pl.BlockSpec((blk, qw), lambda b, j: (cur(b, j), 0)),
                  pl.BlockSpec((blk, kw), lambda b, j: (cur(b, j), kcol)),
                  pl.BlockSpec((blk, kw), lambda b, j: (prev(b, j), kcol)),
                  pl.BlockSpec((blk, kw), lambda b, j: (cur(b, j), kcol + 1)),
                  pl.BlockSpec((blk, kw), lambda b, j: (prev(b, j), kcol + 1)),
                  pl.BlockSpec((HEAD_DIM, blk), lambda b, j: (0, 0)),
                  pl.BlockSpec((HEAD_DIM, 2 * blk), lambda b, j: (0, 0))],
        out_specs=pl.BlockSpec((blk, qw), lambda b, j: (cur(b, j), 0)),
        compiler_params=_params(("parallel", "arbitrary")),
        name="swa_attention",
    )(sinks.astype(F32), qkv, qkv, qkv, qkv, qkv,
      jnp.broadcast_to(q_gain.astype(F32)[:, None], (HEAD_DIM, blk)),
      jnp.broadcast_to(k_gain.astype(F32)[:, None], (HEAD_DIM, 2 * blk)))


def _merge_kernel(ys_ref, ws_ref, ya_ref, wa_ref, gs_ref, ga_ref, o_ref):
    s = jax.nn.sigmoid(gs_ref[...].astype(F32)) * _dot(ys_ref[...], ws_ref[...])
    a = jax.nn.sigmoid(ga_ref[...].astype(F32)) * _dot(ya_ref[...], wa_ref[...])
    o_ref[...] = (s + a).astype(o_ref.dtype)


def branch_merge(y_ssm, w_ssm, y_attn, w_attn, gates, bm=1024, bn=1024):
    m, ks = y_ssm.shape
    ka = y_attn.shape[1]
    n = w_ssm.shape[1]
    bm, bn = _pick(m, bm), _pick(n, bn)
    nj = n // bn
    return pl.pallas_call(
        _merge_kernel,
        out_shape=jax.ShapeDtypeStruct((m, n), BF16),
        grid=(m // bm, nj),
        in_specs=[pl.BlockSpec((bm, ks), lambda i, j: (i, 0)),
                  pl.BlockSpec((ks, bn), lambda i, j: (0, j)),
                  pl.BlockSpec((bm, ka), lambda i, j: (i, 0)),
                  pl.BlockSpec((ka, bn), lambda i, j: (0, j)),
                  pl.BlockSpec((bm, bn), lambda i, j: (i, j)),
                  pl.BlockSpec((bm, bn), lambda i, j: (i, nj + j))],
        out_specs=pl.BlockSpec((bm, bn), lambda i, j: (i, j)),
        compiler_params=_params(("parallel", "parallel")),
        name="branch_merge",
    )(y_ssm, w_ssm, y_attn, w_attn, gates, gates)


def _mm_res_kernel(x_ref, w_ref, r_ref, o_ref):
    o_ref[...] = r_ref[...] + _dot(x_ref[...], w_ref[...])


def matmul_residual(x, w, res, bm=1024, bn=1024):
    m, k = x.shape
    n = w.shape[1]
    bm, bn = _pick(m, bm), _pick(n, bn)
    return pl.pallas_call(
        _mm_res_kernel,
        out_shape=jax.ShapeDtypeStruct((m, n), F32),
        grid=(m // bm, n // bn),
        in_specs=[pl.BlockSpec((bm, k), lambda i, j: (i, 0)),
                  pl.BlockSpec((k, bn), lambda i, j: (0, j)),
                  pl.BlockSpec((bm, bn), lambda i, j: (i, j))],
        out_specs=pl.BlockSpec((bm, bn), lambda i, j: (i, j)),
        compiler_params=_params(("parallel", "parallel")),
        name="out_proj_residual",
    )(x, w, res)


def _router_kernel(h_ref, g_ref, w_ref, b_ref, hm_ref, rows_ref, idx_ref, wt_ref, rank_ref, cnt_ref, seen_ref,
                   *, n_exp):
    @pl.when(pl.program_id(0) == 0)
    def _():
        seen_ref[...] = jnp.zeros_like(seen_ref)

    h = h_ref[...]
    hn = h * lax.rsqrt(jnp.mean(h * h, axis=-1, keepdims=True) + EPS) * g_ref[...]
    hm = hn.astype(BF16)
    hm_ref[...] = hm
    rows_ref[...] = _pack_rows(hn)
    s = jax.nn.sigmoid(_dot(hm, w_ref[...]))
    sel = s + b_ref[...]
    rows = s.shape[0]
    lane_i = lax.broadcasted_iota(jnp.int32, (rows, n_exp), 1)
    per_group = n_exp // N_EXPERT_GROUPS
    lane = lane_i.astype(F32)
    grp = (lane_i // per_group).astype(F32)
    neg = -jnp.inf
    gscore = jnp.zeros_like(sel)
    for g in range(N_EXPERT_GROUPS):
        in_g = grp == g
        v = jnp.where(in_g, sel, neg)
        m1 = jnp.max(v, axis=-1, keepdims=True)
        i1 = jnp.min(jnp.where(v == m1, lane, n_exp), axis=-1, keepdims=True)
        m2 = jnp.max(jnp.where(lane == i1, neg, v), axis=-1, keepdims=True)
        gscore = jnp.where(in_g, m1 + m2, gscore)
    cand = jnp.full_like(sel, neg)
    remaining = gscore
    for _ in range(TOPK_GROUPS):
        gm = jnp.max(remaining, axis=-1, keepdims=True)
        gi = jnp.min(jnp.where(remaining == gm, grp, N_EXPERT_GROUPS), axis=-1, keepdims=True)
        hit = grp == gi
        cand = jnp.where(hit, sel, cand)
        remaining = jnp.where(hit, neg, remaining)
    slot = lax.broadcasted_iota(jnp.int32, (rows, TOP_K), 1)
    idx = jnp.zeros((rows, TOP_K), F32)
    wts = jnp.zeros((rows, TOP_K), F32)
    total = jnp.zeros((rows, 1), F32)
    picked = jnp.zeros_like(sel)
    hits = []
    for k in range(TOP_K):
        mx = jnp.max(cand, axis=-1, keepdims=True)
        ei = jnp.min(jnp.where(cand == mx, lane, n_exp), axis=-1, keepdims=True)
        hit = lane == ei
        wk = jnp.sum(jnp.where(hit, s, 0.0), axis=-1, keepdims=True)
        idx = jnp.where(slot == k, ei, idx)
        wts = jnp.where(slot == k, wk, wts)
        total = total + wk
        cand = jnp.where(hit, neg, cand)
        picked = jnp.where(hit, 1.0, picked)
        hits.append(hit)
    idx_ref[...] = idx.astype(jnp.int32)
    wt_ref[...] = wts / total * ROUTED_SCALE
    r_i = lax.broadcasted_iota(jnp.int32, (rows, rows), 0)
    c_i = lax.broadcasted_iota(jnp.int32, (rows, rows), 1)
    lower = jnp.where(c_i < r_i, 1.0, 0.0).astype(BF16)
    before = _dot(lower, picked.astype(BF16)) + seen_ref[...]
    rank = jnp.zeros((rows, TOP_K), F32)
    for k in range(TOP_K):
        rank = jnp.where(slot == k, jnp.sum(jnp.where(hits[k], before, 0.0), axis=-1, keepdims=True), rank)
    rank_ref[...] = rank.astype(jnp.int32)
    seen = seen_ref[...] + jnp.sum(picked, axis=0, keepdims=True)
    seen_ref[...] = seen
    cnt_ref[...] = seen.astype(jnp.int32)


def norm_and_route(h, gain, w_router, bias, bm=512):
    m, d = h.shape
    n_exp = w_router.shape[1]
    bm = _pick(m, bm)
    packed = jax.eval_shape(_pack_rows, jax.ShapeDtypeStruct((bm, d), F32))
    pw = packed.shape[1]
    tk = lambda dt: jax.ShapeDtypeStruct((m, TOP_K), dt)
    tk_spec = pl.BlockSpec((bm, TOP_K), lambda i: (i, 0))
    return pl.pallas_call(
        functools.partial(_router_kernel, n_exp=n_exp),
        out_shape=(jax.ShapeDtypeStruct((m, d), BF16), jax.ShapeDtypeStruct((m, pw), packed.dtype),
                   tk(jnp.int32), tk(F32), tk(jnp.int32), jax.ShapeDtypeStruct((1, n_exp), jnp.int32)),
        grid=(m // bm,),
        in_specs=[pl.BlockSpec((bm, d), lambda i: (i, 0)),
                  pl.BlockSpec((1, d), lambda i: (0, 0)),
                  pl.BlockSpec((d, n_exp), lambda i: (0, 0)),
                  pl.BlockSpec((1, n_exp), lambda i: (0, 0))],
        out_specs=(pl.BlockSpec((bm, d), lambda i: (i, 0)),
                   pl.BlockSpec((bm, pw), lambda i: (i, 0)),
                   tk_spec, tk_spec, tk_spec, pl.BlockSpec((1, n_exp), lambda i: (0, 0))),
        scratch_shapes=[pltpu.VMEM((1, n_exp), F32)],
        compiler_params=_params(("arbitrary",)),
        name="norm_and_route",
    )(h, gain.reshape(1, d).astype(F32), w_router, bias.reshape(1, n_exp).astype(F32))


def _swiglu(x, wgu, wd):
    ff = wd.shape[0]
    gu = _dot(x, wgu)
    act = (jax.nn.silu(gu[:, :ff]) * gu[:, ff:]).astype(BF16)
    return _dot(act, wd)


def _shared_kernel(x_ref, wgu_ref, wd_ref, r_ref, o_ref):
    o_ref[...] = r_ref[...] + _swiglu(x_ref[...], wgu_ref[...], wd_ref[...])


def shared_expert_residual(hm, wgu, wd, res, bm=256):
    m, d = hm.shape
    ff = wd.shape[0]
    bm = _pick(m, bm)
    return pl.pallas_call(
        _shared_kernel,
        out_shape=jax.ShapeDtypeStruct((m, d), F32),
        grid=(m // bm,),
        in_specs=[pl.BlockSpec((bm, d), lambda i: (i, 0)),
                  pl.BlockSpec((d, 2 * ff), lambda i: (0, 0)),
                  pl.BlockSpec((ff, d), lambda i: (0, 0)),
                  pl.BlockSpec((bm, d), lambda i: (i, 0))],
        out_specs=pl.BlockSpec((bm, d), lambda i: (i, 0)),
        compiler_params=_params(("parallel",)),
        name="shared_expert",
    )(hm, wgu, wd, res)


def _pack_rows(v):
    c = v.shape[1] // 2
    lo = lax.bitcast_convert_type(v[:, :c].astype(BF16).astype(F32), jnp.uint32)
    hi = lax.bitcast_convert_type(v[:, c:].astype(BF16).astype(F32), jnp.uint32)
    return hi | (lo >> 16)


def _unpack_rows(w):
    lo = lax.bitcast_convert_type(w << 16, F32)
    hi = lax.bitcast_convert_type(w & jnp.uint32(0xFFFF0000), F32)
    return lo, hi


def _row_copy(src, src_row, dst, dst_row, sem):
    return pltpu.make_async_copy(src.at[pl.ds(src_row, 1)], dst.at[pl.ds(dst_row, 1)], sem)


def _dispatch_kernel(cnt_ref, pstart_ref, pend_ref, pos_hbm, x_hbm, o_hbm,
                     idx_smem, xbuf, zbuf, idx_sem, load_sem, row_sem):
    i = pl.program_id(0)
    n_steps = pl.num_programs(0)
    n_slots, toks = xbuf.shape[0], xbuf.shape[1]
    per = toks * TOP_K

    def idx_copy(step):
        dst = idx_smem.at[pl.ds(pl.multiple_of((step % 2) * per, per), per)]
        return pltpu.make_async_copy(pos_hbm.at[step], dst, idx_sem.at[step % 2])

    def load(step):
        src = x_hbm.at[pl.ds(pl.multiple_of(step * toks, toks), toks)]
        return pltpu.make_async_copy(src, xbuf.at[step % n_slots], load_sem.at[step % n_slots])

    def wait_rows(step):
        whole = o_hbm.at[pl.ds(0, per)]
        pltpu.make_async_copy(whole, whole, row_sem.at[step % n_slots]).wait()

    @pl.when(i == 0)
    def _():
        zbuf[...] = jnp.zeros_like(zbuf)
        idx_copy(0).start()
        load(0).start()

    @pl.when(i >= 2)
    def _():
        wait_rows(i - 2)

    @pl.when(i + 1 < n_steps)
    def _():
        idx_copy(i + 1).start()
        load(i + 1).start()

    idx_copy(i).wait()
    load(i).wait()
    slot, islot = i % n_slots, i % 2

    def body(g, _):
        t0 = pl.multiple_of(g * 8, 8)
        for tt in range(8):
            for k in range(TOP_K):
                dst = idx_smem[islot * per + t0 * TOP_K + (tt * TOP_K + k)]
                _row_copy(xbuf.at[slot], t0 + tt, o_hbm, dst, row_sem.at[slot]).start(priority=k % 2)
        return 0

    lax.fori_loop(0, toks // 8, body, 0)

    @pl.when(i == n_steps - 1)
    def _():
        @pl.when(i >= 1)
        def _():
            wait_rows(i - 1)

        wait_rows(i)

        def expert_padding(e, _):
            first = pstart_ref[e] + cnt_ref[e]
            n_pad = pend_ref[e] - first

            def zbody(r, _):
                _row_copy(zbuf, 0, o_hbm, first + r, row_sem.at[0]).start()
                return 0

            def zwait(r, _):
                _row_copy(zbuf, 0, o_hbm, first, row_sem.at[0]).wait()
                return 0

            lax.fori_loop(0, n_pad, zbody, 0)
            lax.fori_loop(0, n_pad, zwait, 0)
            return 0

        lax.fori_loop(0, cnt_ref.shape[0], expert_padding, 0)

        blk_rows = zbuf.shape[0]
        n_exp = cnt_ref.shape[0]
        first_blk = pend_ref[n_exp - 1] // blk_rows
        n_tail = o_hbm.shape[0] // blk_rows - first_blk

        def tail_copy(b):
            dst = o_hbm.at[pl.ds(pl.multiple_of((first_blk + b) * blk_rows, blk_rows), blk_rows)]
            return pltpu.make_async_copy(zbuf, dst, row_sem.at[0])

        def tbody(b, _):
            tail_copy(b).start()
            return 0

        def twait(b, _):
            tail_copy(b).wait()
            return 0

        lax.fori_loop(0, n_tail, tbody, 0)
        lax.fori_loop(0, n_tail, twait, 0)


def dispatch_rows(x_rows, pos, counts, pad_start, pad_end, cap, blk_rows, chunk_tokens=512):
    n_tok, width = x_rows.shape
    toks = _pick(n_tok, chunk_tokens)
    steps = n_tok // toks
    n_slots = 3
    grid_spec = pltpu.PrefetchScalarGridSpec(
        num_scalar_prefetch=3,
        grid=(steps,),
        in_specs=[pl.BlockSpec(memory_space=pl.ANY)] * 2,
        out_specs=pl.BlockSpec(memory_space=pl.ANY),
        scratch_shapes=[pltpu.SMEM((2 * toks * TOP_K,), jnp.int32),
                        pltpu.VMEM((n_slots, toks, width), x_rows.dtype),
                        pltpu.VMEM((blk_rows, width), x_rows.dtype),
                        pltpu.SemaphoreType.DMA((2,)),
                        pltpu.SemaphoreType.DMA((n_slots,)),
                        pltpu.SemaphoreType.DMA((n_slots,))],
    )
    return pl.pallas_call(
        _dispatch_kernel,
        out_shape=jax.ShapeDtypeStruct((cap, width), x_rows.dtype),
        grid_spec=grid_spec,
        compiler_params=_params(("arbitrary",)),
        name="moe_dispatch",
    )(counts, pad_start, pad_end, pos.reshape(steps, toks * TOP_K), x_rows)


def _expert_kernel(bexp_ref, nused_ref, next_ref, x_ref, wg_hbm, wu_hbm, wd_hbm, o_ref,
                   sg_ref, su_ref, sd_ref, wgu_ref, wd_ref, sem):
    i = pl.program_id(0)
    e = bexp_ref[i]

    def fetches(ex):
        return [pltpu.make_async_copy(src.at[ex], dst, sem.at[s])
                for s, (src, dst) in enumerate(((wg_hbm, sg_ref), (wu_hbm, su_ref), (wd_hbm, sd_ref)))]

    def to_bf16(src, dst, col0=0):
        chunk = _pick(src.shape[0], 256)

        def body(r, _):
            rows = pl.ds(pl.multiple_of(r * chunk, chunk), chunk)
            dst[rows, pl.ds(col0, src.shape[1])] = src[rows, :].astype(BF16)
            return 0

        lax.fori_loop(0, src.shape[0] // chunk, body, 0)

    @pl.when(i < nused_ref[0])
    def _():
        first_block_of_expert = jnp.logical_or(i == 0, bexp_ref[jnp.maximum(i - 1, 0)] != e)

        @pl.when(first_block_of_expert)
        def _():
            @pl.when(i == 0)
            def _():
                for cp in fetches(e):
                    cp.start()

            for cp in fetches(e):
                cp.wait()
            to_bf16(sg_ref, wgu_ref)
            to_bf16(su_ref, wgu_ref, col0=sg_ref.shape[1])
            to_bf16(sd_ref, wd_ref)

            @pl.when(next_ref[e] >= 0)
            def _():
                for cp in fetches(next_ref[e]):
                    cp.start()

        lo, hi = _unpack_rows(x_ref[...])
        x = jnp.concatenate([lo, hi], axis=1).astype(BF16)
        o_ref[...] = _pack_rows(_swiglu(x, wgu_ref[...], wd_ref[...]))

    @pl.when(i >= nused_ref[0])
    def _():
        o_ref[...] = jnp.zeros_like(o_ref)


def routed_experts_sorted(x_sorted, block_expert, n_used, next_expert, wg, wu, wd, rows):
    cap, width = x_sorted.shape
    nb = cap // rows
    d, ff = wg.shape[1], wg.shape[2]
    used = lambda i, be, nu, nx: (jnp.maximum(jnp.minimum(i, nu[0] - 1), 0), 0)
    grid_spec = pltpu.PrefetchScalarGridSpec(
        num_scalar_prefetch=3,
        grid=(nb,),
        in_specs=[pl.BlockSpec((rows, width), used)] + [pl.BlockSpec(memory_space=pl.ANY)] * 3,
        out_specs=pl.BlockSpec((rows, width), lambda i, be, nu, nx: (i, 0)),
        scratch_shapes=[pltpu.VMEM((d, ff), wg.dtype), pltpu.VMEM((d, ff), wu.dtype), pltpu.VMEM((ff, d), wd.dtype),
                        pltpu.VMEM((d, 2 * ff), BF16), pltpu.VMEM((ff, d), BF16),
                        pltpu.SemaphoreType.DMA((3,))],
    )
    return pl.pallas_call(
        _expert_kernel,
        out_shape=jax.ShapeDtypeStruct((cap, width), x_sorted.dtype),
        grid_spec=grid_spec,
        compiler_params=_params(("arbitrary",), EXPERT_VMEM_LIMIT),
        name="routed_experts",
    )(block_expert, n_used, next_expert, x_sorted, wg, wu, wd)


def _combine_kernel(pos_hbm, y_hbm, base_ref, wt_ref, g_ref, h_ref, hn_ref, idx_smem, gbuf, idx_sem, row_sem):
    i = pl.program_id(0)
    n_steps = pl.num_programs(0)
    toks = gbuf.shape[2]
    per = toks * TOP_K
    grp = COMBINE_GROUP

    def fetch_indices(blk, slot):
        cp = pltpu.make_async_copy(pos_hbm.at[blk], idx_smem.at[pl.ds(pl.multiple_of(slot * per, per), per)], idx_sem)
        cp.start()
        cp.wait()

    def issue_group(slot, g):
        for r in range(g * grp, (g + 1) * grp):
            for k in range(TOP_K):
                _row_copy(y_hbm, idx_smem[slot * per + (r * TOP_K + k)], gbuf.at[slot, k], r,
                          row_sem.at[slot]).start(priority=k % 2)

    def combine_group(slot, g):
        rows = slice(g * grp, (g + 1) * grp)
        wt = wt_ref[rows, :]
        acc_lo, acc_hi = None, None
        for k in range(TOP_K):
            lo, hi = _unpack_rows(gbuf[slot, k, rows, :])
            wk = wt[:, k:k + 1]
            acc_lo = lo * wk if acc_lo is None else acc_lo + lo * wk
            acc_hi = hi * wk if acc_hi is None else acc_hi + hi * wk
        h = base_ref[rows, :] + jnp.concatenate([acc_lo, acc_hi], axis=1)
        h_ref[rows, :] = h
        ms = jnp.mean(h * h, axis=-1, keepdims=True)
        hn_ref[rows, :] = (h * lax.rsqrt(ms + EPS) * g_ref[...]).astype(hn_ref.dtype)

    @pl.when(i == 0)
    def _():
        fetch_indices(0, 0)
        for g in range(toks // grp):
            issue_group(0, g)

    slot = i % 2
    for k in range(TOP_K):
        pltpu.make_async_copy(y_hbm.at[pl.ds(0, toks)], gbuf.at[slot, k], row_sem.at[slot]).wait()

    @pl.when(i + 1 < n_steps)
    def _():
        fetch_indices(i + 1, 1 - slot)
        for g in range(toks // grp):
            issue_group(1 - slot, g)
            combine_group(slot, g)

    @pl.when(i + 1 >= n_steps)
    def _():
        for g in range(toks // grp):
            combine_group(slot, g)


def combine(pos, y_sorted, base, wts, gain, tokens=COMBINE_TOKENS):
    m, d = base.shape
    toks = _pick(m, tokens)
    steps, per = m // toks, toks * TOP_K
    pos = pos.reshape(steps, per)
    return pl.pallas_call(
        _combine_kernel,
        out_shape=(jax.ShapeDtypeStruct((m, d), F32), jax.ShapeDtypeStruct((m, d), BF16)),
        grid=(steps,),
        in_specs=[pl.BlockSpec(memory_space=pl.ANY),
                  pl.BlockSpec(memory_space=pl.ANY),
                  pl.BlockSpec((toks, d), lambda i: (i, 0)),
                  pl.BlockSpec((toks, TOP_K), lambda i: (i, 0)),
                  pl.BlockSpec((1, d), lambda i: (0, 0))],
        out_specs=(pl.BlockSpec((toks, d), lambda i: (i, 0)),
                   pl.BlockSpec((toks, d), lambda i: (i, 0))),
        scratch_shapes=[pltpu.SMEM((2 * per,), jnp.int32),
                        pltpu.VMEM((2, TOP_K, toks, y_sorted.shape[1]), y_sorted.dtype),
                        pltpu.SemaphoreType.DMA(()),
                        pltpu.SemaphoreType.DMA((2,))],
        compiler_params=_params(("arbitrary",)),
        name="moe_combine",
    )(pos, y_sorted, base, wts, gain.reshape(1, d).astype(F32))


def dispatch_tables(eidx, rank, counts, rows):
    n_tok = eidx.shape[0]
    n_exp = counts.shape[0]
    nb = n_tok * TOP_K // rows + n_exp
    pad_end = jnp.cumsum((counts + rows - 1) // rows * rows)
    pad_start = pad_end - (counts + rows - 1) // rows * rows
    experts = jnp.arange(n_exp, dtype=jnp.int32)
    pos = rank + jnp.sum(jnp.where(eidx[..., None] == experts, pad_start, 0), axis=-1)
    blocks = jnp.arange(nb, dtype=jnp.int32)
    block_expert = jnp.minimum(jnp.sum(pad_end[None, :] // rows <= blocks[:, None], axis=-1), n_exp - 1)
    n_used = pad_end[-1:] // rows
    i32 = lambda a: a.astype(jnp.int32)
    later = (experts[None, :] > experts[:, None]) & (counts[None, :] > 0)
    next_expert = jnp.where(jnp.any(later, axis=1), jnp.argmax(later, axis=1), -1)
    return (i32(pos).reshape(-1), i32(block_expert), i32(n_used), i32(next_expert), i32(pad_start), i32(pad_end),
            nb * rows)


def _ple_kernel(hn_ref, wg_ref, p_ref, wp_ref, h_ref, o_ref):
    gate = jax.nn.sigmoid(_dot(hn_ref[...], wg_ref[...]))
    o_ref[...] = h_ref[...] + gate * _dot(p_ref[...].astype(BF16), wp_ref[...])


def ple_gate(hn, w_gate, p, w_ple, h, bm=512, bn=1024):
    m, d = hn.shape
    n = w_gate.shape[1]
    pd = p.shape[1]
    bm, bn = _pick(m, bm), _pick(n, bn)
    return pl.pallas_call(
        _ple_kernel,
        out_shape=jax.ShapeDtypeStruct((m, n), F32),
        grid=(m // bm, n // bn),
        in_specs=[pl.BlockSpec((bm, d), lambda i, j: (i, 0)),
                  pl.BlockSpec((d, bn), lambda i, j: (0, j)),
                  pl.BlockSpec((bm, pd), lambda i, j: (i, 0)),
                  pl.BlockSpec((pd, bn), lambda i, j: (0, j)),
                  pl.BlockSpec((bm, bn), lambda i, j: (i, j))],
        out_specs=pl.BlockSpec((bm, bn), lambda i, j: (i, j)),
        compiler_params=_params(("parallel", "parallel")),
        name="ple_gate",
    )(hn, w_gate, p, w_ple, h)


def _layer(h, p_i, prm, batch):
    (mix_norm, w_in, a_re, a_im, log_step, b_re, b_im, c_re, c_im, ssm_d, w_glu_a, w_glu_b, q_norm, k_norm,
     sinks, w_bs, w_ba, w_out, moe_norm, w_router, router_bias, we_gate, we_up, we_down, ws_gate, ws_up,
     ws_down, ple_norm, w_ple, w_ple_gate) = prm
    ssm_w = w_glu_a.shape[0]
    n_q = sinks.shape[0]
    attn_w = n_q * HEAD_DIM
    kv_w = attn_w // Q_PER_KV
    n_exp = w_router.shape[1]
    c0, c1 = ssm_w, ssm_w + attn_w + 2 * kv_w
    bf = lambda w: w.astype(BF16)

    hn = rmsnorm(h, mix_norm, BF16)
    u = matmul(hn, bf(w_in[:, :c0]), F32, name="proj_u")
    qkv = matmul(hn, bf(w_in[:, c0:c1]), BF16, bn=(c1 - c0) // 2, name="proj_qkv")
    gates = matmul(hn, bf(w_in[:, c1:]), BF16, name="proj_gates")
    y_pre = s5_scan(u, batch, a_re, a_im, log_step, b_re, b_im, c_re, c_im, ssm_d)
    y_ssm = glu(y_pre, bf(w_glu_a), bf(w_glu_b))
    y_attn = swa_attention(qkv, batch, n_q, q_norm, k_norm, sinks)
    merged = branch_merge(y_ssm, bf(w_bs), y_attn, bf(w_ba), gates)
    h = matmul_residual(merged, bf(w_out), h)

    hm, hm_rows, eidx, ew, rank, counts = norm_and_route(h, moe_norm, bf(w_router), router_bias)
    counts = counts.reshape(n_exp)
    pos, block_expert, n_used, next_expert, pad_start, pad_end, cap = dispatch_tables(eidx, rank, counts, MOE_ROWS)
    x_sorted = dispatch_rows(hm_rows, pos, counts, pad_start, pad_end, cap, MOE_ROWS)
    base = shared_expert_residual(hm, jnp.concatenate([bf(ws_gate), bf(ws_up)], axis=1), bf(ws_down), h)
    y_sorted = routed_experts_sorted(x_sorted, block_expert, n_used, next_expert, we_gate, we_up, we_down, MOE_ROWS)
    h, hn3 = combine(pos, y_sorted, base, ew, ple_norm)

    return ple_gate(hn3, bf(w_ple_gate), p_i, bf(w_ple), h)


def kernel(x, p, mix_norm, w_in, ssm_a_re, ssm_a_im, ssm_log_step, ssm_b_re, ssm_b_im, ssm_c_re, ssm_c_im, ssm_d, w_glu_a, w_glu_b, q_norm, k_norm, attn_sinks, w_branch_ssm, w_branch_attn, w_out, moe_norm, w_router, router_bias, we_gate, we_up, we_down, ws_gate, ws_up, ws_down, ple_norm, w_ple, w_ple_gate):
    bsz, seq, d = x.shape
    layer_params = (mix_norm, w_in, ssm_a_re, ssm_a_im, ssm_log_step, ssm_b_re, ssm_b_im, ssm_c_re, ssm_c_im,
                    ssm_d, w_glu_a, w_glu_b, q_norm, k_norm, attn_sinks, w_branch_ssm, w_branch_attn, w_out,
                    moe_norm, w_router, router_bias, we_gate, we_up, we_down, ws_gate, ws_up, ws_down,
                    ple_norm, w_ple, w_ple_gate)
    h = x.reshape(bsz * seq, d)
    for i in range(mix_norm.shape[0]):
        h = _layer(h, p[i].reshape(bsz * seq, -1), tuple(w[i] for w in layer_params), bsz)
    return h.reshape(bsz, seq, d)
```

```python
import functools

import jax
import jax.numpy as jnp
from jax import lax
from jax.experimental import pallas as pl
from jax.experimental.pallas import tpu as pltpu

SSM_GROUP = 16
SSM_STATE = 64
S5_CHUNK = 8
S5_TILE_LANES = 128
HEAD_DIM = 64
Q_PER_KV = 8
ATTN_BLOCK = 128
ALIBI_MAX_BIAS = 8.0
N_EXPERT_GROUPS = 8
TOPK_GROUPS = 4
TOP_K = 8
ROUTED_SCALE = 2.5
EPS = 1e-6
MOE_ROWS = 256
COMBINE_TOKENS = 128
COMBINE_GROUP = 16
V7X_VMEM_LIMIT = 56 * 1024 * 1024
EXPERT_VMEM_LIMIT = 62 * 1024 * 1024

BF16 = jnp.bfloat16
F32 = jnp.float32


def _dot(a, b):
    return jnp.dot(a, b, preferred_element_type=F32)


def _params(sem, vmem=V7X_VMEM_LIMIT):
    return pltpu.CompilerParams(dimension_semantics=sem, vmem_limit_bytes=vmem)


def _pick(n, pref):
    b = min(n, pref)
    while n % b:
        b //= 2
    return b


def _rmsnorm_kernel(x_ref, g_ref, o_ref):
    x = x_ref[...]
    ms = jnp.mean(x * x, axis=-1, keepdims=True)
    o_ref[...] = (x * lax.rsqrt(ms + EPS) * g_ref[...]).astype(o_ref.dtype)


def rmsnorm(x, g, out_dtype, bm=256):
    m, d = x.shape
    bm = _pick(m, bm)
    return pl.pallas_call(
        _rmsnorm_kernel,
        out_shape=jax.ShapeDtypeStruct((m, d), out_dtype),
        grid=(m // bm,),
        in_specs=[pl.BlockSpec((bm, d), lambda i: (i, 0)),
                  pl.BlockSpec((1, d), lambda i: (0, 0))],
        out_specs=pl.BlockSpec((bm, d), lambda i: (i, 0)),
        compiler_params=_params(("parallel",)),
        name="rmsnorm",
    )(x, g.reshape(1, d).astype(F32))


def _mm_kernel(x_ref, w_ref, o_ref):
    o_ref[...] = _dot(x_ref[...], w_ref[...]).astype(o_ref.dtype)


def matmul(x, w, out_dtype, bm=1024, bn=1024, name="matmul"):
    m, k = x.shape
    n = w.shape[1]
    bm, bn = _pick(m, bm), _pick(n, bn)
    return pl.pallas_call(
        _mm_kernel,
        out_shape=jax.ShapeDtypeStruct((m, n), out_dtype),
        grid=(m // bm, n // bn),
        in_specs=[pl.BlockSpec((bm, k), lambda i, j: (i, 0)),
                  pl.BlockSpec((k, bn), lambda i, j: (0, j))],
        out_specs=pl.BlockSpec((bm, bn), lambda i, j: (i, j)),
        compiler_params=_params(("parallel", "parallel")),
        name=name,
    )(x, w)


def _spread_groups(compact, n_out, src_of, group_of, rows_per_group):
    n_in = compact.shape[1]
    k = lax.broadcasted_iota(jnp.int32, (n_in, n_out), 0)
    q = lax.broadcasted_iota(jnp.int32, (n_in, n_out), 1)
    spread = _dot(compact, jnp.where(src_of(q) == k, 1.0, 0.0).astype(BF16))
    r = lax.broadcasted_iota(jnp.int32, spread.shape, 0)
    q = lax.broadcasted_iota(jnp.int32, spread.shape, 1)
    return jnp.where(group_of(q) == r // rows_per_group, spread, 0.0).astype(BF16)


def _s5_chunk_matrices(kc_ref, fc_ref, ec_ref, w_s, e_s):
    tc, lanes, cg, ns = S5_CHUNK, S5_TILE_LANES, SSM_GROUP, SSM_STATE
    half = e_s.shape[0] // 2
    w_s[...] = jnp.zeros_like(w_s)
    for tau in range(tc):
        blk = _spread_groups(kc_ref[tau], lanes, lambda q: q % cg, lambda q: q // cg, cg)
        for i in range(tc - tau):
            w_s[i * lanes:(i + 1) * lanes, (i + tau) * lanes:(i + tau + 1) * lanes] = blk
    for i in range(tc):
        f = fc_ref[i]
        rows = slice(i * lanes, (i + 1) * lanes)
        w_s[rows, tc * lanes:tc * lanes + half] = _spread_groups(f, half, lambda q: q % ns, lambda q: q // ns, cg)
        w_s[rows, tc * lanes + half:] = _spread_groups(f, half, lambda q: q % ns + ns, lambda q: q // ns, cg)
    for part in range(2):
        e_s[part * half:(part + 1) * half, :] = _spread_groups(
            ec_ref[part], tc * lanes, lambda q: (q // lanes) * cg + q % cg, lambda q: (q // cg) % (lanes // cg), ns)


def _s5_kernel(u_ref, kc_ref, fc_ref, ec_ref, pw_ref, d_ref, o_ref, fu_ref, w_ref, e_ref):
    @pl.when(pl.program_id(1) == 0)
    def _():
        _s5_chunk_matrices(kc_ref, fc_ref, ec_ref, w_ref, e_ref)

    tc = S5_CHUNK
    n_chunks = u_ref.shape[0] // tc
    lanes = u_ref.shape[1]
    half = fu_ref.shape[1] // 2
    x = jnp.concatenate([u_ref[pl.ds(i, n_chunks, stride=tc), :] for i in range(tc)], axis=1)
    z = _dot(x.astype(BF16), w_ref[...])
    fu_ref[...] = z[:, tc * lanes:]
    row = lax.broadcasted_iota(jnp.int32, (8, half), 0)
    pw = [pw_ref[k] for k in range(10)]

    def cmul(ar, ai, xr, xi):
        return ar * xr - ai * xi, ar * xi + ai * xr

    def body(t, carry):
        c_re, c_im = carry
        r0 = pl.multiple_of(t * 8, 8)
        x_re = fu_ref[pl.ds(r0, 8), pl.ds(0, half)]
        x_im = fu_ref[pl.ds(r0, 8), pl.ds(half, half)]
        for k, sh in enumerate((1, 2, 4)):
            s_re = jnp.where(row >= sh, pltpu.roll(x_re, sh, 0), 0.0)
            s_im = jnp.where(row >= sh, pltpu.roll(x_im, sh, 0), 0.0)
            m_re, m_im = cmul(pw[2 * k], pw[2 * k + 1], s_re, s_im)
            x_re, x_im = x_re + m_re, x_im + m_im
        p_re, p_im = cmul(pw[6], pw[7], c_re, c_im)
        fu_ref[pl.ds(r0, 8), pl.ds(0, half)] = jnp.where(row >= 1, pltpu.roll(x_re, 1, 0), 0.0) + p_re
        fu_ref[pl.ds(r0, 8), pl.ds(half, half)] = jnp.where(row >= 1, pltpu.roll(x_im, 1, 0), 0.0) + p_im
        f_re, f_im = cmul(pw[8], pw[9], c_re, c_im)
        f_re, f_im = f_re + x_re, f_im + x_im
        return jnp.broadcast_to(f_re[7:8], (8, half)), jnp.broadcast_to(f_im[7:8], (8, half))

    zero = jnp.zeros((8, half), F32)
    lax.fori_loop(0, n_chunks // 8, body, (zero, zero))
    y = z[:, :tc * lanes] + _dot(fu_ref[...].astype(BF16), e_ref[...]) + d_ref[...] * x
    y = jax.nn.gelu(y)
    for j in range(tc):
        o_ref[pl.ds(j, n_chunks, stride=tc), :] = y[:, j * lanes:(j + 1) * lanes]


def s5_scan(u, batch, a_re, a_im, log_step, b_re, b_im, c_re, c_im, d_skip):
    rows, width = u.shape
    seq = rows // batch
    tc, lanes = S5_CHUNK, S5_TILE_LANES
    gpt = lanes // SSM_GROUP
    n_tiles = width // lanes
    half = gpt * SSM_STATE
    assert seq % (8 * tc) == 0 and width % lanes == 0
    lam = lax.complex(a_re.astype(F32), a_im.astype(F32))
    lam_dt = lam * jnp.exp(log_step.astype(F32))[:, None]
    b_bar = ((jnp.exp(lam_dt) - 1.0) / lam)[..., None] * lax.complex(b_re.astype(F32), b_im.astype(F32))
    c_mat = lax.complex(c_re.astype(F32), c_im.astype(F32))
    lp = jnp.exp(lam_dt[None] * jnp.arange(tc + 1, dtype=F32)[:, None, None])
    kern = jnp.real(jnp.einsum('gcn,tgn,gnd->tgcd', c_mat, lp[:tc], b_bar))
    kc = kern.transpose(1, 0, 3, 2).reshape(n_tiles, gpt, tc, SSM_GROUP, SSM_GROUP)
    kc = kc.transpose(0, 2, 1, 3, 4).reshape(n_tiles, tc, lanes, SSM_GROUP).astype(BF16)
    f_c = lp[tc - 1 - jnp.arange(tc)][..., None] * b_bar[None]
    f_t = lambda a: (a.transpose(1, 0, 3, 2).reshape(n_tiles, gpt, tc, SSM_GROUP, SSM_STATE)
                     .transpose(0, 2, 1, 3, 4).reshape(n_tiles, tc, lanes, SSM_STATE))
    fc = jnp.concatenate([f_t(jnp.real(f_c)), f_t(jnp.imag(f_c))], axis=-1).astype(BF16)
    e_c = c_mat[None] * lp[1:, :, None, :]
    e_t = lambda a: a.transpose(1, 3, 0, 2).reshape(n_tiles, half, tc * SSM_GROUP)
    ec = jnp.stack([e_t(jnp.real(e_c)), e_t(-jnp.imag(e_c))], axis=1).astype(BF16)
    r8 = jnp.arange(8, dtype=F32)[:, None, None]
    a_pow = lambda e: jnp.exp(lam_dt[None] * (tc * e))
    plist = [a_pow(jnp.full_like(r8, e)) for e in (1.0, 2.0, 4.0)] + [a_pow(r8), a_pow(r8 + 1.0)]
    pws = jnp.stack([f(p) for p in plist for f in (jnp.real, jnp.imag)])
    pws = pws.reshape(10, 8, n_tiles, half).transpose(2, 0, 1, 3)
    dsk = jnp.tile(d_skip.astype(F32).reshape(n_tiles, 1, lanes), (1, 1, tc))
    return pl.pallas_call(
        _s5_kernel,
        out_shape=jax.ShapeDtypeStruct((rows, width), F32),
        grid=(n_tiles, batch),
        in_specs=[pl.BlockSpec((seq, lanes), lambda s, b: (b, s)),
                  pl.BlockSpec((None, tc, lanes, SSM_GROUP), lambda s, b: (s, 0, 0, 0)),
                  pl.BlockSpec((None, tc, lanes, 2 * SSM_STATE), lambda s, b: (s, 0, 0, 0)),
                  pl.BlockSpec((None, 2, half, tc * SSM_GROUP), lambda s, b: (s, 0, 0, 0)),
                  pl.BlockSpec((None, 10, 8, half), lambda s, b: (s, 0, 0, 0)),
                  pl.BlockSpec((None, 1, tc * lanes), lambda s, b: (s, 0, 0))],
        out_specs=pl.BlockSpec((seq, lanes), lambda s, b: (b, s)),
        scratch_shapes=[pltpu.VMEM((seq // tc, 2 * half), F32),
                        pltpu.VMEM((tc * lanes, tc * lanes + 2 * half), BF16),
                        pltpu.VMEM((2 * half, tc * lanes), BF16)],
        compiler_params=_params(("parallel", "arbitrary")),
        name="s5_scan",
    )(u, kc, fc, ec, pws, dsk)


def _glu_kernel(x_ref, wa_ref, wb_ref, o_ref):
    x = x_ref[...].astype(BF16)
    o_ref[...] = (_dot(x, wa_ref[...]) * jax.nn.sigmoid(_dot(x, wb_ref[...]))).astype(o_ref.dtype)


def glu(x, wa, wb, bm=1024, bn=1024):
    m, k = x.shape
    n = wa.shape[1]
    bm, bn = _pick(m, bm), _pick(n, bn)
    return pl.pallas_call(
        _glu_kernel,
        out_shape=jax.ShapeDtypeStruct((m, n), BF16),
        grid=(m // bm, n // bn),
        in_specs=[pl.BlockSpec((bm, k), lambda i, j: (i, 0)),
                  pl.BlockSpec((k, bn), lambda i, j: (0, j)),
                  pl.BlockSpec((k, bn), lambda i, j: (0, j))],
        out_specs=pl.BlockSpec((bm, bn), lambda i, j: (i, j)),
        compiler_params=_params(("parallel", "parallel")),
        name="ssm_glu",
    )(x, wa, wb)


def _attn_kernel(sink_ref, q_ref, kc_ref, kp_ref, vc_ref, vp_ref, qg_ref, kg_ref, o_ref, *, n_kv):
    blk, hd = ATTN_BLOCK, HEAD_DIM
    first = pl.program_id(1) == 0
    n_q = n_kv * Q_PER_KV
    kj = lax.broadcasted_iota(jnp.int32, (2 * blk, blk), 0)
    qi = lax.broadcasted_iota(jnp.int32, (2 * blk, blk), 1)
    dist = qi + blk - kj
    kmin = jnp.where(first, blk, 0)
    valid = (dist >= 0) & (dist < blk) & (kj >= kmin)
    dist_f = dist.astype(F32)

    def head_norm_t(t, heads, gain):
        t3 = t.reshape(heads, hd, t.shape[1])
        ms = jnp.mean(t3 * t3, axis=1, keepdims=True)
        return t3 * lax.rsqrt(ms + EPS) * gain[None]

    qn = (head_norm_t(q_ref[...].astype(F32).T, n_q, qg_ref[...]) * (hd ** -0.5)).astype(BF16)
    kk = jnp.concatenate([kp_ref[...], kc_ref[...]], axis=0).astype(F32)
    kn = head_norm_t(kk.T, n_kv, kg_ref[...]).reshape(n_kv * hd, 2 * blk).T.astype(BF16)
    vt = jnp.concatenate([vp_ref[...], vc_ref[...]], axis=0).astype(F32).T.astype(BF16)

    for h in range(n_kv):
        qt = jnp.concatenate([qn[h * Q_PER_KV + g] for g in range(Q_PER_KV)], axis=1)
        st = _dot(kn[:, h * hd:(h + 1) * hd], qt)
        probs = []
        for g in range(Q_PER_KV):
            head = h * Q_PER_KV + g
            slope = 2.0 ** (-ALIBI_MAX_BIAS * (head + 1) / n_q)
            s = jnp.where(valid, st[:, g * blk:(g + 1) * blk] - slope * dist_f, -jnp.inf)
            sink = sink_ref[head]
            m = jnp.maximum(jnp.max(s, axis=0, keepdims=True), sink)
            p = jnp.exp(s - m)
            denom = jnp.sum(p, axis=0, keepdims=True) + jnp.exp(sink - m)
            probs.append((p * (1.0 / denom)).astype(BF16))
        ot = _dot(vt[h * hd:(h + 1) * hd, :], jnp.concatenate(probs, axis=1))
        ot = jnp.concatenate([ot[:, g * blk:(g + 1) * blk] for g in range(Q_PER_KV)], axis=0)
        o_ref[:, h * Q_PER_KV * hd:(h + 1) * Q_PER_KV * hd] = ot.T.astype(o_ref.dtype)


def swa_attention(qkv, batch, n_q, q_gain, k_gain, sinks):
    rows = qkv.shape[0]
    n_kv = n_q // Q_PER_KV
    qw, kw = n_q * HEAD_DIM, n_kv * HEAD_DIM
    blk = ATTN_BLOCK
    nb = rows // batch // blk
    kcol = qw // kw
    cur = lambda b, j: b * nb + j
    prev = lambda b, j: b * nb + jnp.maximum(j - 1, 0)
    return pl.pallas_call(
        functools.partial(_attn_kernel, n_kv=n_kv),
        out_shape=jax.ShapeDtypeStruct((rows, qw), BF16),
        grid=(batch, nb),
        in_specs=[pl.BlockSpec(memory_space=pltpu.SMEM),
                  pl.BlockSpec((blk, qw), lambda b, j: (cur(b, j), 0)),
                  pl.BlockSpec((blk, kw), lambda b, j: (cur(b, j), kcol)),
                  pl.BlockSpec((blk, kw), lambda b, j: (prev(b, j), kcol)),
                  pl.BlockSpec((blk, kw), lambda b, j: (cur(b, j), kcol + 1)),
                  pl.BlockSpec((blk, kw), lambda b, j: (prev(b, j), kcol + 1)),
                  pl.BlockSpec((HEAD_DIM, blk), lambda b, j: (0, 0)),
                  pl.BlockSpec((HEAD_DIM, 2 * blk), lambda b, j: (0, 0))],
        out_specs=pl.BlockSpec((blk, qw), lambda b, j: (cur(b, j), 0)),
        compiler_params=_params(("parallel", "arbitrary")),
        name="swa_attention",
    )(sinks.astype(F32), qkv, qkv, qkv, qkv, qkv,
      jnp.broadcast_to(q_gain.astype(F32)[:, None], (HEAD_DIM, blk)),
      jnp.broadcast_to(k_gain.astype(F32)[:, None], (HEAD_DIM, 2 * blk)))


def _merge_kernel(ys_ref, ws_ref, ya_ref, wa_ref, gs_ref, ga_ref, o_ref):
    s = jax.nn.sigmoid(gs_ref[...].astype(F32)) * _dot(ys_ref[...], ws_ref[...])
    a = jax.nn.sigmoid(ga_ref[...].astype(F32)) * _dot(ya_ref[...], wa_ref[...])
    o_ref[...] = (s + a).astype(o_ref.dtype)


def branch_merge(y_ssm, w_ssm, y_attn, w_attn, gates, bm=1024, bn=1024):
    m, ks = y_ssm.shape
    ka = y_attn.shape[1]
    n = w_ssm.shape[1]
    bm, bn = _pick(m, bm), _pick(n, bn)
    nj = n // bn
    return pl.pallas_call(
        _merge_kernel,
        out_shape=jax.ShapeDtypeStruct((m, n), BF16),
        grid=(m // bm, nj),
        in_specs=[pl.BlockSpec((bm, ks), lambda i, j: (i, 0)),
                  pl.BlockSpec((ks, bn), lambda i, j: (0, j)),
                  pl.BlockSpec((bm, ka), lambda i, j: (i, 0)),
                  pl.BlockSpec((ka, bn), lambda i, j: (0, j)),
                  pl.BlockSpec((bm, bn), lambda i, j: (i, j)),
                  pl.BlockSpec((bm, bn), lambda i, j: (i, nj + j))],
        out_specs=pl.BlockSpec((bm, bn), lambda i, j: (i, j)),
        compiler_params=_params(("parallel", "parallel")),
        name="branch_merge",
    )(y_ssm, w_ssm, y_attn, w_attn, gates, gates)


def _mm_res_kernel(x_ref, w_ref, r_ref, o_ref):
    o_ref[...] = r_ref[...] + _dot(x_ref[...], w_ref[...])


def matmul_residual(x, w, res, bm=1024, bn=1024):
    m, k = x.shape
    n = w.shape[1]
    bm, bn = _pick(m, bm), _pick(n, bn)
    return pl.pallas_call(
        _mm_res_kernel,
        out_shape=jax.ShapeDtypeStruct((m, n), F32),
        grid=(m // bm, n // bn),
        in_specs=[pl.BlockSpec((bm, k), lambda i, j: (i, 0)),
                  pl.BlockSpec((k, bn), lambda i, j: (0, j)),
                  pl.BlockSpec((bm, bn), lambda i, j: (i, j))],
        out_specs=pl.BlockSpec((bm, bn), lambda i, j: (i, j)),
        compiler_params=_params(("parallel", "parallel")),
        name="out_proj_residual",
    )(x, w, res)


def _router_kernel(h_ref, g_ref, w_ref, b_ref, hm_ref, rows_ref, idx_ref, wt_ref, rank_ref, cnt_ref, seen_ref,
                   *, n_exp):
    @pl.when(pl.program_id(0) == 0)
    def _():
        seen_ref[...] = jnp.zeros_like(seen_ref)

    h = h_ref[...]
    hn = h * lax.rsqrt(jnp.mean(h * h, axis=-1, keepdims=True) + EPS) * g_ref[...]
    hm = hn.astype(BF16)
    hm_ref[...] = hm
    rows_ref[...] = _pack_rows(hn)
    s = jax.nn.sigmoid(_dot(hm, w_ref[...]))
    sel = s + b_ref[...]
    rows = s.shape[0]
    lane_i = lax.broadcasted_iota(jnp.int32, (rows, n_exp), 1)
    per_group = n_exp // N_EXPERT_GROUPS
    lane = lane_i.astype(F32)
    grp = (lane_i // per_group).astype(F32)
    neg = -jnp.inf
    gscore = jnp.zeros_like(sel)
    for g in range(N_EXPERT_GROUPS):
        in_g = grp == g
        v = jnp.where(in_g, sel, neg)
        m1 = jnp.max(v, axis=-1, keepdims=True)
        i1 = jnp.min(jnp.where(v == m1, lane, n_exp), axis=-1, keepdims=True)
        m2 = jnp.max(jnp.where(lane == i1, neg, v), axis=-1, keepdims=True)
        gscore = jnp.where(in_g, m1 + m2, gscore)
    cand = jnp.full_like(sel, neg)
    remaining = gscore
    for _ in range(TOPK_GROUPS):
        gm = jnp.max(remaining, axis=-1, keepdims=True)
        gi = jnp.min(jnp.where(remaining == gm, grp, N_EXPERT_GROUPS), axis=-1, keepdims=True)
        hit = grp == gi
        cand = jnp.where(hit, sel, cand)
        remaining = jnp.where(hit, neg, remaining)
    slot = lax.broadcasted_iota(jnp.int32, (rows, TOP_K), 1)
    idx = jnp.zeros((rows, TOP_K), F32)
    wts = jnp.zeros((rows, TOP_K), F32)
    total = jnp.zeros((rows, 1), F32)
    picked = jnp.zeros_like(sel)
    hits = []
    for k in range(TOP_K):
        mx = jnp.max(cand, axis=-1, keepdims=True)
        ei = jnp.min(jnp.where(cand == mx, lane, n_exp), axis=-1, keepdims=True)
        hit = lane == ei
        wk = jnp.sum(jnp.where(hit, s, 0.0), axis=-1, keepdims=True)
        idx = jnp.where(slot == k, ei, idx)
        wts = jnp.where(slot == k, wk, wts)
        total = total + wk
        cand = jnp.where(hit, neg, cand)
        picked = jnp.where(hit, 1.0, picked)
        hits.append(hit)
    idx_ref[...] = idx.astype(jnp.int32)
    wt_ref[...] = wts / total * ROUTED_SCALE
    r_i = lax.broadcasted_iota(jnp.int32, (rows, rows), 0)
    c_i = lax.broadcasted_iota(jnp.int32, (rows, rows), 1)
    lower = jnp.where(c_i < r_i, 1.0, 0.0).astype(BF16)
    before = _dot(lower, picked.astype(BF16)) + seen_ref[...]
    rank = jnp.zeros((rows, TOP_K), F32)
    for k in range(TOP_K):
        rank = jnp.where(slot == k, jnp.sum(jnp.where(hits[k], before, 0.0), axis=-1, keepdims=True), rank)
    rank_ref[...] = rank.astype(jnp.int32)
    seen = seen_ref[...] + jnp.sum(picked, axis=0, keepdims=True)
    seen_ref[...] = seen
    cnt_ref[...] = seen.astype(jnp.int32)


def norm_and_route(h, gain, w_router, bias, bm=512):
    m, d = h.shape
    n_exp = w_router.shape[1]
    bm = _pick(m, bm)
    packed = jax.eval_shape(_pack_rows, jax.ShapeDtypeStruct((bm, d), F32))
    pw = packed.shape[1]
    tk = lambda dt: jax.ShapeDtypeStruct((m, TOP_K), dt)
    tk_spec = pl.BlockSpec((bm, TOP_K), lambda i: (i, 0))
    return pl.pallas_call(
        functools.partial(_router_kernel, n_exp=n_exp),
        out_shape=(jax.ShapeDtypeStruct((m, d), BF16), jax.ShapeDtypeStruct((m, pw), packed.dtype),
                   tk(jnp.int32), tk(F32), tk(jnp.int32), jax.ShapeDtypeStruct((1, n_exp), jnp.int32)),
        grid=(m // bm,),
        in_specs=[pl.BlockSpec((bm, d), lambda i: (i, 0)),
                  pl.BlockSpec((1, d), lambda i: (0, 0)),
                  pl.BlockSpec((d, n_exp), lambda i: (0, 0)),
                  pl.BlockSpec((1, n_exp), lambda i: (0, 0))],
        out_specs=(pl.BlockSpec((bm, d), lambda i: (i, 0)),
                   pl.BlockSpec((bm, pw), lambda i: (i, 0)),
                   tk_spec, tk_spec, tk_spec, pl.BlockSpec((1, n_exp), lambda i: (0, 0))),
        scratch_shapes=[pltpu.VMEM((1, n_exp), F32)],
        compiler_params=_params(("arbitrary",)),
        name="norm_and_route",
    )(h, gain.reshape(1, d).astype(F32), w_router, bias.reshape(1, n_exp).astype(F32))


def _swiglu(x, wgu, wd):
    ff = wd.shape[0]
    gu = _dot(x, wgu)
    act = (jax.nn.silu(gu[:, :ff]) * gu[:, ff:]).astype(BF16)
    return _dot(act, wd)


def _pack_rows(v):
    c = v.shape[1] // 2
    lo = lax.bitcast_convert_type(v[:, :c].astype(BF16).astype(F32), jnp.uint32)
    hi = lax.bitcast_convert_type(v[:, c:].astype(BF16).astype(F32), jnp.uint32)
    return hi | (lo >> 16)


def _unpack_rows(w):
    lo = lax.bitcast_convert_type(w << 16, F32)
    hi = lax.bitcast_convert_type(w & jnp.uint32(0xFFFF0000), F32)
    return lo, hi


def _row_copy(src, src_row, dst, dst_row, sem):
    return pltpu.make_async_copy(src.at[pl.ds(src_row, 1)], dst.at[pl.ds(dst_row, 1)], sem)


def _dispatch_shared_kernel(cnt_ref, pstart_ref, pend_ref, pos_hbm, x_hbm, wgu_hbm, wd_hbm, hm_ref, res_ref,
                            base_ref, o_hbm, idx_smem, xbuf, zbuf, wgu_ref, wd_ref, idx_sem, load_sem, row_sem, w_sem):
    i = pl.program_id(0)
    weight_copies = [pltpu.make_async_copy(wgu_hbm, wgu_ref, w_sem.at[0]),
                     pltpu.make_async_copy(wd_hbm, wd_ref, w_sem.at[1])]

    @pl.when(i == 0)
    def _():
        for cp in weight_copies:
            cp.start()

    n_steps = pl.num_programs(0)
    n_slots, toks = xbuf.shape[0], xbuf.shape[1]
    per = toks * TOP_K

    def idx_copy(step):
        dst = idx_smem.at[pl.ds(pl.multiple_of((step % 2) * per, per), per)]
        return pltpu.make_async_copy(pos_hbm.at[step], dst, idx_sem.at[step % 2])

    def load(step):
        src = x_hbm.at[pl.ds(pl.multiple_of(step * toks, toks), toks)]
        return pltpu.make_async_copy(src, xbuf.at[step % n_slots], load_sem.at[step % n_slots])

    def wait_rows(step):
        whole = o_hbm.at[pl.ds(0, per)]
        pltpu.make_async_copy(whole, whole, row_sem.at[step % n_slots]).wait()

    @pl.when(i == 0)
    def _():
        zbuf[...] = jnp.zeros_like(zbuf)
        idx_copy(0).start()
        load(0).start()

    @pl.when(i >= 2)
    def _():
        wait_rows(i - 2)

    @pl.when(i + 1 < n_steps)
    def _():
        idx_copy(i + 1).start()
        load(i + 1).start()

    idx_copy(i).wait()
    load(i).wait()
    slot, islot = i % n_slots, i % 2

    def body(g, _):
        t0 = pl.multiple_of(g * 8, 8)
        for tt in range(8):
            for k in range(TOP_K):
                dst = idx_smem[islot * per + t0 * TOP_K + (tt * TOP_K + k)]
                _row_copy(xbuf.at[slot], t0 + tt, o_hbm, dst, row_sem.at[slot]).start(priority=k % 2)
        return 0

    lax.fori_loop(0, toks // 8, body, 0)

    @pl.when(i == n_steps - 1)
    def _():
        @pl.when(i >= 1)
        def _():
            wait_rows(i - 1)

        wait_rows(i)

        def expert_padding(e, _):
            first = pstart_ref[e] + cnt_ref[e]
            n_pad = pend_ref[e] - first

            def zbody(r, _):
                _row_copy(zbuf, 0, o_hbm, first + r, row_sem.at[0]).start()
                return 0

            def zwait(r, _):
                _row_copy(zbuf, 0, o_hbm, first, row_sem.at[0]).wait()
                return 0

            lax.fori_loop(0, n_pad, zbody, 0)
            lax.fori_loop(0, n_pad, zwait, 0)
            return 0

        lax.fori_loop(0, cnt_ref.shape[0], expert_padding, 0)

        blk_rows = zbuf.shape[0]
        n_exp = cnt_ref.shape[0]
        first_blk = pend_ref[n_exp - 1] // blk_rows
        n_tail = o_hbm.shape[0] // blk_rows - first_blk

        def tail_copy(b):
            dst = o_hbm.at[pl.ds(pl.multiple_of((first_blk + b) * blk_rows, blk_rows), blk_rows)]
            return pltpu.make_async_copy(zbuf, dst, row_sem.at[0])

        def tbody(b, _):
            tail_copy(b).start()
            return 0

        def twait(b, _):
            tail_copy(b).wait()
            return 0

        lax.fori_loop(0, n_tail, tbody, 0)
        lax.fori_loop(0, n_tail, twait, 0)

    @pl.when(i == 0)
    def _():
        for cp in weight_copies:
            cp.wait()

    base_ref[...] = res_ref[...] + _swiglu(hm_ref[...], wgu_ref[...], wd_ref[...])


def dispatch_and_shared_expert(x_rows, pos, counts, pad_start, pad_end, cap, blk_rows, hm, wgu, wd, res, tokens=256):
    n_tok, width = x_rows.shape
    d = hm.shape[1]
    toks = _pick(n_tok, tokens)
    steps = n_tok // toks
    n_slots = 3
    row_block = pl.BlockSpec((toks, d), lambda i, c, s, e: (i, 0))
    grid_spec = pltpu.PrefetchScalarGridSpec(
        num_scalar_prefetch=3,
        grid=(steps,),
        in_specs=[pl.BlockSpec(memory_space=pl.ANY)] * 4 + [row_block, row_block],
        out_specs=(row_block, pl.BlockSpec(memory_space=pl.ANY)),
        scratch_shapes=[pltpu.SMEM((2 * toks * TOP_K,), jnp.int32),
                        pltpu.VMEM((n_slots, toks, width), x_rows.dtype),
                        pltpu.VMEM((blk_rows, width), x_rows.dtype),
                        pltpu.VMEM(wgu.shape, wgu.dtype),
                        pltpu.VMEM(wd.shape, wd.dtype),
                        pltpu.SemaphoreType.DMA((2,)),
                        pltpu.SemaphoreType.DMA((n_slots,)),
                        pltpu.SemaphoreType.DMA((n_slots,)),
                        pltpu.SemaphoreType.DMA((2,))],
    )
    base, x_sorted = pl.pallas_call(
        _dispatch_shared_kernel,
        out_shape=(jax.ShapeDtypeStruct((n_tok, d), F32), jax.ShapeDtypeStruct((cap, width), x_rows.dtype)),
        grid_spec=grid_spec,
        compiler_params=_params(("arbitrary",)),
        name="moe_dispatch_shared",
    )(counts, pad_start, pad_end, pos.reshape(steps, toks * TOP_K), x_rows, wgu, wd, hm, res)
    return x_sorted, base


def _expert_kernel(bexp_ref, nused_ref, next_ref, x_ref, wg_hbm, wu_hbm, wd_hbm, o_ref,
                   sg_ref, su_ref, sd_ref, wgu_ref, wd_ref, sem):
    i = pl.program_id(0)
    e = bexp_ref[i]

    def fetches(ex):
        return [pltpu.make_async_copy(src.at[ex], dst, sem.at[s])
                for s, (src, dst) in enumerate(((wg_hbm, sg_ref), (wu_hbm, su_ref), (wd_hbm, sd_ref)))]

    def to_bf16(src, dst, col0=0):
        chunk = _pick(src.shape[0], 256)

        def body(r, _):
            rows = pl.ds(pl.multiple_of(r * chunk, chunk), chunk)
            dst[rows, pl.ds(col0, src.shape[1])] = src[rows, :].astype(BF16)
            return 0

        lax.fori_loop(0, src.shape[0] // chunk, body, 0)

    @pl.when(i < nused_ref[0])
    def _():
        first_block_of_expert = jnp.logical_or(i == 0, bexp_ref[jnp.maximum(i - 1, 0)] != e)

        @pl.when(first_block_of_expert)
        def _():
            @pl.when(i == 0)
            def _():
                for cp in fetches(e):
                    cp.start()

            for cp in fetches(e):
                cp.wait()
            to_bf16(sg_ref, wgu_ref)
            to_bf16(su_ref, wgu_ref, col0=sg_ref.shape[1])
            to_bf16(sd_ref, wd_ref)

            @pl.when(next_ref[e] >= 0)
            def _():
                for cp in fetches(next_ref[e]):
                    cp.start()

        lo, hi = _unpack_rows(x_ref[...])
        x = jnp.concatenate([lo, hi], axis=1).astype(BF16)
        o_ref[...] = _pack_rows(_swiglu(x, wgu_ref[...], wd_ref[...]))

    @pl.when(i >= nused_ref[0])
    def _():
        o_ref[...] = jnp.zeros_like(o_ref)


def routed_experts_sorted(x_sorted, block_expert, n_used, next_expert, wg, wu, wd, rows):
    cap, width = x_sorted.shape
    nb = cap // rows
    d, ff = wg.shape[1], wg.shape[2]
    used = lambda i, be, nu, nx: (jnp.maximum(jnp.minimum(i, nu[0] - 1), 0), 0)
    grid_spec = pltpu.PrefetchScalarGridSpec(
        num_scalar_prefetch=3,
        grid=(nb,),
        in_specs=[pl.BlockSpec((rows, width), used)] + [pl.BlockSpec(memory_space=pl.ANY)] * 3,
        out_specs=pl.BlockSpec((rows, width), lambda i, be, nu, nx: (i, 0)),
        scratch_shapes=[pltpu.VMEM((d, ff), wg.dtype), pltpu.VMEM((d, ff), wu.dtype), pltpu.VMEM((ff, d), wd.dtype),
                        pltpu.VMEM((d, 2 * ff), BF16), pltpu.VMEM((ff, d), BF16),
                        pltpu.SemaphoreType.DMA((3,))],
    )
    return pl.pallas_call(
        _expert_kernel,
        out_shape=jax.ShapeDtypeStruct((cap, width), x_sorted.dtype),
        grid_spec=grid_spec,
        compiler_params=_params(("arbitrary",), EXPERT_VMEM_LIMIT),
        name="routed_experts",
    )(block_expert, n_used, next_expert, x_sorted, wg, wu, wd)


def _combine_kernel(pos_hbm, y_hbm, base_ref, wt_ref, g_ref, h_ref, hn_ref, idx_smem, gbuf, idx_sem, row_sem):
    i = pl.program_id(0)
    n_steps = pl.num_programs(0)
    toks = gbuf.shape[2]
    per = toks * TOP_K
    grp = COMBINE_GROUP

    def fetch_indices(blk, slot):
        cp = pltpu.make_async_copy(pos_hbm.at[blk], idx_smem.at[pl.ds(pl.multiple_of(slot * per, per), per)], idx_sem)
        cp.start()
        cp.wait()

    def issue_group(slot, g):
        for r in range(g * grp, (g + 1) * grp):
            for k in range(TOP_K):
                _row_copy(y_hbm, idx_smem[slot * per + (r * TOP_K + k)], gbuf.at[slot, k], r,
                          row_sem.at[slot]).start(priority=k % 2)

    def combine_group(slot, g):
        rows = slice(g * grp, (g + 1) * grp)
        wt = wt_ref[rows, :]
        acc_lo, acc_hi = None, None
        for k in range(TOP_K):
            lo, hi = _unpack_rows(gbuf[slot, k, rows, :])
            wk = wt[:, k:k + 1]
            acc_lo = lo * wk if acc_lo is None else acc_lo + lo * wk
            acc_hi = hi * wk if acc_hi is None else acc_hi + hi * wk
        h = base_ref[rows, :] + jnp.concatenate([acc_lo, acc_hi], axis=1)
        h_ref[rows, :] = h
        ms = jnp.mean(h * h, axis=-1, keepdims=True)
        hn_ref[rows, :] = (h * lax.rsqrt(ms + EPS) * g_ref[...]).astype(hn_ref.dtype)

    @pl.when(i == 0)
    def _():
        fetch_indices(0, 0)
        for g in range(toks // grp):
            issue_group(0, g)

    slot = i % 2
    for k in range(TOP_K):
        pltpu.make_async_copy(y_hbm.at[pl.ds(0, toks)], gbuf.at[slot, k], row_sem.at[slot]).wait()

    @pl.when(i + 1 < n_steps)
    def _():
        fetch_indices(i + 1, 1 - slot)
        for g in range(toks // grp):
            issue_group(1 - slot, g)
            combine_group(slot, g)

    @pl.when(i + 1 >= n_steps)
    def _():
        for g in range(toks // grp):
            combine_group(slot, g)


def combine(pos, y_sorted, base, wts, gain, tokens=COMBINE_TOKENS):
    m, d = base.shape
    toks = _pick(m, tokens)
    steps, per = m // toks, toks * TOP_K
    pos = pos.reshape(steps, per)
    return pl.pallas_call(
        _combine_kernel,
        out_shape=(jax.ShapeDtypeStruct((m, d), F32), jax.ShapeDtypeStruct((m, d), BF16)),
        grid=(steps,),
        in_specs=[pl.BlockSpec(memory_space=pl.ANY),
                  pl.BlockSpec(memory_space=pl.ANY),
                  pl.BlockSpec((toks, d), lambda i: (i, 0)),
                  pl.BlockSpec((toks, TOP_K), lambda i: (i, 0)),
                  pl.BlockSpec((1, d), lambda i: (0, 0))],
        out_specs=(pl.BlockSpec((toks, d), lambda i: (i, 0)),
                   pl.BlockSpec((toks, d), lambda i: (i, 0))),
        scratch_shapes=[pltpu.SMEM((2 * per,), jnp.int32),
                        pltpu.VMEM((2, TOP_K, toks, y_sorted.shape[1]), y_sorted.dtype),
                        pltpu.SemaphoreType.DMA(()),
                        pltpu.SemaphoreType.DMA((2,))],
        compiler_params=_params(("arbitrary",)),
        name="moe_combine",
    )(pos, y_sorted, base, wts, gain.reshape(1, d).astype(F32))


def dispatch_tables(eidx, rank, counts, rows):
    n_tok = eidx.shape[0]
    n_exp = counts.shape[0]
    nb = n_tok * TOP_K // rows + n_exp
    pad_end = jnp.cumsum((counts + rows - 1) // rows * rows)
    pad_start = pad_end - (counts + rows - 1) // rows * rows
    experts = jnp.arange(n_exp, dtype=jnp.int32)
    pos = rank + jnp.sum(jnp.where(eidx[..., None] == experts, pad_start, 0), axis=-1)
    blocks = jnp.arange(nb, dtype=jnp.int32)
    block_expert = jnp.minimum(jnp.sum(pad_end[None, :] // rows <= blocks[:, None], axis=-1), n_exp - 1)
    n_used = pad_end[-1:] // rows
    i32 = lambda a: a.astype(jnp.int32)
    later = (experts[None, :] > experts[:, None]) & (counts[None, :] > 0)
    next_expert = jnp.where(jnp.any(later, axis=1), jnp.argmax(later, axis=1), -1)
    return (i32(pos).reshape(-1), i32(block_expert), i32(n_used), i32(next_expert), i32(pad_start), i32(pad_end),
            nb * rows)


def _ple_kernel(hn_ref, wg_ref, p_ref, wp_ref, h_ref, o_ref):
    gate = jax.nn.sigmoid(_dot(hn_ref[...], wg_ref[...]))
    o_ref[...] = h_ref[...] + gate * _dot(p_ref[...].astype(BF16), wp_ref[...])


def ple_gate(hn, w_gate, p, w_ple, h, bm=512, bn=1024):
    m, d = hn.shape
    n = w_gate.shape[1]
    pd = p.shape[1]
    bm, bn = _pick(m, bm), _pick(n, bn)
    return pl.pallas_call(
        _ple_kernel,
        out_shape=jax.ShapeDtypeStruct((m, n), F32),
        grid=(m // bm, n // bn),
        in_specs=[pl.BlockSpec((bm, d), lambda i, j: (i, 0)),
                  pl.BlockSpec((d, bn), lambda i, j: (0, j)),
                  pl.BlockSpec((bm, pd), lambda i, j: (i, 0)),
                  pl.BlockSpec((pd, bn), lambda i, j: (0, j)),
                  pl.BlockSpec((bm, bn), lambda i, j: (i, j))],
        out_specs=pl.BlockSpec((bm, bn), lambda i, j: (i, j)),
        compiler_params=_params(("parallel", "parallel")),
        name="ple_gate",
    )(hn, w_gate, p, w_ple, h)


def _layer(h, p_i, prm, batch):
    (mix_norm, w_in, a_re, a_im, log_step, b_re, b_im, c_re, c_im, ssm_d, w_glu_a, w_glu_b, q_norm, k_norm,
     sinks, w_bs, w_ba, w_out, moe_norm, w_router, router_bias, we_gate, we_up, we_down, ws_gate, ws_up,
     ws_down, ple_norm, w_ple, w_ple_gate) = prm
    ssm_w = w_glu_a.shape[0]
    n_q = sinks.shape[0]
    attn_w = n_q * HEAD_DIM
    kv_w = attn_w // Q_PER_KV
    n_exp = w_router.shape[1]
    c0, c1 = ssm_w, ssm_w + attn_w + 2 * kv_w
    bf = lambda w: w.astype(BF16)

    hn = rmsnorm(h, mix_norm, BF16)
    u = matmul(hn, bf(w_in[:, :c0]), F32, name="proj_u")
    qkv = matmul(hn, bf(w_in[:, c0:c1]), BF16, bn=(c1 - c0) // 2, name="proj_qkv")
    gates = matmul(hn, bf(w_in[:, c1:]), BF16, name="proj_gates")
    y_pre = s5_scan(u, batch, a_re, a_im, log_step, b_re, b_im, c_re, c_im, ssm_d)
    y_ssm = glu(y_pre, bf(w_glu_a), bf(w_glu_b))
    y_attn = swa_attention(qkv, batch, n_q, q_norm, k_norm, sinks)
    merged = branch_merge(y_ssm, bf(w_bs), y_attn, bf(w_ba), gates)
    h = matmul_residual(merged, bf(w_out), h)

    hm, hm_rows, eidx, ew, rank, counts = norm_and_route(h, moe_norm, bf(w_router), router_bias)
    counts = counts.reshape(n_exp)
    pos, block_expert, n_used, next_expert, pad_start, pad_end, cap = dispatch_tables(eidx, rank, counts, MOE_ROWS)
    x_sorted, base = dispatch_and_shared_expert(hm_rows, pos, counts, pad_start, pad_end, cap, MOE_ROWS, hm,
                                                jnp.concatenate([bf(ws_gate), bf(ws_up)], axis=1), bf(ws_down), h)
    y_sorted = routed_experts_sorted(x_sorted, block_expert, n_used, next_expert, we_gate, we_up, we_down, MOE_ROWS)
    h, hn3 = combine(pos, y_sorted, base, ew, ple_norm)

    return ple_gate(hn3, bf(w_ple_gate), p_i, bf(w_ple), h)


def kernel(x, p, mix_norm, w_in, ssm_a_re, ssm_a_im, ssm_log_step, ssm_b_re, ssm_b_im, ssm_c_re, ssm_c_im, ssm_d, w_glu_a, w_glu_b, q_norm, k_norm, attn_sinks, w_branch_ssm, w_branch_attn, w_out, moe_norm, w_router, router_bias, we_gate, we_up, we_down, ws_gate, ws_up, ws_down, ple_norm, w_ple, w_ple_gate):
    bsz, seq, d = x.shape
    layer_params = (mix_norm, w_in, ssm_a_re, ssm_a_im, ssm_log_step, ssm_b_re, ssm_b_im, ssm_c_re, ssm_c_im,
                    ssm_d, w_glu_a, w_glu_b, q_norm, k_norm, attn_sinks, w_branch_ssm, w_branch_attn, w_out,
                    moe_norm, w_router, router_bias, we_gate, we_up, we_down, ws_gate, ws_up, ws_down,
                    ple_norm, w_ple, w_ple_gate)
    h = x.reshape(bsz * seq, d)
    for i in range(mix_norm.shape[0]):
        h = _layer(h, p[i].reshape(bsz * seq, -1), tuple(w[i] for w in layer_params), bsz)
    return h.reshape(bsz, seq, d)
```

```python
import functools

import jax
import jax.numpy as jnp
from jax import lax
from jax.experimental import pallas as pl
from jax.experimental.pallas import tpu as pltpu

SSM_GROUP = 16
SSM_STATE = 64
S5_CHUNK = 8
S5_TILE_LANES = 128
S5_SEQS_PER_STEP = 2
HEAD_DIM = 64
Q_PER_KV = 8
ATTN_BLOCK = 128
ALIBI_MAX_BIAS = 8.0
N_EXPERT_GROUPS = 8
TOPK_GROUPS = 4
TOP_K = 8
ROUTED_SCALE = 2.5
EPS = 1e-6
MOE_ROWS = 256
COMBINE_TOKENS = 128
COMBINE_GROUP = 16
V7X_VMEM_LIMIT = 56 * 1024 * 1024
EXPERT_VMEM_LIMIT = 62 * 1024 * 1024

BF16 = jnp.bfloat16
F32 = jnp.float32


def _dot(a, b):
    return jnp.dot(a, b, preferred_element_type=F32)


def _params(sem, vmem=V7X_VMEM_LIMIT):
    return pltpu.CompilerParams(dimension_semantics=sem, vmem_limit_bytes=vmem)


def _pick(n, pref):
    b = min(n, pref)
    while n % b:
        b //= 2
    return b


def _rmsnorm_kernel(x_ref, g_ref, o_ref):
    x = x_ref[...]
    ms = jnp.mean(x * x, axis=-1, keepdims=True)
    o_ref[...] = (x * lax.rsqrt(ms + EPS) * g_ref[...]).astype(o_ref.dtype)


def rmsnorm(x, g, out_dtype, bm=256):
    m, d = x.shape
    bm = _pick(m, bm)
    return pl.pallas_call(
        _rmsnorm_kernel,
        out_shape=jax.ShapeDtypeStruct((m, d), out_dtype),
        grid=(m // bm,),
        in_specs=[pl.BlockSpec((bm, d), lambda i: (i, 0)),
                  pl.BlockSpec((1, d), lambda i: (0, 0))],
        out_specs=pl.BlockSpec((bm, d), lambda i: (i, 0)),
        compiler_params=_params(("parallel",)),
        name="rmsnorm",
    )(x, g.reshape(1, d).astype(F32))


def _mm_kernel(x_ref, w_ref, o_ref):
    o_ref[...] = _dot(x_ref[...], w_ref[...]).astype(o_ref.dtype)


def matmul(x, w, out_dtype, bm=1024, bn=1024, name="matmul"):
    m, k = x.shape
    n = w.shape[1]
    bm, bn = _pick(m, bm), _pick(n, bn)
    return pl.pallas_call(
        _mm_kernel,
        out_shape=jax.ShapeDtypeStruct((m, n), out_dtype),
        grid=(m // bm, n // bn),
        in_specs=[pl.BlockSpec((bm, k), lambda i, j: (i, 0)),
                  pl.BlockSpec((k, bn), lambda i, j: (0, j))],
        out_specs=pl.BlockSpec((bm, bn), lambda i, j: (i, j)),
        compiler_params=_params(("parallel", "parallel")),
        name=name,
    )(x, w)


def _spread_groups(compact, n_out, src_of, group_of, rows_per_group):
    n_in = compact.shape[1]
    k = lax.broadcasted_iota(jnp.int32, (n_in, n_out), 0)
    q = lax.broadcasted_iota(jnp.int32, (n_in, n_out), 1)
    spread = _dot(compact, jnp.where(src_of(q) == k, 1.0, 0.0).astype(BF16))
    r = lax.broadcasted_iota(jnp.int32, spread.shape, 0)
    q = lax.broadcasted_iota(jnp.int32, spread.shape, 1)
    return jnp.where(group_of(q) == r // rows_per_group, spread, 0.0).astype(BF16)


def _s5_chunk_matrices(kc_ref, fc_ref, ec_ref, w_s, e_s):
    tc, lanes, cg, ns = S5_CHUNK, S5_TILE_LANES, SSM_GROUP, SSM_STATE
    half = e_s.shape[0] // 2
    w_s[...] = jnp.zeros_like(w_s)
    for tau in range(tc):
        blk = _spread_groups(kc_ref[tau], lanes, lambda q: q % cg, lambda q: q // cg, cg)
        for i in range(tc - tau):
            w_s[i * lanes:(i + 1) * lanes, (i + tau) * lanes:(i + tau + 1) * lanes] = blk
    for i in range(tc):
        f = fc_ref[i]
        rows = slice(i * lanes, (i + 1) * lanes)
        w_s[rows, tc * lanes:tc * lanes + half] = _spread_groups(f, half, lambda q: q % ns, lambda q: q // ns, cg)
        w_s[rows, tc * lanes + half:] = _spread_groups(f, half, lambda q: q % ns + ns, lambda q: q // ns, cg)
    for part in range(2):
        e_s[part * half:(part + 1) * half, :] = _spread_groups(
            ec_ref[part], tc * lanes, lambda q: (q // lanes) * cg + q % cg, lambda q: (q // cg) % (lanes // cg), ns)


def _s5_kernel(u_ref, kc_ref, fc_ref, ec_ref, pw_ref, d_ref, o_ref, fu_ref, w_ref, e_ref):
    @pl.when(pl.program_id(1) == 0)
    def _():
        _s5_chunk_matrices(kc_ref, fc_ref, ec_ref, w_ref, e_ref)

    tc = S5_CHUNK
    seqs = fu_ref.shape[0]
    n_chunks = u_ref.shape[0] // seqs // tc
    lanes = u_ref.shape[1]
    half = fu_ref.shape[2] // 2
    row = lax.broadcasted_iota(jnp.int32, (8, half), 0)
    pw = [pw_ref[k] for k in range(10)]

    def cmul(ar, ai, xr, xi):
        return ar * xr - ai * xi, ar * xi + ai * xr

    def scan_tile(fu, t, carry):
        c_re, c_im = carry
        rows = slice(t * 8, (t + 1) * 8)
        x_re = fu[rows, 0:half]
        x_im = fu[rows, half:2 * half]
        for k, sh in enumerate((1, 2, 4)):
            s_re = jnp.where(row >= sh, pltpu.roll(x_re, sh, 0), 0.0)
            s_im = jnp.where(row >= sh, pltpu.roll(x_im, sh, 0), 0.0)
            m_re, m_im = cmul(pw[2 * k], pw[2 * k + 1], s_re, s_im)
            x_re, x_im = x_re + m_re, x_im + m_im
        p_re, p_im = cmul(pw[6], pw[7], c_re, c_im)
        fu[rows, 0:half] = jnp.where(row >= 1, pltpu.roll(x_re, 1, 0), 0.0) + p_re
        fu[rows, half:2 * half] = jnp.where(row >= 1, pltpu.roll(x_im, 1, 0), 0.0) + p_im
        f_re, f_im = cmul(pw[8], pw[9], c_re, c_im)
        f_re, f_im = f_re + x_re, f_im + x_im
        return jnp.broadcast_to(f_re[7:8], (8, half)), jnp.broadcast_to(f_im[7:8], (8, half))

    for q in range(seqs):
        first = q * n_chunks * tc
        x = jnp.concatenate([u_ref[pl.ds(first + i, n_chunks, stride=tc), :] for i in range(tc)], axis=1)
        z = _dot(x.astype(BF16), w_ref[...])
        fu = fu_ref.at[q]
        fu[...] = z[:, tc * lanes:]
        carry = (jnp.zeros((8, half), F32), jnp.zeros((8, half), F32))
        for t in range(n_chunks // 8):
            carry = scan_tile(fu, t, carry)
        y = z[:, :tc * lanes] + _dot(fu[...].astype(BF16), e_ref[...]) + d_ref[...] * x
        y = jax.nn.gelu(y)
        for j in range(tc):
            o_ref[pl.ds(first + j, n_chunks, stride=tc), :] = y[:, j * lanes:(j + 1) * lanes]


def s5_scan(u, batch, a_re, a_im, log_step, b_re, b_im, c_re, c_im, d_skip):
    rows, width = u.shape
    seq = rows // batch
    tc, lanes = S5_CHUNK, S5_TILE_LANES
    gpt = lanes // SSM_GROUP
    n_tiles = width // lanes
    half = gpt * SSM_STATE
    assert seq % (8 * tc) == 0 and width % lanes == 0
    seqs = _pick(batch, S5_SEQS_PER_STEP)
    lam = lax.complex(a_re.astype(F32), a_im.astype(F32))
    lam_dt = lam * jnp.exp(log_step.astype(F32))[:, None]
    b_bar = ((jnp.exp(lam_dt) - 1.0) / lam)[..., None] * lax.complex(b_re.astype(F32), b_im.astype(F32))
    c_mat = lax.complex(c_re.astype(F32), c_im.astype(F32))
    lp = jnp.exp(lam_dt[None] * jnp.arange(tc + 1, dtype=F32)[:, None, None])
    kern = jnp.real(jnp.einsum('gcn,tgn,gnd->tgcd', c_mat, lp[:tc], b_bar))
    kc = kern.transpose(1, 0, 3, 2).reshape(n_tiles, gpt, tc, SSM_GROUP, SSM_GROUP)
    kc = kc.transpose(0, 2, 1, 3, 4).reshape(n_tiles, tc, lanes, SSM_GROUP).astype(BF16)
    f_c = lp[tc - 1 - jnp.arange(tc)][..., None] * b_bar[None]
    f_t = lambda a: (a.transpose(1, 0, 3, 2).reshape(n_tiles, gpt, tc, SSM_GROUP, SSM_STATE)
                     .transpose(0, 2, 1, 3, 4).reshape(n_tiles, tc, lanes, SSM_STATE))
    fc = jnp.concatenate([f_t(jnp.real(f_c)), f_t(jnp.imag(f_c))], axis=-1).astype(BF16)
    e_c = c_mat[None] * lp[1:, :, None, :]
    e_t = lambda a: a.transpose(1, 3, 0, 2).reshape(n_tiles, half, tc * SSM_GROUP)
    ec = jnp.stack([e_t(jnp.real(e_c)), e_t(-jnp.imag(e_c))], axis=1).astype(BF16)
    r8 = jnp.arange(8, dtype=F32)[:, None, None]
    a_pow = lambda e: jnp.exp(lam_dt[None] * (tc * e))
    plist = [a_pow(jnp.full_like(r8, e)) for e in (1.0, 2.0, 4.0)] + [a_pow(r8), a_pow(r8 + 1.0)]
    pws = jnp.stack([f(p) for p in plist for f in (jnp.real, jnp.imag)])
    pws = pws.reshape(10, 8, n_tiles, half).transpose(2, 0, 1, 3)
    dsk = jnp.tile(d_skip.astype(F32).reshape(n_tiles, 1, lanes), (1, 1, tc))
    return pl.pallas_call(
        _s5_kernel,
        out_shape=jax.ShapeDtypeStruct((rows, width), F32),
        grid=(n_tiles, batch // seqs),
        in_specs=[pl.BlockSpec((seqs * seq, lanes), lambda s, b: (b, s)),
                  pl.BlockSpec((None, tc, lanes, SSM_GROUP), lambda s, b: (s, 0, 0, 0)),
                  pl.BlockSpec((None, tc, lanes, 2 * SSM_STATE), lambda s, b: (s, 0, 0, 0)),
                  pl.BlockSpec((None, 2, half, tc * SSM_GROUP), lambda s, b: (s, 0, 0, 0)),
                  pl.BlockSpec((None, 10, 8, half), lambda s, b: (s, 0, 0, 0)),
                  pl.BlockSpec((None, 1, tc * lanes), lambda s, b: (s, 0, 0))],
        out_specs=pl.BlockSpec((seqs * seq, lanes), lambda s, b: (b, s)),
        scratch_shapes=[pltpu.VMEM((seqs, seq // tc, 2 * half), F32),
                        pltpu.VMEM((tc * lanes, tc * lanes + 2 * half), BF16),
                        pltpu.VMEM((2 * half, tc * lanes), BF16)],
        compiler_params=_params(("parallel", "arbitrary")),
        name="s5_scan",
    )(u, kc, fc, ec, pws, dsk)


def _glu_kernel(x_ref, wa_ref, wb_ref, o_ref):
    x = x_ref[...].astype(BF16)
    o_ref[...] = (_dot(x, wa_ref[...]) * jax.nn.sigmoid(_dot(x, wb_ref[...]))).astype(o_ref.dtype)


def glu(x, wa, wb, bm=1024, bn=1024):
    m, k = x.shape
    n = wa.shape[1]
    bm, bn = _pick(m, bm), _pick(n, bn)
    return pl.pallas_call(
        _glu_kernel,
        out_shape=jax.ShapeDtypeStruct((m, n), BF16),
        grid=(m // bm, n // bn),
        in_specs=[pl.BlockSpec((bm, k), lambda i, j: (i, 0)),
                  pl.BlockSpec((k, bn), lambda i, j: (0, j)),
                  pl.BlockSpec((k, bn), lambda i, j: (0, j))],
        out_specs=pl.BlockSpec((bm, bn), lambda i, j: (i, j)),
        compiler_params=_params(("parallel", "parallel")),
        name="ssm_glu",
    )(x, wa, wb)


def _attn_kernel(sink_ref, q_ref, kc_ref, kp_ref, vc_ref, vp_ref, qg_ref, kg_ref, o_ref, *, n_kv):
    blk, hd = ATTN_BLOCK, HEAD_DIM
    first = pl.program_id(1) == 0
    n_q = n_kv * Q_PER_KV
    kj = lax.broadcasted_iota(jnp.int32, (2 * blk, blk), 0)
    qi = lax.broadcasted_iota(jnp.int32, (2 * blk, blk), 1)
    dist = qi + blk - kj
    kmin = jnp.where(first, blk, 0)
    valid = (dist >= 0) & (dist < blk) & (kj >= kmin)
    dist_f = dist.astype(F32)

    def head_norm_t(t, heads, gain):
        t3 = t.reshape(heads, hd, t.shape[1])
        ms = jnp.mean(t3 * t3, axis=1, keepdims=True)
        return t3 * lax.rsqrt(ms + EPS) * gain[None]

    qn = (head_norm_t(q_ref[...].astype(F32).T, n_q, qg_ref[...]) * (hd ** -0.5)).astype(BF16)
    kk = jnp.concatenate([kp_ref[...], kc_ref[...]], axis=0).astype(F32)
    kn = head_norm_t(kk.T, n_kv, kg_ref[...]).reshape(n_kv * hd, 2 * blk).T.astype(BF16)
    vt = jnp.concatenate([vp_ref[...], vc_ref[...]], axis=0).astype(F32).T.astype(BF16)

    for h in range(n_kv):
        qt = jnp.concatenate([qn[h * Q_PER_KV + g] for g in range(Q_PER_KV)], axis=1)
        st = _dot(kn[:, h * hd:(h + 1) * hd], qt)
        probs = []
        for g in range(Q_PER_KV):
            head = h * Q_PER_KV + g
            slope = 2.0 ** (-ALIBI_MAX_BIAS * (head + 1) / n_q)
            s = jnp.where(valid, st[:, g * blk:(g + 1) * blk] - slope * dist_f, -jnp.inf)
            sink = sink_ref[head]
            m = jnp.maximum(jnp.max(s, axis=0, keepdims=True), sink)
            p = jnp.exp(s - m)
            denom = jnp.sum(p, axis=0, keepdims=True) + jnp.exp(sink - m)
            probs.append((p * (1.0 / denom)).astype(BF16))
        ot = _dot(vt[h * hd:(h + 1) * hd, :], jnp.concatenate(probs, axis=1))
        ot = jnp.concatenate([ot[:, g * blk:(g + 1) * blk] for g in range(Q_PER_KV)], axis=0)
        o_ref[:, h * Q_PER_KV * hd:(h + 1) * Q_PER_KV * hd] = ot.T.astype(o_ref.dtype)


def swa_attention(qkv, batch, n_q, q_gain, k_gain, sinks):
    rows = qkv.shape[0]
    n_kv = n_q // Q_PER_KV
    qw, kw = n_q * HEAD_DIM, n_kv * HEAD_DIM
    blk = ATTN_BLOCK
    nb = rows // batch // blk
    kcol = qw // kw
    cur = lambda b, j: b * nb + j
    prev = lambda b, j: b * nb + jnp.maximum(j - 1, 0)
    return pl.pallas_call(
        functools.partial(_attn_kernel, n_kv=n_kv),
        out_shape=jax.ShapeDtypeStruct((rows, qw), BF16),
        grid=(batch, nb),
        in_specs=[pl.BlockSpec(memory_space=pltpu.SMEM),
                  pl.BlockSpec((blk, qw), lambda b, j: (cur(b, j), 0)),
                  pl.BlockSpec((blk, kw), lambda b, j: (cur(b, j), kcol)),
                  pl.BlockSpec((blk, kw), lambda b, j: (prev(b, j), kcol)),
                  pl.BlockSpec((blk, kw), lambda b, j: (cur(b, j), kcol + 1)),
                  pl.BlockSpec((blk, kw), lambda b, j: (prev(b, j), kcol + 1)),
                  pl.BlockSpec((HEAD_DIM, blk), lambda b, j: (0, 0)),
                  pl.BlockSpec((HEAD_DIM, 2 * blk), lambda b, j: (0, 0))],
        out_specs=pl.BlockSpec((blk, qw), lambda b, j: (cur(b, j), 0)),
        compiler_params=_params(("parallel", "arbitrary")),
        name="swa_attention",
    )(sinks.astype(F32), qkv, qkv, qkv, qkv, qkv,
      jnp.broadcast_to(q_gain.astype(F32)[:, None], (HEAD_DIM, blk)),
      jnp.broadcast_to(k_gain.astype(F32)[:, None], (HEAD_DIM, 2 * blk)))


def _merge_kernel(ys_ref, ws_ref, ya_ref, wa_ref, gs_ref, ga_ref, o_ref):
    s = jax.nn.sigmoid(gs_ref[...].astype(F32)) * _dot(ys_ref[...], ws_ref[...])
    a = jax.nn.sigmoid(ga_ref[...].astype(F32)) * _dot(ya_ref[...], wa_ref[...])
    o_ref[...] = (s + a).astype(o_ref.dtype)


def branch_merge(y_ssm, w_ssm, y_attn, w_attn, gates, bm=1024, bn=1024):
    m, ks = y_ssm.shape
    ka = y_attn.shape[1]
    n = w_ssm.shape[1]
    bm, bn = _pick(m, bm), _pick(n, bn)
    nj = n // bn
    return pl.pallas_call(
        _merge_kernel,
        out_shape=jax.ShapeDtypeStruct((m, n), BF16),
        grid=(m // bm, nj),
        in_specs=[pl.BlockSpec((bm, ks), lambda i, j: (i, 0)),
                  pl.BlockSpec((ks, bn), lambda i, j: (0, j)),
                  pl.BlockSpec((bm, ka), lambda i, j: (i, 0)),
                  pl.BlockSpec((ka, bn), lambda i, j: (0, j)),
                  pl.BlockSpec((bm, bn), lambda i, j: (i, j)),
                  pl.BlockSpec((bm, bn), lambda i, j: (i, nj + j))],
        out_specs=pl.BlockSpec((bm, bn), lambda i, j: (i, j)),
        compiler_params=_params(("parallel", "parallel")),
        name="branch_merge",
    )(y_ssm, w_ssm, y_attn, w_attn, gates, gates)


def _mm_res_kernel(x_ref, w_ref, r_ref, o_ref):
    o_ref[...] = r_ref[...] + _dot(x_ref[...], w_ref[...])


def matmul_residual(x, w, res, bm=1024, bn=1024):
    m, k = x.shape
    n = w.shape[1]
    bm, bn = _pick(m, bm), _pick(n, bn)
    return pl.pallas_call(
        _mm_res_kernel,
        out_shape=jax.ShapeDtypeStruct((m, n), F32),
        grid=(m // bm, n // bn),
        in_specs=[pl.BlockSpec((bm, k), lambda i, j: (i, 0)),
                  pl.BlockSpec((k, bn), lambda i, j: (0, j)),
                  pl.BlockSpec((bm, bn), lambda i, j: (i, j))],
        out_specs=pl.BlockSpec((bm, bn), lambda i, j: (i, j)),
        compiler_params=_params(("parallel", "parallel")),
        name="out_proj_residual",
    )(x, w, res)


def _router_kernel(h_ref, g_ref, w_ref, b_ref, hm_ref, rows_ref, idx_ref, wt_ref, rank_ref, cnt_ref, seen_ref,
                   *, n_exp):
    @pl.when(pl.program_id(0) == 0)
    def _():
        seen_ref[...] = jnp.zeros_like(seen_ref)

    h = h_ref[...]
    hn = h * lax.rsqrt(jnp.mean(h * h, axis=-1, keepdims=True) + EPS) * g_ref[...]
    hm = hn.astype(BF16)
    hm_ref[...] = hm
    rows_ref[...] = _pack_rows(hn)
    s = jax.nn.sigmoid(_dot(hm, w_ref[...]))
    sel = s + b_ref[...]
    rows = s.shape[0]
    lane_i = lax.broadcasted_iota(jnp.int32, (rows, n_exp), 1)
    per_group = n_exp // N_EXPERT_GROUPS
    lane = lane_i.astype(F32)
    grp = (lane_i // per_group).astype(F32)
    neg = -jnp.inf
    gscore = jnp.zeros_like(sel)
    for g in range(N_EXPERT_GROUPS):
        in_g = grp == g
        v = jnp.where(in_g, sel, neg)
        m1 = jnp.max(v, axis=-1, keepdims=True)
        i1 = jnp.min(jnp.where(v == m1, lane, n_exp), axis=-1, keepdims=True)
        m2 = jnp.max(jnp.where(lane == i1, neg, v), axis=-1, keepdims=True)
        gscore = jnp.where(in_g, m1 + m2, gscore)
    cand = jnp.full_like(sel, neg)
    remaining = gscore
    for _ in range(TOPK_GROUPS):
        gm = jnp.max(remaining, axis=-1, keepdims=True)
        gi = jnp.min(jnp.where(remaining == gm, grp, N_EXPERT_GROUPS), axis=-1, keepdims=True)
        hit = grp == gi
        cand = jnp.where(hit, sel, cand)
        remaining = jnp.where(hit, neg, remaining)
    slot = lax.broadcasted_iota(jnp.int32, (rows, TOP_K), 1)
    idx = jnp.zeros((rows, TOP_K), F32)
    wts = jnp.zeros((rows, TOP_K), F32)
    total = jnp.zeros((rows, 1), F32)
    picked = jnp.zeros_like(sel)
    hits = []
    for k in range(TOP_K):
        mx = jnp.max(cand, axis=-1, keepdims=True)
        ei = jnp.min(jnp.where(cand == mx, lane, n_exp), axis=-1, keepdims=True)
        hit = lane == ei
        wk = jnp.sum(jnp.where(hit, s, 0.0), axis=-1, keepdims=True)
        idx = jnp.where(slot == k, ei, idx)
        wts = jnp.where(slot == k, wk, wts)
        total = total + wk
        cand = jnp.where(hit, neg, cand)
        picked = jnp.where(hit, 1.0, picked)
        hits.append(hit)
    idx_ref[...] = idx.astype(jnp.int32)
    wt_ref[...] = wts / total * ROUTED_SCALE
    r_i = lax.broadcasted_iota(jnp.int32, (rows, rows), 0)
    c_i = lax.broadcasted_iota(jnp.int32, (rows, rows), 1)
    lower = jnp.where(c_i < r_i, 1.0, 0.0).astype(BF16)
    before = _dot(lower, picked.astype(BF16)) + seen_ref[...]
    rank = jnp.zeros((rows, TOP_K), F32)
    for k in range(TOP_K):
        rank = jnp.where(slot == k, jnp.sum(jnp.where(hits[k], before, 0.0), axis=-1, keepdims=True), rank)
    rank_ref[...] = rank.astype(jnp.int32)
    seen = seen_ref[...] + jnp.sum(picked, axis=0, keepdims=True)
    seen_ref[...] = seen
    cnt_ref[...] = seen.astype(jnp.int32)


def norm_and_route(h, gain, w_router, bias, bm=512):
    m, d = h.shape
    n_exp = w_router.shape[1]
    bm = _pick(m, bm)
    packed = jax.eval_shape(_pack_rows, jax.ShapeDtypeStruct((bm, d), F32))
    pw = packed.shape[1]
    tk = lambda dt: jax.ShapeDtypeStruct((m, TOP_K), dt)
    tk_spec = pl.BlockSpec((bm, TOP_K), lambda i: (i, 0))
    return pl.pallas_call(
        functools.partial(_router_kernel, n_exp=n_exp),
        out_shape=(jax.ShapeDtypeStruct((m, d), BF16), jax.ShapeDtypeStruct((m, pw), packed.dtype),
                   tk(jnp.int32), tk(F32), tk(jnp.int32), jax.ShapeDtypeStruct((1, n_exp), jnp.int32)),
        grid=(m // bm,),
        in_specs=[pl.BlockSpec((bm, d), lambda i: (i, 0)),
                  pl.BlockSpec((1, d), lambda i: (0, 0)),
                  pl.BlockSpec((d, n_exp), lambda i: (0, 0)),
                  pl.BlockSpec((1, n_exp), lambda i: (0, 0))],
        out_specs=(pl.BlockSpec((bm, d), lambda i: (i, 0)),
                   pl.BlockSpec((bm, pw), lambda i: (i, 0)),
                   tk_spec, tk_spec, tk_spec, pl.BlockSpec((1, n_exp), lambda i: (0, 0))),
        scratch_shapes=[pltpu.VMEM((1, n_exp), F32)],
        compiler_params=_params(("arbitrary",)),
        name="norm_and_route",
    )(h, gain.reshape(1, d).astype(F32), w_router, bias.reshape(1, n_exp).astype(F32))


def _swiglu(x, wgu, wd):
    ff = wd.shape[0]
    gu = _dot(x, wgu)
    act = (jax.nn.silu(gu[:, :ff]) * gu[:, ff:]).astype(BF16)
    return _dot(act, wd)


def _shared_kernel(x_ref, wgu_ref, wd_ref, r_ref, o_ref):
    o_ref[...] = r_ref[...] + _swiglu(x_ref[...], wgu_ref[...], wd_ref[...])


def shared_expert_residual(hm, wgu, wd, res, bm=256):
    m, d = hm.shape
    ff = wd.shape[0]
    bm = _pick(m, bm)
    return pl.pallas_call(
        _shared_kernel,
        out_shape=jax.ShapeDtypeStruct((m, d), F32),
        grid=(m // bm,),
        in_specs=[pl.BlockSpec((bm, d), lambda i: (i, 0)),
                  pl.BlockSpec((d, 2 * ff), lambda i: (0, 0)),
                  pl.BlockSpec((ff, d), lambda i: (0, 0)),
                  pl.BlockSpec((bm, d), lambda i: (i, 0))],
        out_specs=pl.BlockSpec((bm, d), lambda i: (i, 0)),
        compiler_params=_params(("parallel",)),
        name="shared_expert",
    )(hm, wgu, wd, res)


def _pack_rows(v):
    c = v.shape[1] // 2
    lo = lax.bitcast_convert_type(v[:, :c].astype(BF16).astype(F32), jnp.uint32)
    hi = lax.bitcast_convert_type(v[:, c:].astype(BF16).astype(F32), jnp.uint32)
    return hi | (lo >> 16)


def _unpack_rows(w):
    lo = lax.bitcast_convert_type(w << 16, F32)
    hi = lax.bitcast_convert_type(w & jnp.uint32(0xFFFF0000), F32)
    return lo, hi


def _row_copy(src, src_row, dst, dst_row, sem):
    return pltpu.make_async_copy(src.at[pl.ds(src_row, 1)], dst.at[pl.ds(dst_row, 1)], sem)


def _dispatch_kernel(cnt_ref, pstart_ref, pend_ref, pos_hbm, x_hbm, o_hbm,
                     idx_smem, xbuf, zbuf, idx_sem, load_sem, row_sem):
    i = pl.program_id(0)
    n_steps = pl.num_programs(0)
    n_slots, toks = xbuf.shape[0], xbuf.shape[1]
    per = toks * TOP_K

    def idx_copy(step):
        dst = idx_smem.at[pl.ds(pl.multiple_of((step % 2) * per, per), per)]
        return pltpu.make_async_copy(pos_hbm.at[step], dst, idx_sem.at[step % 2])

    def load(step):
        src = x_hbm.at[pl.ds(pl.multiple_of(step * toks, toks), toks)]
        return pltpu.make_async_copy(src, xbuf.at[step % n_slots], load_sem.at[step % n_slots])

    def wait_rows(step):
        whole = o_hbm.at[pl.ds(0, per)]
        pltpu.make_async_copy(whole, whole, row_sem.at[step % n_slots]).wait()

    @pl.when(i == 0)
    def _():
        zbuf[...] = jnp.zeros_like(zbuf)
        idx_copy(0).start()
        load(0).start()

    @pl.when(i >= 2)
    def _():
        wait_rows(i - 2)

    @pl.when(i + 1 < n_steps)
    def _():
        idx_copy(i + 1).start()
        load(i + 1).start()

    idx_copy(i).wait()
    load(i).wait()
    slot, islot = i % n_slots, i % 2

    def body(g, _):
        t0 = pl.multiple_of(g * 8, 8)
        for tt in range(8):
            for k in range(TOP_K):
                dst = idx_smem[islot * per + t0 * TOP_K + (tt * TOP_K + k)]
                _row_copy(xbuf.at[slot], t0 + tt, o_hbm, dst, row_sem.at[slot]).start(priority=k % 2)
        return 0

    lax.fori_loop(0, toks // 8, body, 0)

    @pl.when(i == n_steps - 1)
    def _():
        @pl.when(i >= 1)
        def _():
            wait_rows(i - 1)

        wait_rows(i)

        def expert_padding(e, _):
            first = pstart_ref[e] + cnt_ref[e]
            n_pad = pend_ref[e] - first

            def zbody(r, _):
                _row_copy(zbuf, 0, o_hbm, first + r, row_sem.at[0]).start()
                return 0

            def zwait(r, _):
                _row_copy(zbuf, 0, o_hbm, first, row_sem.at[0]).wait()
                return 0

            lax.fori_loop(0, n_pad, zbody, 0)
            lax.fori_loop(0, n_pad, zwait, 0)
            return 0

        lax.fori_loop(0, cnt_ref.shape[0], expert_padding, 0)

        blk_rows = zbuf.shape[0]
        n_exp = cnt_ref.shape[0]
        first_blk = pend_ref[n_exp - 1] // blk_rows
        n_tail = o_hbm.shape[0] // blk_rows - first_blk

        def tail_copy(b):
            dst = o_hbm.at[pl.ds(pl.multiple_of((first_blk + b) * blk_rows, blk_rows), blk_rows)]
            return pltpu.make_async_copy(zbuf, dst, row_sem.at[0])

        def tbody(b, _):
            tail_copy(b).start()
            return 0

        def twait(b, _):
            tail_copy(b).wait()
            return 0

        lax.fori_loop(0, n_tail, tbody, 0)
        lax.fori_loop(0, n_tail, twait, 0)


def dispatch_rows(x_rows, pos, counts, pad_start, pad_end, cap, blk_rows, chunk_tokens=512):
    n_tok, width = x_rows.shape
    toks = _pick(n_tok, chunk_tokens)
    steps = n_tok // toks
    n_slots = 3
    grid_spec = pltpu.PrefetchScalarGridSpec(
        num_scalar_prefetch=3,
        grid=(steps,),
        in_specs=[pl.BlockSpec(memory_space=pl.ANY)] * 2,
        out_specs=pl.BlockSpec(memory_space=pl.ANY),
        scratch_shapes=[pltpu.SMEM((2 * toks * TOP_K,), jnp.int32),
                        pltpu.VMEM((n_slots, toks, width), x_rows.dtype),
                        pltpu.VMEM((blk_rows, width), x_rows.dtype),
                        pltpu.SemaphoreType.DMA((2,)),
                        pltpu.SemaphoreType.DMA((n_slots,)),
                        pltpu.SemaphoreType.DMA((n_slots,))],
    )
    return pl.pallas_call(
        _dispatch_kernel,
        out_shape=jax.ShapeDtypeStruct((cap, width), x_rows.dtype),
        grid_spec=grid_spec,
        compiler_params=_params(("arbitrary",)),
        name="moe_dispatch",
    )(counts, pad_start, pad_end, pos.reshape(steps, toks * TOP_K), x_rows)


def _expert_kernel(bexp_ref, nused_ref, next_ref, x_ref, wg_hbm, wu_hbm, wd_hbm, o_ref,
                   sg_ref, su_ref, sd_ref, wgu_ref, wd_ref, sem):
    i = pl.program_id(0)
    e = bexp_ref[i]

    def fetches(ex):
        return [pltpu.make_async_copy(src.at[ex], dst, sem.at[s])
                for s, (src, dst) in enumerate(((wg_hbm, sg_ref), (wu_hbm, su_ref), (wd_hbm, sd_ref)))]

    def to_bf16(src, dst, col0=0):
        chunk = _pick(src.shape[0], 256)

        def body(r, _):
            rows = pl.ds(pl.multiple_of(r * chunk, chunk), chunk)
            dst[rows, pl.ds(col0, src.shape[1])] = src[rows, :].astype(BF16)
            return 0

        lax.fori_loop(0, src.shape[0] // chunk, body, 0)

    @pl.when(i < nused_ref[0])
    def _():
        first_block_of_expert = jnp.logical_or(i == 0, bexp_ref[jnp.maximum(i - 1, 0)] != e)

        @pl.when(first_block_of_expert)
        def _():
            @pl.when(i == 0)
            def _():
                for cp in fetches(e):
                    cp.start()

            for cp in fetches(e):
                cp.wait()
            to_bf16(sg_ref, wgu_ref)
            to_bf16(su_ref, wgu_ref, col0=sg_ref.shape[1])
            to_bf16(sd_ref, wd_ref)

            @pl.when(next_ref[e] >= 0)
            def _():
                for cp in fetches(next_ref[e]):
                    cp.start()

        lo, hi = _unpack_rows(x_ref[...])
        x = jnp.concatenate([lo, hi], axis=1).astype(BF16)
        o_ref[...] = _pack_rows(_swiglu(x, wgu_ref[...], wd_ref[...]))

    @pl.when(i >= nused_ref[0])
    def _():
        o_ref[...] = jnp.zeros_like(o_ref)


def routed_experts_sorted(x_sorted, block_expert, n_used, next_expert, wg, wu, wd, rows):
    cap, width = x_sorted.shape
    nb = cap // rows
    d, ff = wg.shape[1], wg.shape[2]
    used = lambda i, be, nu, nx: (jnp.maximum(jnp.minimum(i, nu[0] - 1), 0), 0)
    grid_spec = pltpu.PrefetchScalarGridSpec(
        num_scalar_prefetch=3,
        grid=(nb,),
        in_specs=[pl.BlockSpec((rows, width), used)] + [pl.BlockSpec(memory_space=pl.ANY)] * 3,
        out_specs=pl.BlockSpec((rows, width), lambda i, be, nu, nx: (i, 0)),
        scratch_shapes=[pltpu.VMEM((d, ff), wg.dtype), pltpu.VMEM((d, ff), wu.dtype), pltpu.VMEM((ff, d), wd.dtype),
                        pltpu.VMEM((d, 2 * ff), BF16), pltpu.VMEM((ff, d), BF16),
                        pltpu.SemaphoreType.DMA((3,))],
    )
    return pl.pallas_call(
        _expert_kernel,
        out_shape=jax.ShapeDtypeStruct((cap, width), x_sorted.dtype),
        grid_spec=grid_spec,
        compiler_params=_params(("arbitrary",), EXPERT_VMEM_LIMIT),
        name="routed_experts",
    )(block_expert, n_used, next_expert, x_sorted, wg, wu, wd)


def _combine_kernel(pos_hbm, y_hbm, base_ref, wt_ref, g_ref, h_ref, hn_ref, idx_smem, gbuf, idx_sem, row_sem):
    i = pl.program_id(0)
    n_steps = pl.num_programs(0)
    toks = gbuf.shape[2]
    per = toks * TOP_K
    grp = COMBINE_GROUP

    def fetch_indices(blk, slot):
        cp = pltpu.make_async_copy(pos_hbm.at[blk], idx_smem.at[pl.ds(pl.multiple_of(slot * per, per), per)], idx_sem)
        cp.start()
        cp.wait()

    def issue_group(slot, g):
        for r in range(g * grp, (g + 1) * grp):
            for k in range(TOP_K):
                _row_copy(y_hbm, idx_smem[slot * per + (r * TOP_K + k)], gbuf.at[slot, k], r,
                          row_sem.at[slot]).start(priority=k % 2)

    def combine_group(slot, g):
        rows = slice(g * grp, (g + 1) * grp)
        wt = wt_ref[rows, :]
        acc_lo, acc_hi = None, None
        for k in range(TOP_K):
            lo, hi = _unpack_rows(gbuf[slot, k, rows, :])
            wk = wt[:, k:k + 1]
            acc_lo = lo * wk if acc_lo is None else acc_lo + lo * wk
            acc_hi = hi * wk if acc_hi is None else acc_hi + hi * wk
        h = base_ref[rows, :] + jnp.concatenate([acc_lo, acc_hi], axis=1)
        h_ref[rows, :] = h
        ms = jnp.mean(h * h, axis=-1, keepdims=True)
        hn_ref[rows, :] = (h * lax.rsqrt(ms + EPS) * g_ref[...]).astype(hn_ref.dtype)

    @pl.when(i == 0)
    def _():
        fetch_indices(0, 0)
        for g in range(toks // grp):
            issue_group(0, g)

    slot = i % 2
    for k in range(TOP_K):
        pltpu.make_async_copy(y_hbm.at[pl.ds(0, toks)], gbuf.at[slot, k], row_sem.at[slot]).wait()

    @pl.when(i + 1 < n_steps)
    def _():
        fetch_indices(i + 1, 1 - slot)
        for g in range(toks // grp):
            issue_group(1 - slot, g)
            combine_group(slot, g)

    @pl.when(i + 1 >= n_steps)
    def _():
        for g in range(toks // grp):
            combine_group(slot, g)


def combine(pos, y_sorted, base, wts, gain, tokens=COMBINE_TOKENS):
    m, d = base.shape
    toks = _pick(m, tokens)
    steps, per = m // toks, toks * TOP_K
    pos = pos.reshape(steps, per)
    return pl.pallas_call(
        _combine_kernel,
        out_shape=(jax.ShapeDtypeStruct((m, d), F32), jax.ShapeDtypeStruct((m, d), BF16)),
        grid=(steps,),
        in_specs=[pl.BlockSpec(memory_space=pl.ANY),
                  pl.BlockSpec(memory_space=pl.ANY),
                  pl.BlockSpec((toks, d), lambda i: (i, 0)),
                  pl.BlockSpec((toks, TOP_K), lambda i: (i, 0)),
                  pl.BlockSpec((1, d), lambda i: (0, 0))],
        out_specs=(pl.BlockSpec((toks, d), lambda i: (i, 0)),
                   pl.BlockSpec((toks, d), lambda i: (i, 0))),
        scratch_shapes=[pltpu.SMEM((2 * per,), jnp.int32),
                        pltpu.VMEM((2, TOP_K, toks, y_sorted.shape[1]), y_sorted.dtype),
                        pltpu.SemaphoreType.DMA(()),
                        pltpu.SemaphoreType.DMA((2,))],
        compiler_params=_params(("arbitrary",)),
        name="moe_combine",
    )(pos, y_sorted, base, wts, gain.reshape(1, d).astype(F32))


def dispatch_tables(eidx, rank, counts, rows):
    n_tok = eidx.shape[0]
    n_exp = counts.shape[0]
    nb = n_tok * TOP_K // rows + n_exp
    pad_end = jnp.cumsum((counts + rows - 1) // rows * rows)
    pad_start = pad_end - (counts + rows - 1) // rows * rows
    experts = jnp.arange(n_exp, dtype=jnp.int32)
    pos = rank + jnp.sum(jnp.where(eidx[..., None] == experts, pad_start, 0), axis=-1)
    blocks = jnp.arange(nb, dtype=jnp.int32)
    block_expert = jnp.minimum(jnp.sum(pad_end[None, :] // rows <= blocks[:, None], axis=-1), n_exp - 1)
    n_used = pad_end[-1:] // rows
    i32 = lambda a: a.astype(jnp.int32)
    later = (experts[None, :] > experts[:, None]) & (counts[None, :] > 0)
    next_expert = jnp.where(jnp.any(later, axis=1), jnp.argmax(later, axis=1), -1)
    return (i32(pos).reshape(-1), i32(block_expert), i32(n_used), i32(next_expert), i32(pad_start), i32(pad_end),
            nb * rows)


def _ple_kernel(hn_ref, wg_ref, p_ref, wp_ref, h_ref, o_ref):
    gate = jax.nn.sigmoid(_dot(hn_ref[...], wg_ref[...]))
    o_ref[...] = h_ref[...] + gate * _dot(p_ref[...].astype(BF16), wp_ref[...])


def ple_gate(hn, w_gate, p, w_ple, h, bm=512, bn=1024):
    m, d = hn.shape
    n = w_gate.shape[1]
    pd = p.shape[1]
    bm, bn = _pick(m, bm), _pick(n, bn)
    return pl.pallas_call(
        _ple_kernel,
        out_shape=jax.ShapeDtypeStruct((m, n), F32),
        grid=(m // bm, n // bn),
        in_specs=[pl.BlockSpec((bm, d), lambda i, j: (i, 0)),
                  pl.BlockSpec((d, bn), lambda i, j: (0, j)),
                  pl.BlockSpec((bm, pd), lambda i, j: (i, 0)),
                  pl.BlockSpec((pd, bn), lambda i, j: (0, j)),
                  pl.BlockSpec((bm, bn), lambda i, j: (i, j))],
        out_specs=pl.BlockSpec((bm, bn), lambda i, j: (i, j)),
        compiler_params=_params(("parallel", "parallel")),
        name="ple_gate",
    )(hn, w_gate, p, w_ple, h)


def _layer(h, p_i, prm, batch):
    (mix_norm, w_in, a_re, a_im, log_step, b_re, b_im, c_re, c_im, ssm_d, w_glu_a, w_glu_b, q_norm, k_norm,
     sinks, w_bs, w_ba, w_out, moe_norm, w_router, router_bias, we_gate, we_up, we_down, ws_gate, ws_up,
     ws_down, ple_norm, w_ple, w_ple_gate) = prm
    ssm_w = w_glu_a.shape[0]
    n_q = sinks.shape[0]
    attn_w = n_q * HEAD_DIM
    kv_w = attn_w // Q_PER_KV
    n_exp = w_router.shape[1]
    c0, c1 = ssm_w, ssm_w + attn_w + 2 * kv_w
    bf = lambda w: w.astype(BF16)

    hn = rmsnorm(h, mix_norm, BF16)
    u = matmul(hn, bf(w_in[:, :c0]), F32, name="proj_u")
    qkv = matmul(hn, bf(w_in[:, c0:c1]), BF16, bn=(c1 - c0) // 2, name="proj_qkv")
    gates = matmul(hn, bf(w_in[:, c1:]), BF16, name="proj_gates")
    y_pre = s5_scan(u, batch, a_re, a_im, log_step, b_re, b_im, c_re, c_im, ssm_d)
    y_ssm = glu(y_pre, bf(w_glu_a), bf(w_glu_b))
    y_attn = swa_attention(qkv, batch, n_q, q_norm, k_norm, sinks)
    merged = branch_merge(y_ssm, bf(w_bs), y_attn, bf(w_ba), gates)
    h = matmul_residual(merged, bf(w_out), h)

    hm, hm_rows, eidx, ew, rank, counts = norm_and_route(h, moe_norm, bf(w_router), router_bias)
    counts = counts.reshape(n_exp)
    pos, block_expert, n_used, next_expert, pad_start, pad_end, cap = dispatch_tables(eidx, rank, counts, MOE_ROWS)
    x_sorted = dispatch_rows(hm_rows, pos, counts, pad_start, pad_end, cap, MOE_ROWS)
    base = shared_expert_residual(hm, jnp.concatenate([bf(ws_gate), bf(ws_up)], axis=1), bf(ws_down), h)
    y_sorted = routed_experts_sorted(x_sorted, block_expert, n_used, next_expert, we_gate, we_up, we_down, MOE_ROWS)
    h, hn3 = combine(pos, y_sorted, base, ew, ple_norm)

    return ple_gate(hn3, bf(w_ple_gate), p_i, bf(w_ple), h)


def kernel(x, p, mix_norm, w_in, ssm_a_re, ssm_a_im, ssm_log_step, ssm_b_re, ssm_b_im, ssm_c_re, ssm_c_im, ssm_d, w_glu_a, w_glu_b, q_norm, k_norm, attn_sinks, w_branch_ssm, w_branch_attn, w_out, moe_norm, w_router, router_bias, we_gate, we_up, we_down, ws_gate, ws_up, ws_down, ple_norm, w_ple, w_ple_gate):
    bsz, seq, d = x.shape
    layer_params = (mix_norm, w_in, ssm_a_re, ssm_a_im, ssm_log_step, ssm_b_re, ssm_b_im, ssm_c_re, ssm_c_im,
                    ssm_d, w_glu_a, w_glu_b, q_norm, k_norm, attn_sinks, w_branch_ssm, w_branch_attn, w_out,
                    moe_norm, w_router, router_bias, we_gate, we_up, we_down, ws_gate, ws_up, ws_down,
                    ple_norm, w_ple, w_ple_gate)
    h = x.reshape(bsz * seq, d)
    for i in range(mix_norm.shape[0]):
        h = _layer(h, p[i].reshape(bsz * seq, -1), tuple(w[i] for w in layer_params), bsz)
    return h.reshape(bsz, seq, d)
```

```python
import functools

import jax
import jax.numpy as jnp
from jax import lax
from jax.experimental import pallas as pl
from jax.experimental.pallas import tpu as pltpu

SSM_GROUP = 16
SSM_STATE = 64
S5_CHUNK = 8
S5_TILE_LANES = 128
S5_SEQS_PER_STEP = 2
HEAD_DIM = 64
Q_PER_KV = 8
ATTN_BLOCK = 128
ATTN_BLOCKS_PER_STEP = 2
ALIBI_MAX_BIAS = 8.0
N_EXPERT_GROUPS = 8
TOPK_GROUPS = 4
TOP_K = 8
ROUTED_SCALE = 2.5
EPS = 1e-6
MOE_ROWS = 256
COMBINE_TOKENS = 128
COMBINE_GROUP = 16
V7X_VMEM_LIMIT = 56 * 1024 * 1024
EXPERT_VMEM_LIMIT = 62 * 1024 * 1024

BF16 = jnp.bfloat16
F32 = jnp.float32


def _dot(a, b):
    return jnp.dot(a, b, preferred_element_type=F32)


def _params(sem, vmem=V7X_VMEM_LIMIT):
    return pltpu.CompilerParams(dimension_semantics=sem, vmem_limit_bytes=vmem)


def _pick(n, pref):
    b = min(n, pref)
    while n % b:
        b //= 2
    return b


def _rmsnorm_kernel(x_ref, g_ref, o_ref):
    x = x_ref[...]
    ms = jnp.mean(x * x, axis=-1, keepdims=True)
    o_ref[...] = (x * lax.rsqrt(ms + EPS) * g_ref[...]).astype(o_ref.dtype)


def rmsnorm(x, g, out_dtype, bm=256):
    m, d = x.shape
    bm = _pick(m, bm)
    return pl.pallas_call(
        _rmsnorm_kernel,
        out_shape=jax.ShapeDtypeStruct((m, d), out_dtype),
        grid=(m // bm,),
        in_specs=[pl.BlockSpec((bm, d), lambda i: (i, 0)),
                  pl.BlockSpec((1, d), lambda i: (0, 0))],
        out_specs=pl.BlockSpec((bm, d), lambda i: (i, 0)),
        compiler_params=_params(("parallel",)),
        name="rmsnorm",
    )(x, g.reshape(1, d).astype(F32))


def _mm_kernel(x_ref, w_ref, o_ref):
    o_ref[...] = _dot(x_ref[...], w_ref[...]).astype(o_ref.dtype)


def matmul(x, w, out_dtype, bm=1024, bn=1024, name="matmul"):
    m, k = x.shape
    n = w.shape[1]
    bm, bn = _pick(m, bm), _pick(n, bn)
    return pl.pallas_call(
        _mm_kernel,
        out_shape=jax.ShapeDtypeStruct((m, n), out_dtype),
        grid=(m // bm, n // bn),
        in_specs=[pl.BlockSpec((bm, k), lambda i, j: (i, 0)),
                  pl.BlockSpec((k, bn), lambda i, j: (0, j))],
        out_specs=pl.BlockSpec((bm, bn), lambda i, j: (i, j)),
        compiler_params=_params(("parallel", "parallel")),
        name=name,
    )(x, w)


def _spread_groups(compact, n_out, src_of, group_of, rows_per_group):
    n_in = compact.shape[1]
    k = lax.broadcasted_iota(jnp.int32, (n_in, n_out), 0)
    q = lax.broadcasted_iota(jnp.int32, (n_in, n_out), 1)
    spread = _dot(compact, jnp.where(src_of(q) == k, 1.0, 0.0).astype(BF16))
    r = lax.broadcasted_iota(jnp.int32, spread.shape, 0)
    q = lax.broadcasted_iota(jnp.int32, spread.shape, 1)
    return jnp.where(group_of(q) == r // rows_per_group, spread, 0.0).astype(BF16)


def _s5_chunk_matrices(kc_ref, fc_ref, ec_ref, w_s, e_s):
    tc, lanes, cg, ns = S5_CHUNK, S5_TILE_LANES, SSM_GROUP, SSM_STATE
    half = e_s.shape[0] // 2
    w_s[...] = jnp.zeros_like(w_s)
    for tau in range(tc):
        blk = _spread_groups(kc_ref[tau], lanes, lambda q: q % cg, lambda q: q // cg, cg)
        for i in range(tc - tau):
            w_s[i * lanes:(i + 1) * lanes, (i + tau) * lanes:(i + tau + 1) * lanes] = blk
    for i in range(tc):
        f = fc_ref[i]
        rows = slice(i * lanes, (i + 1) * lanes)
        w_s[rows, tc * lanes:tc * lanes + half] = _spread_groups(f, half, lambda q: q % ns, lambda q: q // ns, cg)
        w_s[rows, tc * lanes + half:] = _spread_groups(f, half, lambda q: q % ns + ns, lambda q: q // ns, cg)
    for part in range(2):
        e_s[part * half:(part + 1) * half, :] = _spread_groups(
            ec_ref[part], tc * lanes, lambda q: (q // lanes) * cg + q % cg, lambda q: (q // cg) % (lanes // cg), ns)


def _s5_kernel(u_ref, kc_ref, fc_ref, ec_ref, pw_ref, d_ref, o_ref, fu_ref, w_ref, e_ref):
    @pl.when(pl.program_id(1) == 0)
    def _():
        _s5_chunk_matrices(kc_ref, fc_ref, ec_ref, w_ref, e_ref)

    tc = S5_CHUNK
    seqs = fu_ref.shape[0]
    n_chunks = u_ref.shape[0] // seqs // tc
    lanes = u_ref.shape[1]
    half = fu_ref.shape[2] // 2
    row = lax.broadcasted_iota(jnp.int32, (8, half), 0)
    pw = [pw_ref[k] for k in range(10)]

    def cmul(ar, ai, xr, xi):
        return ar * xr - ai * xi, ar * xi + ai * xr

    def scan_tile(fu, t, carry):
        c_re, c_im = carry
        rows = slice(t * 8, (t + 1) * 8)
        x_re = fu[rows, 0:half]
        x_im = fu[rows, half:2 * half]
        for k, sh in enumerate((1, 2, 4)):
            s_re = jnp.where(row >= sh, pltpu.roll(x_re, sh, 0), 0.0)
            s_im = jnp.where(row >= sh, pltpu.roll(x_im, sh, 0), 0.0)
            m_re, m_im = cmul(pw[2 * k], pw[2 * k + 1], s_re, s_im)
            x_re, x_im = x_re + m_re, x_im + m_im
        p_re, p_im = cmul(pw[6], pw[7], c_re, c_im)
        fu[rows, 0:half] = jnp.where(row >= 1, pltpu.roll(x_re, 1, 0), 0.0) + p_re
        fu[rows, half:2 * half] = jnp.where(row >= 1, pltpu.roll(x_im, 1, 0), 0.0) + p_im
        f_re, f_im = cmul(pw[8], pw[9], c_re, c_im)
        f_re, f_im = f_re + x_re, f_im + x_im
        return jnp.broadcast_to(f_re[7:8], (8, half)), jnp.broadcast_to(f_im[7:8], (8, half))

    for q in range(seqs):
        first = q * n_chunks * tc
        x = jnp.concatenate([u_ref[pl.ds(first + i, n_chunks, stride=tc), :] for i in range(tc)], axis=1)
        z = _dot(x.astype(BF16), w_ref[...])
        fu = fu_ref.at[q]
        fu[...] = z[:, tc * lanes:]
        carry = (jnp.zeros((8, half), F32), jnp.zeros((8, half), F32))
        for t in range(n_chunks // 8):
            carry = scan_tile(fu, t, carry)
        y = z[:, :tc * lanes] + _dot(fu[...].astype(BF16), e_ref[...]) + d_ref[...] * x
        y = jax.nn.gelu(y)
        for j in range(tc):
            o_ref[pl.ds(first + j, n_chunks, stride=tc), :] = y[:, j * lanes:(j + 1) * lanes]


def s5_scan(u, batch, a_re, a_im, log_step, b_re, b_im, c_re, c_im, d_skip):
    rows, width = u.shape
    seq = rows // batch
    tc, lanes = S5_CHUNK, S5_TILE_LANES
    gpt = lanes // SSM_GROUP
    n_tiles = width // lanes
    half = gpt * SSM_STATE
    assert seq % (8 * tc) == 0 and width % lanes == 0
    seqs = _pick(batch, S5_SEQS_PER_STEP)
    lam = lax.complex(a_re.astype(F32), a_im.astype(F32))
    lam_dt = lam * jnp.exp(log_step.astype(F32))[:, None]
    b_bar = ((jnp.exp(lam_dt) - 1.0) / lam)[..., None] * lax.complex(b_re.astype(F32), b_im.astype(F32))
    c_mat = lax.complex(c_re.astype(F32), c_im.astype(F32))
    lp = jnp.exp(lam_dt[None] * jnp.arange(tc + 1, dtype=F32)[:, None, None])
    kern = jnp.real(jnp.einsum('gcn,tgn,gnd->tgcd', c_mat, lp[:tc], b_bar))
    kc = kern.transpose(1, 0, 3, 2).reshape(n_tiles, gpt, tc, SSM_GROUP, SSM_GROUP)
    kc = kc.transpose(0, 2, 1, 3, 4).reshape(n_tiles, tc, lanes, SSM_GROUP).astype(BF16)
    f_c = lp[tc - 1 - jnp.arange(tc)][..., None] * b_bar[None]
    f_t = lambda a: (a.transpose(1, 0, 3, 2).reshape(n_tiles, gpt, tc, SSM_GROUP, SSM_STATE)
                     .transpose(0, 2, 1, 3, 4).reshape(n_tiles, tc, lanes, SSM_STATE))
    fc = jnp.concatenate([f_t(jnp.real(f_c)), f_t(jnp.imag(f_c))], axis=-1).astype(BF16)
    e_c = c_mat[None] * lp[1:, :, None, :]
    e_t = lambda a: a.transpose(1, 3, 0, 2).reshape(n_tiles, half, tc * SSM_GROUP)
    ec = jnp.stack([e_t(jnp.real(e_c)), e_t(-jnp.imag(e_c))], axis=1).astype(BF16)
    r8 = jnp.arange(8, dtype=F32)[:, None, None]
    a_pow = lambda e: jnp.exp(lam_dt[None] * (tc * e))
    plist = [a_pow(jnp.full_like(r8, e)) for e in (1.0, 2.0, 4.0)] + [a_pow(r8), a_pow(r8 + 1.0)]
    pws = jnp.stack([f(p) for p in plist for f in (jnp.real, jnp.imag)])
    pws = pws.reshape(10, 8, n_tiles, half).transpose(2, 0, 1, 3)
    dsk = jnp.tile(d_skip.astype(F32).reshape(n_tiles, 1, lanes), (1, 1, tc))
    return pl.pallas_call(
        _s5_kernel,
        out_shape=jax.ShapeDtypeStruct((rows, width), F32),
        grid=(n_tiles, batch // seqs),
        in_specs=[pl.BlockSpec((seqs * seq, lanes), lambda s, b: (b, s)),
                  pl.BlockSpec((None, tc, lanes, SSM_GROUP), lambda s, b: (s, 0, 0, 0)),
                  pl.BlockSpec((None, tc, lanes, 2 * SSM_STATE), lambda s, b: (s, 0, 0, 0)),
                  pl.BlockSpec((None, 2, half, tc * SSM_GROUP), lambda s, b: (s, 0, 0, 0)),
                  pl.BlockSpec((None, 10, 8, half), lambda s, b: (s, 0, 0, 0)),
                  pl.BlockSpec((None, 1, tc * lanes), lambda s, b: (s, 0, 0))],
        out_specs=pl.BlockSpec((seqs * seq, lanes), lambda s, b: (b, s)),
        scratch_shapes=[pltpu.VMEM((seqs, seq // tc, 2 * half), F32),
                        pltpu.VMEM((tc * lanes, tc * lanes + 2 * half), BF16),
                        pltpu.VMEM((2 * half, tc * lanes), BF16)],
        compiler_params=_params(("parallel", "arbitrary")),
        name="s5_scan",
    )(u, kc, fc, ec, pws, dsk)


def _glu_kernel(x_ref, wa_ref, wb_ref, o_ref):
    x = x_ref[...].astype(BF16)
    o_ref[...] = (_dot(x, wa_ref[...]) * jax.nn.sigmoid(_dot(x, wb_ref[...]))).astype(o_ref.dtype)


def glu(x, wa, wb, bm=1024, bn=1024):
    m, k = x.shape
    n = wa.shape[1]
    bm, bn = _pick(m, bm), _pick(n, bn)
    return pl.pallas_call(
        _glu_kernel,
        out_shape=jax.ShapeDtypeStruct((m, n), BF16),
        grid=(m // bm, n // bn),
        in_specs=[pl.BlockSpec((bm, k), lambda i, j: (i, 0)),
                  pl.BlockSpec((k, bn), lambda i, j: (0, j)),
                  pl.BlockSpec((k, bn), lambda i, j: (0, j))],
        out_specs=pl.BlockSpec((bm, bn), lambda i, j: (i, j)),
        compiler_params=_params(("parallel", "parallel")),
        name="ssm_glu",
    )(x, wa, wb)


def _attn_kernel(sink_ref, q_ref, kc_ref, kp_ref, vc_ref, vp_ref, qg_ref, kg_ref, o_ref, *, n_kv):
    blk, hd = ATTN_BLOCK, HEAD_DIM
    n_q = n_kv * Q_PER_KV
    kj = lax.broadcasted_iota(jnp.int32, (2 * blk, blk), 0)
    qi = lax.broadcasted_iota(jnp.int32, (2 * blk, blk), 1)
    dist = qi + blk - kj
    in_window = (dist >= 0) & (dist < blk)
    no_prev = in_window & (kj >= jnp.where(pl.program_id(1) == 0, blk, 0))
    dist_f = dist.astype(F32)

    def head_norm_t(t, heads, gain):
        t3 = t.reshape(heads, hd, t.shape[1])
        ms = jnp.mean(t3 * t3, axis=1, keepdims=True)
        return t3 * lax.rsqrt(ms + EPS) * gain[None]

    for sub in range(q_ref.shape[0] // blk):
        own = slice(sub * blk, (sub + 1) * blk)
        before = slice((sub - 1) * blk, sub * blk)
        k_prev, v_prev = (kp_ref[...], vp_ref[...]) if sub == 0 else (kc_ref[before, :], vc_ref[before, :])
        valid = no_prev if sub == 0 else in_window
        qn = (head_norm_t(q_ref[own, :].astype(F32).T, n_q, qg_ref[...]) * (hd ** -0.5)).astype(BF16)
        kk = jnp.concatenate([k_prev, kc_ref[own, :]], axis=0).astype(F32)
        kn = head_norm_t(kk.T, n_kv, kg_ref[...]).reshape(n_kv * hd, 2 * blk).T.astype(BF16)
        vt = jnp.concatenate([v_prev, vc_ref[own, :]], axis=0).astype(F32).T.astype(BF16)
        for h in range(n_kv):
            qt = jnp.concatenate([qn[h * Q_PER_KV + g] for g in range(Q_PER_KV)], axis=1)
            st = _dot(kn[:, h * hd:(h + 1) * hd], qt)
            probs = []
            for g in range(Q_PER_KV):
                head = h * Q_PER_KV + g
                slope = 2.0 ** (-ALIBI_MAX_BIAS * (head + 1) / n_q)
                s = jnp.where(valid, st[:, g * blk:(g + 1) * blk] - slope * dist_f, -jnp.inf)
                sink = sink_ref[head]
                m = jnp.maximum(jnp.max(s, axis=0, keepdims=True), sink)
                p = jnp.exp(s - m)
                denom = jnp.sum(p, axis=0, keepdims=True) + jnp.exp(sink - m)
                probs.append((p * (1.0 / denom)).astype(BF16))
            ot = _dot(vt[h * hd:(h + 1) * hd, :], jnp.concatenate(probs, axis=1))
            ot = jnp.concatenate([ot[:, g * blk:(g + 1) * blk] for g in range(Q_PER_KV)], axis=0)
            o_ref[own, h * Q_PER_KV * hd:(h + 1) * Q_PER_KV * hd] = ot.T.astype(o_ref.dtype)


def swa_attention(qkv, batch, n_q, q_gain, k_gain, sinks):
    rows = qkv.shape[0]
    n_kv = n_q // Q_PER_KV
    qw, kw = n_q * HEAD_DIM, n_kv * HEAD_DIM
    blk = ATTN_BLOCK
    nb = rows // batch // blk
    per = _pick(nb, ATTN_BLOCKS_PER_STEP)
    kcol = qw // kw
    cur = lambda b, j: b * (nb // per) + j
    prev = lambda b, j: b * nb + jnp.maximum(j * per - 1, 0)
    return pl.pallas_call(
        functools.partial(_attn_kernel, n_kv=n_kv),
        out_shape=jax.ShapeDtypeStruct((rows, qw), BF16),
        grid=(batch, nb // per),
        in_specs=[pl.BlockSpec(memory_space=pltpu.SMEM),
                  pl.BlockSpec((per * blk, qw), lambda b, j: (cur(b, j), 0)),
                  pl.BlockSpec((per * blk, kw), lambda b, j: (cur(b, j), kcol)),
                  pl.BlockSpec((blk, kw), lambda b, j: (prev(b, j), kcol)),
                  pl.BlockSpec((per * blk, kw), lambda b, j: (cur(b, j), kcol + 1)),
                  pl.BlockSpec((blk, kw), lambda b, j: (prev(b, j), kcol + 1)),
                  pl.BlockSpec((HEAD_DIM, blk), lambda b, j: (0, 0)),
                  pl.BlockSpec((HEAD_DIM, 2 * blk), lambda b, j: (0, 0))],
        out_specs=pl.BlockSpec((per * blk, qw), lambda b, j: (cur(b, j), 0)),
        compiler_params=_params(("parallel", "arbitrary")),
        name="swa_attention",
    )(sinks.astype(F32), qkv, qkv, qkv, qkv, qkv,
      jnp.broadcast_to(q_gain.astype(F32)[:, None], (HEAD_DIM, blk)),
      jnp.broadcast_to(k_gain.astype(F32)[:, None], (HEAD_DIM, 2 * blk)))


def _merge_kernel(ys_ref, ws_ref, ya_ref, wa_ref, gs_ref, ga_ref, o_ref):
    s = jax.nn.sigmoid(gs_ref[...].astype(F32)) * _dot(ys_ref[...], ws_ref[...])
    a = jax.nn.sigmoid(ga_ref[...].astype(F32)) * _dot(ya_ref[...], wa_ref[...])
    o_ref[...] = (s + a).astype(o_ref.dtype)


def branch_merge(y_ssm, w_ssm, y_attn, w_attn, gates, bm=1024, bn=1024):
    m, ks = y_ssm.shape
    ka = y_attn.shape[1]
    n = w_ssm.shape[1]
    bm, bn = _pick(m, bm), _pick(n, bn)
    nj = n // bn
    return pl.pallas_call(
        _merge_kernel,
        out_shape=jax.ShapeDtypeStruct((m, n), BF16),
        grid=(m // bm, nj),
        in_specs=[pl.BlockSpec((bm, ks), lambda i, j: (i, 0)),
                  pl.BlockSpec((ks, bn), lambda i, j: (0, j)),
                  pl.BlockSpec((bm, ka), lambda i, j: (i, 0)),
                  pl.BlockSpec((ka, bn), lambda i, j: (0, j)),
                  pl.BlockSpec((bm, bn), lambda i, j: (i, j)),
                  pl.BlockSpec((bm, bn), lambda i, j: (i, nj + j))],
        out_specs=pl.BlockSpec((bm, bn), lambda i, j: (i, j)),
        compiler_params=_params(("parallel", "parallel")),
        name="branch_merge",
    )(y_ssm, w_ssm, y_attn, w_attn, gates, gates)


def _mm_res_kernel(x_ref, w_ref, r_ref, o_ref):
    o_ref[...] = r_ref[...] + _dot(x_ref[...], w_ref[...])


def matmul_residual(x, w, res, bm=1024, bn=1024):
    m, k = x.shape
    n = w.shape[1]
    bm, bn = _pick(m, bm), _pick(n, bn)
    return pl.pallas_call(
        _mm_res_kernel,
        out_shape=jax.ShapeDtypeStruct((m, n), F32),
        grid=(m // bm, n // bn),
        in_specs=[pl.BlockSpec((bm, k), lambda i, j: (i, 0)),
                  pl.BlockSpec((k, bn), lambda i, j: (0, j)),
                  pl.BlockSpec((bm, bn), lambda i, j: (i, j))],
        out_specs=pl.BlockSpec((bm, bn), lambda i, j: (i, j)),
        compiler_params=_params(("parallel", "parallel")),
        name="out_proj_residual",
    )(x, w, res)


def _router_kernel(h_ref, g_ref, w_ref, b_ref, hm_ref, rows_ref, idx_ref, wt_ref, rank_ref, cnt_ref, seen_ref,
                   *, n_exp):
    @pl.when(pl.program_id(0) == 0)
    def _():
        seen_ref[...] = jnp.zeros_like(seen_ref)

    h = h_ref[...]
    hn = h * lax.rsqrt(jnp.mean(h * h, axis=-1, keepdims=True) + EPS) * g_ref[...]
    hm = hn.astype(BF16)
    hm_ref[...] = hm
    rows_ref[...] = _pack_rows(hn)
    s = jax.nn.sigmoid(_dot(hm, w_ref[...]))
    sel = s + b_ref[...]
    rows = s.shape[0]
    lane_i = lax.broadcasted_iota(jnp.int32, (rows, n_exp), 1)
    per_group = n_exp // N_EXPERT_GROUPS
    lane = lane_i.astype(F32)
    grp = (lane_i // per_group).astype(F32)
    neg = -jnp.inf
    gscore = jnp.zeros_like(sel)
    for g in range(N_EXPERT_GROUPS):
        in_g = grp == g
        v = jnp.where(in_g, sel, neg)
        m1 = jnp.max(v, axis=-1, keepdims=True)
        i1 = jnp.min(jnp.where(v == m1, lane, n_exp), axis=-1, keepdims=True)
        m2 = jnp.max(jnp.where(lane == i1, neg, v), axis=-1, keepdims=True)
        gscore = jnp.where(in_g, m1 + m2, gscore)
    cand = jnp.full_like(sel, neg)
    remaining = gscore
    for _ in range(TOPK_GROUPS):
        gm = jnp.max(remaining, axis=-1, keepdims=True)
        gi = jnp.min(jnp.where(remaining == gm, grp, N_EXPERT_GROUPS), axis=-1, keepdims=True)
        hit = grp == gi
        cand = jnp.where(hit, sel, cand)
        remaining = jnp.where(hit, neg, remaining)
    slot = lax.broadcasted_iota(jnp.int32, (rows, TOP_K), 1)
    idx = jnp.zeros((rows, TOP_K), F32)
    wts = jnp.zeros((rows, TOP_K), F32)
    total = jnp.zeros((rows, 1), F32)
    picked = jnp.zeros_like(sel)
    hits = []
    for k in range(TOP_K):
        mx = jnp.max(cand, axis=-1, keepdims=True)
        ei = jnp.min(jnp.where(cand == mx, lane, n_exp), axis=-1, keepdims=True)
        hit = lane == ei
        wk = jnp.sum(jnp.where(hit, s, 0.0), axis=-1, keepdims=True)
        idx = jnp.where(slot == k, ei, idx)
        wts = jnp.where(slot == k, wk, wts)
        total = total + wk
        cand = jnp.where(hit, neg, cand)
        picked = jnp.where(hit, 1.0, picked)
        hits.append(hit)
    idx_ref[...] = idx.astype(jnp.int32)
    wt_ref[...] = wts / total * ROUTED_SCALE
    r_i = lax.broadcasted_iota(jnp.int32, (rows, rows), 0)
    c_i = lax.broadcasted_iota(jnp.int32, (rows, rows), 1)
    lower = jnp.where(c_i < r_i, 1.0, 0.0).astype(BF16)
    before = _dot(lower, picked.astype(BF16)) + seen_ref[...]
    rank = jnp.zeros((rows, TOP_K), F32)
    for k in range(TOP_K):
        rank = jnp.where(slot == k, jnp.sum(jnp.where(hits[k], before, 0.0), axis=-1, keepdims=True), rank)
    rank_ref[...] = rank.astype(jnp.int32)
    seen = seen_ref[...] + jnp.sum(picked, axis=0, keepdims=True)
    seen_ref[...] = seen
    cnt_ref[...] = seen.astype(jnp.int32)


def norm_and_route(h, gain, w_router, bias, bm=512):
    m, d = h.shape
    n_exp = w_router.shape[1]
    bm = _pick(m, bm)
    packed = jax.eval_shape(_pack_rows, jax.ShapeDtypeStruct((bm, d), F32))
    pw = packed.shape[1]
    tk = lambda dt: jax.ShapeDtypeStruct((m, TOP_K), dt)
    tk_spec = pl.BlockSpec((bm, TOP_K), lambda i: (i, 0))
    return pl.pallas_call(
        functools.partial(_router_kernel, n_exp=n_exp),
        out_shape=(jax.ShapeDtypeStruct((m, d), BF16), jax.ShapeDtypeStruct((m, pw), packed.dtype),
                   tk(jnp.int32), tk(F32), tk(jnp.int32), jax.ShapeDtypeStruct((1, n_exp), jnp.int32)),
        grid=(m // bm,),
        in_specs=[pl.BlockSpec((bm, d), lambda i: (i, 0)),
                  pl.BlockSpec((1, d), lambda i: (0, 0)),
                  pl.BlockSpec((d, n_exp), lambda i: (0, 0)),
                  pl.BlockSpec((1, n_exp), lambda i: (0, 0))],
        out_specs=(pl.BlockSpec((bm, d), lambda i: (i, 0)),
                   pl.BlockSpec((bm, pw), lambda i: (i, 0)),
                   tk_spec, tk_spec, tk_spec, pl.BlockSpec((1, n_exp), lambda i: (0, 0))),
        scratch_shapes=[pltpu.VMEM((1, n_exp), F32)],
        compiler_params=_params(("arbitrary",)),
        name="norm_and_route",
    )(h, gain.reshape(1, d).astype(F32), w_router, bias.reshape(1, n_exp).astype(F32))


def _swiglu(x, wgu, wd):
    ff = wd.shape[0]
    gu = _dot(x, wgu)
    act = (jax.nn.silu(gu[:, :ff]) * gu[:, ff:]).astype(BF16)
    return _dot(act, wd)


def _shared_kernel(x_ref, wgu_ref, wd_ref, r_ref, o_ref):
    o_ref[...] = r_ref[...] + _swiglu(x_ref[...], wgu_ref[...], wd_ref[...])


def shared_expert_residual(hm, wgu, wd, res, bm=256):
    m, d = hm.shape
    ff = wd.shape[0]
    bm = _pick(m, bm)
    return pl.pallas_call(
        _shared_kernel,
        out_shape=jax.ShapeDtypeStruct((m, d), F32),
        grid=(m // bm,),
        in_specs=[pl.BlockSpec((bm, d), lambda i: (i, 0)),
                  pl.BlockSpec((d, 2 * ff), lambda i: (0, 0)),
                  pl.BlockSpec((ff, d), lambda i: (0, 0)),
                  pl.BlockSpec((bm, d), lambda i: (i, 0))],
        out_specs=pl.BlockSpec((bm, d), lambda i: (i, 0)),
        compiler_params=_params(("parallel",)),
        name="shared_expert",
    )(hm, wgu, wd, res)


def _pack_rows(v):
    c = v.shape[1] // 2
    lo = lax.bitcast_convert_type(v[:, :c].astype(BF16).astype(F32), jnp.uint32)
    hi = lax.bitcast_convert_type(v[:, c:].astype(BF16).astype(F32), jnp.uint32)
    return hi | (lo >> 16)


def _unpack_rows(w):
    lo = lax.bitcast_convert_type(w << 16, F32)
    hi = lax.bitcast_convert_type(w & jnp.uint32(0xFFFF0000), F32)
    return lo, hi


def _row_copy(src, src_row, dst, dst_row, sem):
    return pltpu.make_async_copy(src.at[pl.ds(src_row, 1)], dst.at[pl.ds(dst_row, 1)], sem)


def _dispatch_kernel(cnt_ref, pstart_ref, pend_ref, pos_hbm, x_hbm, o_hbm,
                     idx_smem, xbuf, zbuf, idx_sem, load_sem, row_sem):
    i = pl.program_id(0)
    n_steps = pl.num_programs(0)
    n_slots, toks = xbuf.shape[0], xbuf.shape[1]
    per = toks * TOP_K

    def idx_copy(step):
        dst = idx_smem.at[pl.ds(pl.multiple_of((step % 2) * per, per), per)]
        return pltpu.make_async_copy(pos_hbm.at[step], dst, idx_sem.at[step % 2])

    def load(step):
        src = x_hbm.at[pl.ds(pl.multiple_of(step * toks, toks), toks)]
        return pltpu.make_async_copy(src, xbuf.at[step % n_slots], load_sem.at[step % n_slots])

    def wait_rows(step):
        whole = o_hbm.at[pl.ds(0, per)]
        pltpu.make_async_copy(whole, whole, row_sem.at[step % n_slots]).wait()

    @pl.when(i == 0)
    def _():
        zbuf[...] = jnp.zeros_like(zbuf)
        idx_copy(0).start()
        load(0).start()

    @pl.when(i >= 2)
    def _():
        wait_rows(i - 2)

    @pl.when(i + 1 < n_steps)
    def _():
        idx_copy(i + 1).start()
        load(i + 1).start()

    idx_copy(i).wait()
    load(i).wait()
    slot, islot = i % n_slots, i % 2

    def body(g, _):
        t0 = pl.multiple_of(g * 8, 8)
        for tt in range(8):
            for k in range(TOP_K):
                dst = idx_smem[islot * per + t0 * TOP_K + (tt * TOP_K + k)]
                _row_copy(xbuf.at[slot], t0 + tt, o_hbm, dst, row_sem.at[slot]).start(priority=k % 2)
        return 0

    lax.fori_loop(0, toks // 8, body, 0)

    @pl.when(i == n_steps - 1)
    def _():
        @pl.when(i >= 1)
        def _():
            wait_rows(i - 1)

        wait_rows(i)

        def expert_padding(e, _):
            first = pstart_ref[e] + cnt_ref[e]
            n_pad = pend_ref[e] - first

            def zbody(r, _):
                _row_copy(zbuf, 0, o_hbm, first + r, row_sem.at[0]).start()
                return 0

            def zwait(r, _):
                _row_copy(zbuf, 0, o_hbm, first, row_sem.at[0]).wait()
                return 0

            lax.fori_loop(0, n_pad, zbody, 0)
            lax.fori_loop(0, n_pad, zwait, 0)
            return 0

        lax.fori_loop(0, cnt_ref.shape[0], expert_padding, 0)

        blk_rows = zbuf.shape[0]
        n_exp = cnt_ref.shape[0]
        first_blk = pend_ref[n_exp - 1] // blk_rows
        n_tail = o_hbm.shape[0] // blk_rows - first_blk

        def tail_copy(b):
            dst = o_hbm.at[pl.ds(pl.multiple_of((first_blk + b) * blk_rows, blk_rows), blk_rows)]
            return pltpu.make_async_copy(zbuf, dst, row_sem.at[0])

        def tbody(b, _):
            tail_copy(b).start()
            return 0

        def twait(b, _):
            tail_copy(b).wait()
            return 0

        lax.fori_loop(0, n_tail, tbody, 0)
        lax.fori_loop(0, n_tail, twait, 0)


def dispatch_rows(x_rows, pos, counts, pad_start, pad_end, cap, blk_rows, chunk_tokens=512):
    n_tok, width = x_rows.shape
    toks = _pick(n_tok, chunk_tokens)
    steps = n_tok // toks
    n_slots = 3
    grid_spec = pltpu.PrefetchScalarGridSpec(
        num_scalar_prefetch=3,
        grid=(steps,),
        in_specs=[pl.BlockSpec(memory_space=pl.ANY)] * 2,
        out_specs=pl.BlockSpec(memory_space=pl.ANY),
        scratch_shapes=[pltpu.SMEM((2 * toks * TOP_K,), jnp.int32),
                        pltpu.VMEM((n_slots, toks, width), x_rows.dtype),
                        pltpu.VMEM((blk_rows, width), x_rows.dtype),
                        pltpu.SemaphoreType.DMA((2,)),
                        pltpu.SemaphoreType.DMA((n_slots,)),
                        pltpu.SemaphoreType.DMA((n_slots,))],
    )
    return pl.pallas_call(
        _dispatch_kernel,
        out_shape=jax.ShapeDtypeStruct((cap, width), x_rows.dtype),
        grid_spec=grid_spec,
        compiler_params=_params(("arbitrary",)),
        name="moe_dispatch",
    )(counts, pad_start, pad_end, pos.reshape(steps, toks * TOP_K), x_rows)


def _expert_kernel(bexp_ref, nused_ref, next_ref, x_ref, wg_hbm, wu_hbm, wd_hbm, o_ref,
                   sg_ref, su_ref, sd_ref, wgu_ref, wd_ref, sem):
    i = pl.program_id(0)
    e = bexp_ref[i]

    def fetches(ex):
        return [pltpu.make_async_copy(src.at[ex], dst, sem.at[s])
                for s, (src, dst) in enumerate(((wg_hbm, sg_ref), (wu_hbm, su_ref), (wd_hbm, sd_ref)))]

    def to_bf16(src, dst, col0=0):
        chunk = _pick(src.shape[0], 256)

        def body(r, _):
            rows = pl.ds(pl.multiple_of(r * chunk, chunk), chunk)
            dst[rows, pl.ds(col0, src.shape[1])] = src[rows, :].astype(BF16)
            return 0

        lax.fori_loop(0, src.shape[0] // chunk, body, 0)

    @pl.when(i < nused_ref[0])
    def _():
        first_block_of_expert = jnp.logical_or(i == 0, bexp_ref[jnp.maximum(i - 1, 0)] != e)

        @pl.when(first_block_of_expert)
        def _():
            @pl.when(i == 0)
            def _():
                for cp in fetches(e):
                    cp.start()

            for cp in fetches(e):
                cp.wait()
            to_bf16(sg_ref, wgu_ref)
            to_bf16(su_ref, wgu_ref, col0=sg_ref.shape[1])
            to_bf16(sd_ref, wd_ref)

            @pl.when(next_ref[e] >= 0)
            def _():
                for cp in fetches(next_ref[e]):
                    cp.start()

        lo, hi = _unpack_rows(x_ref[...])
        x = jnp.concatenate([lo, hi], axis=1).astype(BF16)
        o_ref[...] = _pack_rows(_swiglu(x, wgu_ref[...], wd_ref[...]))

    @pl.when(i >= nused_ref[0])
    def _():
        o_ref[...] = jnp.zeros_like(o_ref)


def routed_experts_sorted(x_sorted, block_expert, n_used, next_expert, wg, wu, wd, rows):
    cap, width = x_sorted.shape
    nb = cap // rows
    d, ff = wg.shape[1], wg.shape[2]
    used = lambda i, be, nu, nx: (jnp.maximum(jnp.minimum(i, nu[0] - 1), 0), 0)
    grid_spec = pltpu.PrefetchScalarGridSpec(
        num_scalar_prefetch=3,
        grid=(nb,),
        in_specs=[pl.BlockSpec((rows, width), used)] + [pl.BlockSpec(memory_space=pl.ANY)] * 3,
        out_specs=pl.BlockSpec((rows, width), lambda i, be, nu, nx: (i, 0)),
        scratch_shapes=[pltpu.VMEM((d, ff), wg.dtype), pltpu.VMEM((d, ff), wu.dtype), pltpu.VMEM((ff, d), wd.dtype),
                        pltpu.VMEM((d, 2 * ff), BF16), pltpu.VMEM((ff, d), BF16),
                        pltpu.SemaphoreType.DMA((3,))],
    )
    return pl.pallas_call(
        _expert_kernel,
        out_shape=jax.ShapeDtypeStruct((cap, width), x_sorted.dtype),
        grid_spec=grid_spec,
        compiler_params=_params(("arbitrary",), EXPERT_VMEM_LIMIT),
        name="routed_experts",
    )(block_expert, n_used, next_expert, x_sorted, wg, wu, wd)


def _combine_kernel(pos_hbm, y_hbm, base_ref, wt_ref, g_ref, h_ref, hn_ref, idx_smem, gbuf, idx_sem, row_sem):
    i = pl.program_id(0)
    n_steps = pl.num_programs(0)
    toks = gbuf.shape[2]
    per = toks * TOP_K
    grp = COMBINE_GROUP

    def fetch_indices(blk, slot):
        cp = pltpu.make_async_copy(pos_hbm.at[blk], idx_smem.at[pl.ds(pl.multiple_of(slot * per, per), per)], idx_sem)
        cp.start()
        cp.wait()

    def issue_group(slot, g):
        for r in range(g * grp, (g + 1) * grp):
            for k in range(TOP_K):
                _row_copy(y_hbm, idx_smem[slot * per + (r * TOP_K + k)], gbuf.at[slot, k], r,
                          row_sem.at[slot]).start(priority=k % 2)

    def combine_group(slot, g):
        rows = slice(g * grp, (g + 1) * grp)
        wt = wt_ref[rows, :]
        acc_lo, acc_hi = None, None
        for k in range(TOP_K):
            lo, hi = _unpack_rows(gbuf[slot, k, rows, :])
            wk = wt[:, k:k + 1]
            acc_lo = lo * wk if acc_lo is None else acc_lo + lo * wk
            acc_hi = hi * wk if acc_hi is None else acc_hi + hi * wk
        h = base_ref[rows, :] + jnp.concatenate([acc_lo, acc_hi], axis=1)
        h_ref[rows, :] = h
        ms = jnp.mean(h * h, axis=-1, keepdims=True)
        hn_ref[rows, :] = (h * lax.rsqrt(ms + EPS) * g_ref[...]).astype(hn_ref.dtype)

    @pl.when(i == 0)
    def _():
        fetch_indices(0, 0)
        for g in range(toks // grp):
            issue_group(0, g)

    slot = i % 2
    for k in range(TOP_K):
        pltpu.make_async_copy(y_hbm.at[pl.ds(0, toks)], gbuf.at[slot, k], row_sem.at[slot]).wait()

    @pl.when(i + 1 < n_steps)
    def _():
        fetch_indices(i + 1, 1 - slot)
        for g in range(toks // grp):
            issue_group(1 - slot, g)
            combine_group(slot, g)

    @pl.when(i + 1 >= n_steps)
    def _():
        for g in range(toks // grp):
            combine_group(slot, g)


def combine(pos, y_sorted, base, wts, gain, tokens=COMBINE_TOKENS):
    m, d = base.shape
    toks = _pick(m, tokens)
    steps, per = m // toks, toks * TOP_K
    pos = pos.reshape(steps, per)
    return pl.pallas_call(
        _combine_kernel,
        out_shape=(jax.ShapeDtypeStruct((m, d), F32), jax.ShapeDtypeStruct((m, d), BF16)),
        grid=(steps,),
        in_specs=[pl.BlockSpec(memory_space=pl.ANY),
                  pl.BlockSpec(memory_space=pl.ANY),
                  pl.BlockSpec((toks, d), lambda i: (i, 0)),
                  pl.BlockSpec((toks, TOP_K), lambda i: (i, 0)),
                  pl.BlockSpec((1, d), lambda i: (0, 0))],
        out_specs=(pl.BlockSpec((toks, d), lambda i: (i, 0)),
                   pl.BlockSpec((toks, d), lambda i: (i, 0))),
        scratch_shapes=[pltpu.SMEM((2 * per,), jnp.int32),
                        pltpu.VMEM((2, TOP_K, toks, y_sorted.shape[1]), y_sorted.dtype),
                        pltpu.SemaphoreType.DMA(()),
                        pltpu.SemaphoreType.DMA((2,))],
        compiler_params=_params(("arbitrary",)),
        name="moe_combine",
    )(pos, y_sorted, base, wts, gain.reshape(1, d).astype(F32))


def dispatch_tables(eidx, rank, counts, rows):
    n_tok = eidx.shape[0]
    n_exp = counts.shape[0]
    nb = n_tok * TOP_K // rows + n_exp
    pad_end = jnp.cumsum((counts + rows - 1) // rows * rows)
    pad_start = pad_end - (counts + rows - 1) // rows * rows
    experts = jnp.arange(n_exp, dtype=jnp.int32)
    pos = rank + jnp.sum(jnp.where(eidx[..., None] == experts, pad_start, 0), axis=-1)
    blocks = jnp.arange(nb, dtype=jnp.int32)
    block_expert = jnp.minimum(jnp.sum(pad_end[None, :] // rows <= blocks[:, None], axis=-1), n_exp - 1)
    n_used = pad_end[-1:] // rows
    i32 = lambda a: a.astype(jnp.int32)
    later = (experts[None, :] > experts[:, None]) & (counts[None, :] > 0)
    next_expert = jnp.where(jnp.any(later, axis=1), jnp.argmax(later, axis=1), -1)
    return (i32(pos).reshape(-1), i32(block_expert), i32(n_used), i32(next_expert), i32(pad_start), i32(pad_end),
            nb * rows)


def _ple_kernel(hn_ref, wg_ref, p_ref, wp_ref, h_ref, o_ref):
    gate = jax.nn.sigmoid(_dot(hn_ref[...], wg_ref[...]))
    o_ref[...] = h_ref[...] + gate * _dot(p_ref[...].astype(BF16), wp_ref[...])


def ple_gate(hn, w_gate, p, w_ple, h, bm=512, bn=1024):
    m, d = hn.shape
    n = w_gate.shape[1]
    pd = p.shape[1]
    bm, bn = _pick(m, bm), _pick(n, bn)
    return pl.pallas_call(
        _ple_kernel,
        out_shape=jax.ShapeDtypeStruct((m, n), F32),
        grid=(m // bm, n // bn),
        in_specs=[pl.BlockSpec((bm, d), lambda i, j: (i, 0)),
                  pl.BlockSpec((d, bn), lambda i, j: (0, j)),
                  pl.BlockSpec((bm, pd), lambda i, j: (i, 0)),
                  pl.BlockSpec((pd, bn), lambda i, j: (0, j)),
                  pl.BlockSpec((bm, bn), lambda i, j: (i, j))],
        out_specs=pl.BlockSpec((bm, bn), lambda i, j: (i, j)),
        compiler_params=_params(("parallel", "parallel")),
        name="ple_gate",
    )(hn, w_gate, p, w_ple, h)


def _layer(h, p_i, prm, batch):
    (mix_norm, w_in, a_re, a_im, log_step, b_re, b_im, c_re, c_im, ssm_d, w_glu_a, w_glu_b, q_norm, k_norm,
     sinks, w_bs, w_ba, w_out, moe_norm, w_router, router_bias, we_gate, we_up, we_down, ws_gate, ws_up,
     ws_down, ple_norm, w_ple, w_ple_gate) = prm
    ssm_w = w_glu_a.shape[0]
    n_q = sinks.shape[0]
    attn_w = n_q * HEAD_DIM
    kv_w = attn_w // Q_PER_KV
    n_exp = w_router.shape[1]
    c0, c1 = ssm_w, ssm_w + attn_w + 2 * kv_w
    bf = lambda w: w.astype(BF16)

    hn = rmsnorm(h, mix_norm, BF16)
    u = matmul(hn, bf(w_in[:, :c0]), F32, name="proj_u")
    qkv = matmul(hn, bf(w_in[:, c0:c1]), BF16, bn=(c1 - c0) // 2, name="proj_qkv")
    gates = matmul(hn, bf(w_in[:, c1:]), BF16, name="proj_gates")
    y_pre = s5_scan(u, batch, a_re, a_im, log_step, b_re, b_im, c_re, c_im, ssm_d)
    y_ssm = glu(y_pre, bf(w_glu_a), bf(w_glu_b))
    y_attn = swa_attention(qkv, batch, n_q, q_norm, k_norm, sinks)
    merged = branch_merge(y_ssm, bf(w_bs), y_attn, bf(w_ba), gates)
    h = matmul_residual(merged, bf(w_out), h)

    hm, hm_rows, eidx, ew, rank, counts = norm_and_route(h, moe_norm, bf(w_router), router_bias)
    counts = counts.reshape(n_exp)
    pos, block_expert, n_used, next_expert, pad_start, pad_end, cap = dispatch_tables(eidx, rank, counts, MOE_ROWS)
    x_sorted = dispatch_rows(hm_rows, pos, counts, pad_start, pad_end, cap, MOE_ROWS)
    base = shared_expert_residual(hm, jnp.concatenate([bf(ws_gate), bf(ws_up)], axis=1), bf(ws_down), h)
    y_sorted = routed_experts_sorted(x_sorted, block_expert, n_used, next_expert, we_gate, we_up, we_down, MOE_ROWS)
    h, hn3 = combine(pos, y_sorted, base, ew, ple_norm)

    return ple_gate(hn3, bf(w_ple_gate), p_i, bf(w_ple), h)


def kernel(x, p, mix_norm, w_in, ssm_a_re, ssm_a_im, ssm_log_step, ssm_b_re, ssm_b_im, ssm_c_re, ssm_c_im, ssm_d, w_glu_a, w_glu_b, q_norm, k_norm, attn_sinks, w_branch_ssm, w_branch_attn, w_out, moe_norm, w_router, router_bias, we_gate, we_up, we_down, ws_gate, ws_up, ws_down, ple_norm, w_ple, w_ple_gate):
    bsz, seq, d = x.shape
    layer_params = (mix_norm, w_in, ssm_a_re, ssm_a_im, ssm_log_step, ssm_b_re, ssm_b_im, ssm_c_re, ssm_c_im,
                    ssm_d, w_glu_a, w_glu_b, q_norm, k_norm, attn_sinks, w_branch_ssm, w_branch_attn, w_out,
                    moe_norm, w_router, router_bias, we_gate, we_up, we_down, ws_gate, ws_up, ws_down,
                    ple_norm, w_ple, w_ple_gate)
    h = x.reshape(bsz * seq, d)
    for i in range(mix_norm.shape[0]):
        h = _layer(h, p[i].reshape(bsz * seq, -1), tuple(w[i] for w in layer_params), bsz)
    return h.reshape(bsz, seq, d)
```
